```python
import jax, jax.numpy as jnp
from jax import lax
import numpy as np

D_MODEL = 2048
BATCH = 2
SEQ = 8192
DEPTH = 2
DEC_BATCH = 8
DEC_SEQ = 32
PAST_LEN = 4096

CHUNK = 64
N_A = DEPTH // 2
N_B = DEPTH - N_A
RW_HEAD = 64
RW_HEADS = D_MODEL // RW_HEAD
LORA_W = max(32, int(round(1.8 * D_MODEL ** 0.5 / 32)) * 32)
LORA_A = max(32, int(round(1.8 * D_MODEL ** 0.5 / 32)) * 32)
LORA_G = max(32, int(round(0.6 * D_MODEL ** 0.8 / 32)) * 32)
GN_EPS = 64e-5
SB_HEADS = 16
SB_HEAD_DIM = D_MODEL // SB_HEADS
D_ATT = SB_HEADS * SB_HEAD_DIM
SB_SCALE = SB_HEAD_DIM ** -0.5
Q_BLOCK = 128
D_FF = ((8 * D_MODEL // 3 + 255) // 256) * 256
CONV_W = 3
NORM_EPS = 1e-6

kernel_name = 'yoco_rwkv7_stickbreaking_convffn_step'


def rmsnorm(x, g):
    xf = x.astype(jnp.float32)
    y = xf * lax.rsqrt(jnp.mean(xf * xf, axis=-1, keepdims=True) + NORM_EPS)
    return (y * g.astype(jnp.float32)).astype(x.dtype)


def _wkv_step(S, inp):
    r, w, k, v, a, b = inp
    sa = jnp.einsum('bhij,bhj->bhi', S, a)
    S = S * w[:, :, None, :] + sa[..., None] * b[:, :, None, :] + v[..., None] * k[:, :, None, :]
    y = jnp.einsum('bhij,bhj->bhi', S, r)
    return S, y


def rwkv7_time_mix(xn, shift_prev, wkv_prev, p, i):
    B, T, _ = xn.shape
    f32 = jnp.float32
    x_prev = jnp.concatenate([shift_prev[:, None, :].astype(xn.dtype), xn[:, :-1]], axis=1)
    xx = x_prev - xn
    mu = p['rw_mu'][i]
    xr, xw, xk, xv, xa, xg = (xn + xx * mu[j] for j in range(6))
    r = xr @ p['rw_wr'][i]
    k = xk @ p['rw_wk'][i]
    v = xv @ p['rw_wv'][i]
    w_log = -jax.nn.softplus(-(p['rw_w0'][i] + jnp.tanh(xw @ p['rw_w1'][i]) @ p['rw_w2'][i])) - 0.5
    a = jax.nn.sigmoid(p['rw_a0'][i] + (xa @ p['rw_a1'][i]) @ p['rw_a2'][i])
    g = jax.nn.sigmoid(xg @ p['rw_g1'][i]) @ p['rw_g2'][i]
    heads = lambda t: t.astype(f32).reshape(B, T, RW_HEADS, RW_HEAD)
    kk = heads(k * p['rw_kk'][i])
    kk = kk / jnp.maximum(jnp.sqrt(jnp.sum(kk * kk, axis=-1, keepdims=True)), 1e-12)
    a_h = heads(a)
    k_h = heads(k * (1 + (a - 1) * p['rw_ka'][i]))
    r_h = heads(r)
    v_h = heads(v)
    decay = jnp.exp(-jnp.exp(heads(w_log)))
    tm = lambda t: jnp.swapaxes(t, 0, 1)
    S_T, ys = lax.scan(_wkv_step, wkv_prev.astype(f32),
                       (tm(r_h), tm(decay), tm(k_h), tm(v_h), tm(-kk), tm(kk * a_h)))
    y = tm(ys)
    mean = jnp.mean(y, axis=-1, keepdims=True)
    var = jnp.mean(jnp.square(y - mean), axis=-1, keepdims=True)
    y = ((y - mean) * lax.rsqrt(var + GN_EPS)).reshape(B, T, D_MODEL)
    y = y * p['rw_lnx_g'][i].astype(f32) + p['rw_lnx_b'][i].astype(f32)
    bonus = jnp.sum(r_h * k_h * p['rw_rk'][i].astype(f32), axis=-1, keepdims=True) * v_h
    o = (y + bonus.reshape(B, T, D_MODEL)).astype(xn.dtype) * g
    return o @ p['rw_wo'][i], xn[:, -1], S_T


def sb_block(q, q_pos, k, v, k_pos):
    z = jnp.einsum('bqhd,bkhd->bhqk', q, k).astype(jnp.float32) * SB_SCALE
    valid = k_pos[None, :] < q_pos[:, None]
    log_keep = jnp.where(valid, jax.nn.log_sigmoid(-z), 0.0)
    after = lax.cumsum(log_keep, axis=3, reverse=True) - log_keep
    wts = jnp.where(valid, jnp.exp(jax.nn.log_sigmoid(z) + after), 0.0)
    return jnp.einsum('bhqk,bkhd->bqhd', wts.astype(v.dtype), v)


def sb_attend(q, k, v, q_start):
    B, T, H, hd = q.shape
    k_pos = jnp.arange(k.shape[1])
    q_pos = q_start + jnp.arange(T)
    if T <= Q_BLOCK:
        return sb_block(q, q_pos, k, v, k_pos)
    nb = T // Q_BLOCK
    qb = jnp.swapaxes(q.reshape(B, nb, Q_BLOCK, H, hd), 0, 1)
    pb = q_pos.reshape(nb, Q_BLOCK)
    out = lax.map(lambda qp: sb_block(qp[0], qp[1], k, v, k_pos), (qb, pb))
    return jnp.swapaxes(out, 0, 1).reshape(B, T, H, hd)


def conv_ffn(xn, conv_prev, p, l):
    T = xn.shape[1]
    h = xn @ p['f_wup'][l]
    gate, val = h[..., :D_FF], h[..., D_FF:]
    ext = jnp.concatenate([conv_prev.astype(gate.dtype), gate], axis=1)
    cw = p['f_conv_w'][l]
    c = p['f_conv_b'][l] + cw[0] * ext[:, 0:T]
    for j in range(1, CONV_W):
        c = c + cw[j] * ext[:, j:j + T]
    y = jax.nn.silu(c) * val
    return y @ p['f_wdown'][l], ext[:, -(CONV_W - 1):]


def trunk(x, shift0, wkv0, conv0, k_past, v_past, q_start, p):
    B, T, _ = x.shape
    new_shift, new_wkv, new_conv = [], [], []
    k_sh = v_sh = None
    for l in range(DEPTH):
        if l < N_A:
            xn = rmsnorm(x, p['a_norm_g'][l])
            h, s_new, S_new = rwkv7_time_mix(xn, shift0[l], wkv0[l], p, l)
            x = x + h
            new_shift.append(s_new)
            new_wkv.append(S_new)
        else:
            i = l - N_A
            xn = rmsnorm(x, p['b_norm_g'][i])
            q = (xn @ p['sb_wq'][i]).reshape(B, T, SB_HEADS, SB_HEAD_DIM)
            if k_past is None:
                k_all, v_all = k_sh, v_sh
            else:
                k_all = jnp.concatenate([k_past.astype(k_sh.dtype), k_sh], axis=1)
                v_all = jnp.concatenate([v_past.astype(v_sh.dtype), v_sh], axis=1)
            o = sb_attend(q, k_all, v_all, q_start)
            x = x + o.reshape(B, T, D_ATT) @ p['sb_wo'][i]
        xn = rmsnorm(x, p['f_norm_g'][l])
        h, c_new = conv_ffn(xn, conv0[l], p, l)
        x = x + h
        new_conv.append(c_new)
        if l == N_A - 1:
            kv = rmsnorm(x, p['kv_norm_g']) @ p['w_kv']
            k_sh = kv[..., :D_ATT].reshape(B, T, SB_HEADS, SB_HEAD_DIM)
            v_sh = kv[..., D_ATT:].reshape(B, T, SB_HEADS, SB_HEAD_DIM)
    y = rmsnorm(x, p['out_norm_g'])
    return y, jnp.stack(new_wkv), jnp.stack(new_shift), jnp.stack(new_conv), k_sh, v_sh


def setup_inputs(seed: int = 0) -> dict:
    key = jax.random.key(seed)
    ks = iter(jax.random.split(key, 40))
    f32 = jnp.float32
    nrm = lambda shape, s: jax.random.normal(next(ks), shape, f32) * s
    gain = lambda shape: 1.0 + nrm(shape, 0.02)
    D, F = D_MODEL, D_FF
    return {
        'x_prompt': nrm((BATCH, SEQ, D), 1.0),
        'x_sample': nrm((DEC_BATCH, DEC_SEQ, D), 1.0),
        'cache_k': nrm((DEC_BATCH, PAST_LEN, SB_HEADS, SB_HEAD_DIM), 1.0),
        'cache_v': nrm((DEC_BATCH, PAST_LEN, SB_HEADS, SB_HEAD_DIM), 1.0),
        'state_wkv': nrm((N_A, DEC_BATCH, RW_HEADS, RW_HEAD, RW_HEAD), 0.5),
        'state_shift': nrm((N_A, DEC_BATCH, D), 1.0),
        'state_conv': nrm((DEPTH, DEC_BATCH, CONV_W - 1, F), 1.0),
        'a_norm_g': gain((N_A, D)),
        'rw_mu': jax.random.uniform(next(ks), (N_A, 6, D), f32),
        'rw_w0': jax.random.uniform(next(ks), (N_A, D), f32, -4.0, 0.5),
        'rw_w1': nrm((N_A, D, LORA_W), D ** -0.5),
        'rw_w2': nrm((N_A, LORA_W, D), 0.5 * LORA_W ** -0.5),
        'rw_a0': nrm((N_A, D), 0.1),
        'rw_a1': nrm((N_A, D, LORA_A), D ** -0.5),
        'rw_a2': nrm((N_A, LORA_A, D), 0.5 * LORA_A ** -0.5),
        'rw_g1': nrm((N_A, D, LORA_G), D ** -0.5),
        'rw_g2': nrm((N_A, LORA_G, D), LORA_G ** -0.5),
        'rw_kk': 0.85 + nrm((N_A, D), 0.05),
        'rw_ka': 1.0 + nrm((N_A, D), 0.05),
        'rw_rk': nrm((N_A, RW_HEADS, RW_HEAD), 0.1),
        'rw_wr': nrm((N_A, D, D), D ** -0.5),
        'rw_wk': nrm((N_A, D, D), D ** -0.5),
        'rw_wv': nrm((N_A, D, D), D ** -0.5),
        'rw_wo': nrm((N_A, D, D), D ** -0.5),
        'rw_lnx_g': gain((N_A, D)),
        'rw_lnx_b': nrm((N_A, D), 0.01),
        'kv_norm_g': gain((D,)),
        'w_kv': nrm((D, 2 * D_ATT), D ** -0.5),
        'b_norm_g': gain((N_B, D)),
        'sb_wq': nrm((N_B, D, D_ATT), D ** -0.5),
        'sb_wo': nrm((N_B, D_ATT, D), D_ATT ** -0.5),
        'f_norm_g': gain((DEPTH, D)),
        'f_wup': nrm((DEPTH, D, 2 * F), D ** -0.5),
        'f_conv_w': nrm((DEPTH, CONV_W, F), CONV_W ** -0.5),
        'f_conv_b': nrm((DEPTH, F), 0.01),
        'f_wdown': nrm((DEPTH, F, D), F ** -0.5),
        'out_norm_g': gain((D,)),
    }


def reference(x_prompt, x_sample, cache_k, cache_v, state_wkv, state_shift, state_conv,
              a_norm_g, rw_mu, rw_w0, rw_w1, rw_w2, rw_a0, rw_a1, rw_a2, rw_g1, rw_g2,
              rw_kk, rw_ka, rw_rk, rw_wr, rw_wk, rw_wv, rw_wo, rw_lnx_g, rw_lnx_b,
              kv_norm_g, w_kv, b_norm_g, sb_wq, sb_wo,
              f_norm_g, f_wup, f_conv_w, f_conv_b, f_wdown, out_norm_g):
    p = dict(a_norm_g=a_norm_g, rw_mu=rw_mu, rw_w0=rw_w0, rw_w1=rw_w1, rw_w2=rw_w2,
             rw_a0=rw_a0, rw_a1=rw_a1, rw_a2=rw_a2, rw_g1=rw_g1, rw_g2=rw_g2,
             rw_kk=rw_kk, rw_ka=rw_ka, rw_rk=rw_rk, rw_wr=rw_wr, rw_wk=rw_wk, rw_wv=rw_wv,
             rw_wo=rw_wo, rw_lnx_g=rw_lnx_g, rw_lnx_b=rw_lnx_b, kv_norm_g=kv_norm_g, w_kv=w_kv,
             b_norm_g=b_norm_g, sb_wq=sb_wq, sb_wo=sb_wo, f_norm_g=f_norm_g, f_wup=f_wup,
             f_conv_w=f_conv_w, f_conv_b=f_conv_b, f_wdown=f_wdown, out_norm_g=out_norm_g)
    B = x_prompt.shape[0]
    shift0 = jnp.zeros((N_A, B, D_MODEL), x_prompt.dtype)
    wkv0 = jnp.zeros((N_A, B, RW_HEADS, RW_HEAD, RW_HEAD), jnp.float32)
    conv0 = jnp.zeros((DEPTH, B, CONV_W - 1, D_FF), x_prompt.dtype)
    y_prompt, wkv_p, shift_p, conv_p, k_p, v_p = trunk(x_prompt, shift0, wkv0, conv0, None, None, 0, p)
    y_sample, wkv_s, shift_s, conv_s, k_s, v_s = trunk(x_sample, state_shift, state_wkv, state_conv,
                                                       cache_k, cache_v, PAST_LEN, p)
    wkv_p = wkv_p.astype(state_wkv.dtype)
    wkv_s = wkv_s.astype(state_wkv.dtype)
    return (y_prompt, y_sample, wkv_p, shift_p, conv_p, k_p, v_p, wkv_s, shift_s, conv_s, k_s, v_s)
```

```python
import functools

import jax
import jax.numpy as jnp
from jax import lax
from jax.experimental import pallas as pl
from jax.experimental.pallas import tpu as pltpu

F32 = jnp.float32
BF16 = jnp.bfloat16

RW_HEAD = 64
PAIR = 2 * RW_HEAD
SB_HEAD_DIM = 128
GN_EPS = 64e-5
NORM_EPS = 1e-6
CONV_W = 3
WKV_CHUNK = 64
NEUMANN_BLOCK = 16
VMEM_LIMIT = 56 * 1024 * 1024


def _tile(n, pref):
    if n <= pref:
        return n
    t = pref
    while n % t:
        t //= 2
    return t


def _params(sem):
    return pltpu.CompilerParams(dimension_semantics=sem, vmem_limit_bytes=VMEM_LIMIT)


def _dot(a, b):
    return jnp.dot(a.astype(BF16), b.astype(BF16), preferred_element_type=F32)


def _dot_nt(a, b):
    return lax.dot_general(a.astype(BF16), b.astype(BF16), (((1,), (1,)), ((), ())),
                           preferred_element_type=F32)


def _dot_tn(a, b):
    return lax.dot_general(a.astype(BF16), b.astype(BF16), (((0,), (0,)), ((), ())),
                           preferred_element_type=F32)


def _rms_scale(x):
    return x * lax.rsqrt(jnp.mean(x * x, axis=-1, keepdims=True) + NORM_EPS)


def _softplus(u):
    return jnp.maximum(u, 0.0) + jnp.log1p(jnp.exp(-jnp.abs(u)))


def _row_ids(shape):
    return lax.broadcasted_iota(jnp.int32, shape, 0)


def _col_ids(shape):
    return lax.broadcasted_iota(jnp.int32, shape, 1)


def _rw_proj_kernel(x_ref, shift0_ref, ng_ref, mu_ref, w0_ref, a0_ref,
                    wr_ref, wk_ref, wv_ref, w1_ref, w2_ref, a1_ref, a2_ref, g1_ref, g2_ref,
                    r_out, k_out, v_out, lw_out, as_out, g_out, shift_out,
                    xr_s, xk_s, xv_s, hw_s, ha_s, hg_s, carry_s):
    t = pl.program_id(1)
    j = pl.program_id(2)

    @pl.when(j == 0)
    def _():
        @pl.when(t == 0)
        def _():
            carry_s[...] = shift0_ref[...]

        xn = _rms_scale(x_ref[...]) * ng_ref[...]
        tm = xn.shape[0]
        rows = _row_ids(xn.shape)
        x_prev = jnp.where(rows == 0, carry_s[...], pltpu.roll(xn, 1, axis=0))
        last = xn[tm - 1:tm, :]
        carry_s[...] = last
        shift_out[...] = last
        xx = x_prev - xn
        xr_s[...] = (xn + xx * mu_ref[0:1, :]).astype(BF16)
        xk_s[...] = (xn + xx * mu_ref[2:3, :]).astype(BF16)
        xv_s[...] = (xn + xx * mu_ref[3:4, :]).astype(BF16)
        xw = xn + xx * mu_ref[1:2, :]
        xa = xn + xx * mu_ref[4:5, :]
        xg = xn + xx * mu_ref[5:6, :]
        hw_s[...] = jnp.tanh(_dot(xw, w1_ref[...])).astype(BF16)
        ha_s[...] = _dot(xa, a1_ref[...]).astype(BF16)
        hg_s[...] = jax.nn.sigmoid(_dot(xg, g1_ref[...])).astype(BF16)

    r_out[...] = _dot(xr_s[...], wr_ref[...])
    k_out[...] = _dot(xk_s[...], wk_ref[...])
    v_out[...] = _dot(xv_s[...], wv_ref[...])
    w_log = -_softplus(-(w0_ref[...] + _dot(hw_s[...], w2_ref[...]))) - 0.5
    lw_out[...] = -jnp.exp(w_log)
    as_out[...] = jax.nn.sigmoid(a0_ref[...] + _dot(ha_s[...], a2_ref[...]))
    g_out[...] = _dot(hg_s[...], g2_ref[...])


def _rw_proj(x, shift0, ng, mu, w0, a0, wr, wk, wv, w1, w2, a1, a2, g1, g2):
    B, T, D = x.shape
    tm = _tile(T, 512)
    tn = _tile(D, 256)
    lw_dim, la_dim, lg_dim = w1.shape[1], a1.shape[1], g1.shape[1]
    row = lambda b, t, j: (0, 0)
    colv = pl.BlockSpec((1, tn), lambda b, t, j: (0, j))
    act = pl.BlockSpec((None, tm, tn), lambda b, t, j: (b, t, j))
    out_sds = jax.ShapeDtypeStruct((B, T, D), F32)
    return pl.pallas_call(
        _rw_proj_kernel,
        grid=(B, T // tm, D // tn),
        in_specs=[
            pl.BlockSpec((None, tm, D), lambda b, t, j: (b, t, 0)),
            pl.BlockSpec((None, 1, D), lambda b, t, j: (b, 0, 0)),
            pl.BlockSpec((1, D), row),
            pl.BlockSpec((6, D), row),
            colv, colv,
            pl.BlockSpec((D, tn), lambda b, t, j: (0, j)),
            pl.BlockSpec((D, tn), lambda b, t, j: (0, j)),
            pl.BlockSpec((D, tn), lambda b, t, j: (0, j)),
            pl.BlockSpec((D, lw_dim), row),
            pl.BlockSpec((lw_dim, tn), lambda b, t, j: (0, j)),
            pl.BlockSpec((D, la_dim), row),
            pl.BlockSpec((la_dim, tn), lambda b, t, j: (0, j)),
            pl.BlockSpec((D, lg_dim), row),
            pl.BlockSpec((lg_dim, tn), lambda b, t, j: (0, j)),
        ],
        out_specs=[act] * 6 + [pl.BlockSpec((None, 1, D), lambda b, t, j: (b, 0, 0))],
        out_shape=[out_sds] * 6 + [jax.ShapeDtypeStruct((B, 1, D), F32)],
        scratch_shapes=[
            pltpu.VMEM((tm, D), BF16), pltpu.VMEM((tm, D), BF16), pltpu.VMEM((tm, D), BF16),
            pltpu.VMEM((tm, lw_dim), BF16), pltpu.VMEM((tm, la_dim), BF16),
            pltpu.VMEM((tm, lg_dim), BF16), pltpu.VMEM((1, D), F32),
        ],
        compiler_params=_params(("arbitrary", "arbitrary", "arbitrary")),
        name="rw_proj",
    )(x, shift0.reshape(B, 1, D), ng.reshape(1, D), mu, w0.reshape(1, D), a0.reshape(1, D),
      wr, wk, wv, w1, w2, a1, a2, g1, g2)


def _wkv_kernel(r_ref, lw_ref, k_ref, v_ref, as_ref, g_ref, s0_ref,
                kkw_ref, kaw_ref, rk_ref, lng_ref, lnb_ref,
                o_ref, s_ref, c_s, *, chunk, n_pairs):
    C = chunk
    C2 = 2 * C

    @pl.when(pl.program_id(1) == 0)
    def _():
        s_ref[...] = s0_ref[...]

    tri = (_col_ids((C, C)) <= _row_ids((C, C))).astype(F32)
    c_s[...] = jnp.dot(tri, lw_ref[...], precision=lax.Precision.HIGHEST,
                       preferred_element_type=F32)

    lane = _col_ids((C, PAIR))
    lane2 = _col_ids((C2, PAIR))
    head0 = lane < RW_HEAD
    head0_2 = lane2 < RW_HEAD
    ones_bd = ((_row_ids((PAIR, PAIR)) // RW_HEAD) == (_col_ids((PAIR, PAIR)) // RW_HEAD))
    ones_bd_bf = ones_bd.astype(BF16)

    rr = _row_ids((C2, C2))
    cc = _col_ids((C2, C2))
    same_head = (rr // C) == (cc // C)
    strict22 = (cc % C) < (rr % C)
    nb = NEUMANN_BLOCK
    same16 = (rr // nb) == (cc // nb)
    same32 = (rr // (2 * nb)) == (cc // (2 * nb))
    m_diag16 = same16 & strict22
    m_e1 = same32 & (~same16) & strict22 if C >= 2 * nb else None
    m_e2 = same_head & (~same32) & strict22 if C >= 4 * nb else None
    eye22 = (rr == cc).astype(F32)
    r12 = _row_ids((C, C2))
    c12 = _col_ids((C, C2))
    strict12 = (c12 % C) < r12
    incl12 = (c12 % C) <= r12
    left12 = c12 < C

    def group_sum(x):
        hi = x.astype(BF16)
        lo = (x - hi.astype(F32)).astype(BF16)
        return (jnp.dot(hi, ones_bd_bf, preferred_element_type=F32)
                + jnp.dot(lo, ones_bd_bf, preferred_element_type=F32))

    def pair_body(p, carry):
        sl = pl.ds(pl.multiple_of(p * PAIR, PAIR), PAIR)
        r = r_ref[:, sl]
        lw = lw_ref[:, sl]
        k = k_ref[:, sl]
        v = v_ref[:, sl]
        asig = as_ref[:, sl]
        c = c_s[:, sl]
        c_end = c[C - 1:C, :]

        kkv = k * kkw_ref[:, sl]
        kk = kkv / jnp.maximum(jnp.sqrt(group_sum(kkv * kkv)), 1e-12)
        b_in = kk * asig
        k_in = k * (1.0 + (asig - 1.0) * kaw_ref[:, sl])

        e_neg = jnp.exp(-c)
        a_t = (-kk) * jnp.exp(c - lw)
        r_t = r * jnp.exp(c)
        b_h = b_in * e_neg
        k_h = k_in * e_neg
        e_end = jnp.exp(c_end - c)
        b_e = b_in * e_end
        k_e = k_in * e_end

        L = jnp.concatenate([a_t, r_t], axis=0)
        L0 = jnp.where(head0_2, L, 0.0)
        L1 = jnp.where(head0_2, 0.0, L)
        R0 = jnp.concatenate([b_h, k_h], axis=0)
        R1 = jnp.concatenate([k_h, b_h], axis=0)
        G0 = _dot_nt(L0, R0)
        G1 = _dot_nt(L1, R1)
        top = jnp.concatenate([G0[:C], G1[:C]], axis=0)

        d1 = jnp.where(m_diag16, top, 0.0)
        d2 = _dot(d1, d1)
        d4 = _dot(d2, d2)
        d8 = _dot(d4, d4)
        tinv = _dot(_dot(eye22 + d1, eye22 + d2), _dot(eye22 + d4, eye22 + d8))
        if m_e1 is not None:
            e1 = jnp.where(m_e1, top, 0.0)
            tinv = tinv + _dot(_dot(tinv, e1), tinv)
        if m_e2 is not None:
            e2 = jnp.where(m_e2, top, 0.0)
            tinv = tinv + _dot(_dot(tinv, e2), tinv)
        tcat = tinv[:C] + tinv[C:]

        S = s_ref[p]
        P = _dot_nt(L, S)

        a_ak = jnp.where(strict12, jnp.where(left12, top[C:], top[:C]), 0.0)
        v0 = jnp.where(head0, v, 0.0)
        v1 = jnp.where(head0, 0.0, v)
        W = P[:C] + _dot(a_ak, jnp.concatenate([v1, v0], axis=0))
        w0 = jnp.where(head0, W, 0.0)
        w1 = jnp.where(head0, 0.0, W)
        U = _dot(tcat, jnp.concatenate([w0, w1], axis=0))
        u0 = jnp.where(head0, U, 0.0)
        u1 = jnp.where(head0, 0.0, U)

        bot0 = jnp.where(incl12, G0[C:], 0.0)
        bot1 = jnp.where(incl12, G1[C:], 0.0)
        Y = (P[C:] + _dot(bot0, jnp.concatenate([u0, v0], axis=0))
             + _dot(bot1, jnp.concatenate([v1, u1], axis=0)))

        UV = jnp.concatenate([U, v], axis=0)
        RE = jnp.concatenate([b_e, k_e], axis=0)
        s_new = S * jnp.exp(c_end) + jnp.where(ones_bd, _dot_tn(UV, RE), 0.0)
        s_ref[p] = s_new

        inv_n = 1.0 / RW_HEAD
        mean = group_sum(Y) * inv_n
        dlt = Y - mean
        var = group_sum(dlt * dlt) * inv_n
        yn = dlt * lax.rsqrt(var + GN_EPS) * lng_ref[:, sl] + lnb_ref[:, sl]
        bonus = group_sum(r * k_in * rk_ref[:, sl]) * v
        o_ref[:, sl] = ((yn + bonus) * g_ref[:, sl]).astype(o_ref.dtype)
        return carry

    lax.fori_loop(0, n_pairs, pair_body, 0)


def _wkv(r, lw, k, v, asig, g, s0_bd, kkw, kaw, rk, lng, lnb):
    B, T, D = r.shape
    C = _tile(T, WKV_CHUNK)
    n_pairs = D // PAIR
    act = pl.BlockSpec((None, C, D), lambda b, t: (b, t, 0))
    vec = pl.BlockSpec((1, D), lambda b, t: (0, 0))
    st = pl.BlockSpec((None, n_pairs, PAIR, PAIR), lambda b, t: (b, 0, 0, 0))
    return pl.pallas_call(
        functools.partial(_wkv_kernel, chunk=C, n_pairs=n_pairs),
        grid=(B, T // C),
        in_specs=[act] * 6 + [st] + [vec] * 5,
        out_specs=[act, st],
        out_shape=[jax.ShapeDtypeStruct((B, T, D), BF16),
                   jax.ShapeDtypeStruct((B, n_pairs, PAIR, PAIR), F32)],
        scratch_shapes=[pltpu.VMEM((C, D), F32)],
        compiler_params=_params(("arbitrary", "arbitrary")),
        name="wkv",
    )(r, lw, k, v, asig, g, s0_bd, kkw.reshape(1, D), kaw.reshape(1, D), rk.reshape(1, D),
      lng.reshape(1, D), lnb.reshape(1, D))


def _state_to_pairs(s):
    B, H, n, _ = s.shape
    s = s.reshape(B, H // 2, 2, n, n)
    z = jnp.zeros((B, H // 2, n, n), s.dtype)
    top = jnp.concatenate([s[:, :, 0], z], axis=-1)
    bot = jnp.concatenate([z, s[:, :, 1]], axis=-1)
    return jnp.concatenate([top, bot], axis=-2)


def _pairs_to_state(sp):
    B, P, _, _ = sp.shape
    n = RW_HEAD
    h0 = sp[:, :, :n, :n]
    h1 = sp[:, :, n:, n:]
    return jnp.stack([h0, h1], axis=2).reshape(B, 2 * P, n, n)


def _matmul_res_kernel(a_ref, w_ref, res_ref, o_ref):
    o_ref[...] = res_ref[...] + jnp.dot(a_ref[...], w_ref[...], preferred_element_type=F32)


def _matmul_res(a, w, res):
    M, K = a.shape
    N = w.shape[1]
    tm = _tile(M, 512)
    tn = _tile(N, 512)
    return pl.pallas_call(
        _matmul_res_kernel,
        grid=(M // tm, N // tn),
        in_specs=[pl.BlockSpec((tm, K), lambda i, j: (i, 0)),
                  pl.BlockSpec((K, tn), lambda i, j: (0, j)),
                  pl.BlockSpec((tm, tn), lambda i, j: (i, j))],
        out_specs=pl.BlockSpec((tm, tn), lambda i, j: (i, j)),
        out_shape=jax.ShapeDtypeStruct((M, N), F32),
        compiler_params=_params(("parallel", "parallel")),
        name="matmul_res",
    )(a, w, res)


def _ffn_kernel(x_ref, conv0_ref, ng_ref, wg_ref, wv_ref, cw_ref, cb_ref, wd_ref, og_ref,
                o_ref, conv_out, xn_s, acc_s, carry_s, *, final_norm):
    t = pl.program_id(1)
    f = pl.program_id(2)
    nf = pl.num_programs(2)

    @pl.when(f == 0)
    def _():
        xn_s[...] = (_rms_scale(x_ref[...]) * ng_ref[...]).astype(BF16)
        acc_s[...] = jnp.zeros_like(acc_s)

    @pl.when(t == 0)
    def _():
        carry_s[f] = conv0_ref[...]

    xn = xn_s[...]
    gate = jnp.dot(xn, wg_ref[...], preferred_element_type=F32)
    val = jnp.dot(xn, wv_ref[...], preferred_element_type=F32)
    tm = gate.shape[0]
    prev = carry_s[f]
    rows = _row_ids(gate.shape)
    g1 = jnp.where(rows == 0, prev[1:2, :], pltpu.roll(gate, 1, axis=0))
    g2 = jnp.where(rows == 0, prev[0:1, :],
                   jnp.where(rows == 1, prev[1:2, :], pltpu.roll(gate, 2, axis=0)))
    c = cb_ref[...] + cw_ref[0:1, :] * g2
    c = c + cw_ref[1:2, :] * g1
    c = c + cw_ref[2:3, :] * gate
    y = (c * jax.nn.sigmoid(c)) * val
    acc_s[...] += jnp.dot(y.astype(BF16), wd_ref[...], preferred_element_type=F32)
    tail = gate[tm - (CONV_W - 1):tm, :]
    carry_s[f] = tail
    conv_out[...] = tail

    @pl.when(f == nf - 1)
    def _():
        out = x_ref[...] + acc_s[...]
        if final_norm:
            out = _rms_scale(out) * og_ref[...]
        o_ref[...] = out


def _ffn(x, conv0, ng, wup, cw, cb, wdown, og, *, final_norm):
    B, T, D = x.shape
    F = wdown.shape[0]
    tm = _tile(T, 512)
    tf = _tile(F, 512)
    nf = F // tf
    row = lambda b, t, f: (0, 0)
    return pl.pallas_call(
        functools.partial(_ffn_kernel, final_norm=final_norm),
        grid=(B, T // tm, nf),
        in_specs=[
            pl.BlockSpec((None, tm, D), lambda b, t, f: (b, t, 0)),
            pl.BlockSpec((None, CONV_W - 1, tf), lambda b, t, f: (b, 0, f)),
            pl.BlockSpec((1, D), row),
            pl.BlockSpec((D, tf), lambda b, t, f: (0, f)),
            pl.BlockSpec((D, tf), lambda b, t, f: (0, f + nf)),
            pl.BlockSpec((CONV_W, tf), lambda b, t, f: (0, f)),
            pl.BlockSpec((1, tf), lambda b, t, f: (0, f)),
            pl.BlockSpec((tf, D), lambda b, t, f: (f, 0)),
            pl.BlockSpec((1, D), row),
        ],
        out_specs=[pl.BlockSpec((None, tm, D), lambda b, t, f: (b, t, 0)),
                   pl.BlockSpec((None, CONV_W - 1, tf), lambda b, t, f: (b, 0, f))],
        out_shape=[jax.ShapeDtypeStruct((B, T, D), F32),
                   jax.ShapeDtypeStruct((B, CONV_W - 1, F), F32)],
        scratch_shapes=[pltpu.VMEM((tm, D), BF16), pltpu.VMEM((tm, D), F32),
                        pltpu.VMEM((nf, CONV_W - 1, tf), F32)],
        compiler_params=_params(("arbitrary", "arbitrary", "arbitrary")),
        name="conv_ffn",
    )(x, conv0, ng.reshape(1, D), wup, wup, cw, cb.reshape(1, F), wdown, og.reshape(1, D))


def _kvq_kernel(x_ref, gkv_ref, gq_ref, wk_ref, wv_ref, wq_ref,
                k_out, v_out, kb_out, vb_out, q_out, xkv_s, xq_s):
    @pl.when(pl.program_id(1) == 0)
    def _():
        xh = _rms_scale(x_ref[...])
        xkv_s[...] = (xh * gkv_ref[...]).astype(BF16)
        xq_s[...] = (xh * gq_ref[...]).astype(BF16)

    xkv = xkv_s[...]
    k = jnp.dot(xkv, wk_ref[...], preferred_element_type=F32)
    v = jnp.dot(xkv, wv_ref[...], preferred_element_type=F32)
    k_out[...] = k
    v_out[...] = v
    kb_out[...] = k.astype(BF16)
    vb_out[...] = v.astype(BF16)
    q_out[...] = jnp.dot(xq_s[...], wq_ref[...], preferred_element_type=F32).astype(BF16)


def _kvq(x, gkv, gq, wkv, wq):
    M, D = x.shape
    N = wq.shape[1]
    tm = _tile(M, 512)
    tn = _tile(N, 512)
    nn = N // tn
    row = lambda i, j: (0, 0)
    blk = pl.BlockSpec((tm, tn), lambda i, j: (i, j))
    return pl.pallas_call(
        _kvq_kernel,
        grid=(M // tm, nn),
        in_specs=[pl.BlockSpec((tm, D), lambda i, j: (i, 0)),
                  pl.BlockSpec((1, D), row), pl.BlockSpec((1, D), row),
                  pl.BlockSpec((D, tn), lambda i, j: (0, j)),
                  pl.BlockSpec((D, tn), lambda i, j: (0, j + nn)),
                  pl.BlockSpec((D, tn), lambda i, j: (0, j))],
        out_specs=[blk] * 5,
        out_shape=[jax.ShapeDtypeStruct((M, N), F32), jax.ShapeDtypeStruct((M, N), F32),
                   jax.ShapeDtypeStruct((M, N), BF16), jax.ShapeDtypeStruct((M, N), BF16),
                   jax.ShapeDtypeStruct((M, N), BF16)],
        scratch_shapes=[pltpu.VMEM((tm, D), BF16), pltpu.VMEM((tm, D), BF16)],
        compiler_params=_params(("parallel", "arbitrary")),
        name="kvq_proj",
    )(x, gkv.reshape(1, D), gq.reshape(1, D), wkv, wkv, wq)


def _sb_block(q, kb, vb, tri, carry, scale, valid):
    z = _dot_nt(q, kb) * scale
    sp = _softplus(z)
    if valid is not None:
        sp = jnp.where(valid, sp, 0.0)
    later = jnp.dot(sp.astype(BF16), tri, preferred_element_type=F32)
    logw = z - sp - later
    if carry is not None:
        logw = logw - carry
    w = jnp.exp(logw)
    if valid is not None:
        w = jnp.where(valid, w, 0.0)
    out = jnp.dot(w.astype(BF16), vb.astype(BF16), preferred_element_type=F32)
    return out, later[:, 0:1] + sp[:, 0:1]


def _sb_kernel(*refs, tq, tk, n_past, scale):
    if n_past is None:
        q_ref, k_ref, v_ref, trid_ref, trip_ref, o_ref = refs
        kp_ref, vp_ref = k_ref, v_ref
        qi = pl.program_id(2)
        d0 = pl.multiple_of(qi * tq, tq)
        kd = k_ref[pl.ds(d0, tq), :]
        vd = v_ref[pl.ds(d0, tq), :]
        n_blocks = qi * (tq // tk)
    else:
        q_ref, k_ref, v_ref, kp_ref, vp_ref, trid_ref, trip_ref, o_ref = refs
        kd = k_ref[...]
        vd = v_ref[...]
        n_blocks = n_past

    q = q_ref[...]
    valid = _col_ids((tq, tq)) < _row_ids((tq, tq))
    acc, carry = _sb_block(q, kd, vd, trid_ref[...], None, scale, valid)
    trip = trip_ref[...]

    def body(i, st):
        acc, carry = st
        k0 = pl.multiple_of((n_blocks - 1 - i) * tk, tk)
        out, tot = _sb_block(q, kp_ref[pl.ds(k0, tk), :], vp_ref[pl.ds(k0, tk), :],
                             trip, carry, scale, None)
        return acc + out, carry + tot

    acc, carry = lax.fori_loop(0, n_blocks, body, (acc, carry))
    o_ref[...] = acc.astype(o_ref.dtype)


def _suffix_ones(n):
    return (_row_ids((n, n)) > _col_ids((n, n))).astype(BF16)


def _sb_attention(q, k, v, k_past=None, v_past=None):
    B, T, D = q.shape
    H = D // SB_HEAD_DIM
    hd = SB_HEAD_DIM
    tq = _tile(T, 256)
    scale = float(hd) ** -0.5
    qspec = pl.BlockSpec((None, tq, hd), lambda b, h, i: (b, i, h))
    seq = pl.BlockSpec((None, T, hd), lambda b, h, i: (b, 0, h))
    full = lambda n: pl.BlockSpec((n, n), lambda b, h, i: (0, 0))
    if k_past is None:
        tk = tq
        n_past = None
        ins = [q, k, v, _suffix_ones(tq), _suffix_ones(tk)]
        specs = [qspec, seq, seq, full(tq), full(tk)]
    else:
        P = k_past.shape[1]
        tk = _tile(P, 256)
        n_past = P // tk
        past = pl.BlockSpec((None, P, hd), lambda b, h, i: (b, 0, h))
        ins = [q, k, v, k_past, v_past, _suffix_ones(tq), _suffix_ones(tk)]
        specs = [qspec, seq, seq, past, past, full(tq), full(tk)]
    return pl.pallas_call(
        functools.partial(_sb_kernel, tq=tq, tk=tk, n_past=n_past, scale=scale),
        grid=(B, H, T // tq),
        in_specs=specs,
        out_specs=qspec,
        out_shape=jax.ShapeDtypeStruct((B, T, D), BF16),
        compiler_params=_params(("parallel", "parallel", "arbitrary")),
        name="sb_attention",
    )(*ins)


def _trunk(x, shift0, wkv0, conv0, k_past, v_past, p):
    B, T, D = x.shape
    M = B * T
    r, k, v, lw, asig, g, shift = _rw_proj(
        x, shift0[0], p['a_norm_g'][0], p['rw_mu'][0], p['rw_w0'][0], p['rw_a0'][0],
        p['rw_wr'], p['rw_wk'], p['rw_wv'], p['rw_w1'], p['rw_w2'], p['rw_a1'], p['rw_a2'],
        p['rw_g1'], p['rw_g2'])
    o, s_bd = _wkv(r, lw, k, v, asig, g, _state_to_pairs(wkv0[0]),
                   p['rw_kk'][0], p['rw_ka'][0], p['rw_rk'][0].reshape(-1),
                   p['rw_lnx_g'][0], p['rw_lnx_b'][0])
    x = _matmul_res(o.reshape(M, D), p['rw_wo'], x.reshape(M, D)).reshape(B, T, D)
    x, conv_a = _ffn(x, conv0[0], p['f_norm_g'][0], p['f_wup'][0], p['f_conv_w'][0],
                     p['f_conv_b'][0], p['f_wdown'][0], p['out_norm_g'], final_norm=False)
    k_sh, v_sh, kb, vb, q = _kvq(x.reshape(M, D), p['kv_norm_g'], p['b_norm_g'][0],
                                 p['w_kv'], p['sb_wq'])
    d_att = q.shape[1]
    att = _sb_attention(q.reshape(B, T, d_att), kb.reshape(B, T, d_att), vb.reshape(B, T, d_att),
                        k_past, v_past)
    x = _matmul_res(att.reshape(M, d_att), p['sb_wo'], x.reshape(M, D)).reshape(B, T, D)
    y, conv_b = _ffn(x, conv0[1], p['f_norm_g'][1], p['f_wup'][1], p['f_conv_w'][1],
                     p['f_conv_b'][1], p['f_wdown'][1], p['out_norm_g'], final_norm=True)
    H = d_att // SB_HEAD_DIM
    return (y, _pairs_to_state(s_bd)[None], shift.reshape(1, B, D), jnp.stack([conv_a, conv_b]),
            k_sh.reshape(B, T, H, SB_HEAD_DIM), v_sh.reshape(B, T, H, SB_HEAD_DIM))


def kernel(x_prompt, x_sample, cache_k, cache_v, state_wkv, state_shift, state_conv, a_norm_g, rw_mu, rw_w0, rw_w1, rw_w2, rw_a0, rw_a1, rw_a2, rw_g1, rw_g2, rw_kk, rw_ka, rw_rk, rw_wr, rw_wk, rw_wv, rw_wo, rw_lnx_g, rw_lnx_b, kv_norm_g, w_kv, b_norm_g, sb_wq, sb_wo, f_norm_g, f_wup, f_conv_w, f_conv_b, f_wdown, out_norm_g):
    bf = lambda w: w.astype(BF16)
    p = dict(a_norm_g=a_norm_g, rw_mu=rw_mu, rw_w0=rw_w0, rw_a0=rw_a0,
             rw_w1=bf(rw_w1[0]), rw_w2=bf(rw_w2[0]), rw_a1=bf(rw_a1[0]), rw_a2=bf(rw_a2[0]),
             rw_g1=bf(rw_g1[0]), rw_g2=bf(rw_g2[0]),
             rw_kk=rw_kk, rw_ka=rw_ka, rw_rk=rw_rk,
             rw_wr=bf(rw_wr[0]), rw_wk=bf(rw_wk[0]), rw_wv=bf(rw_wv[0]), rw_wo=bf(rw_wo[0]),
             rw_lnx_g=rw_lnx_g, rw_lnx_b=rw_lnx_b, kv_norm_g=kv_norm_g, w_kv=bf(w_kv),
             b_norm_g=b_norm_g, sb_wq=bf(sb_wq[0]), sb_wo=bf(sb_wo[0]), f_norm_g=f_norm_g,
             f_wup=bf(f_wup), f_conv_w=f_conv_w, f_conv_b=f_conv_b, f_wdown=bf(f_wdown),
             out_norm_g=out_norm_g)
    B, _, D = x_prompt.shape
    n_a = state_shift.shape[0]
    depth = state_conv.shape[0]
    F = state_conv.shape[-1]
    H = state_wkv.shape[2]
    shift0 = jnp.zeros((n_a, B, D), x_prompt.dtype)
    wkv0 = jnp.zeros((n_a, B, H, RW_HEAD, RW_HEAD), F32)
    conv0 = jnp.zeros((depth, B, CONV_W - 1, F), x_prompt.dtype)
    y_p, wkv_p, shift_p, conv_p, k_p, v_p = _trunk(x_prompt, shift0, wkv0, conv0, None, None, p)
    Bs, P = cache_k.shape[0], cache_k.shape[1]
    y_s, wkv_s, shift_s, conv_s, k_s, v_s = _trunk(
        x_sample, state_shift, state_wkv.astype(F32), state_conv,
        cache_k.reshape(Bs, P, -1), cache_v.reshape(Bs, P, -1), p)
    return (y_p, y_s, wkv_p.astype(state_wkv.dtype), shift_p, conv_p, k_p, v_p,
            wkv_s.astype(state_wkv.dtype), shift_s, conv_s, k_s, v_s)
```

```python
import functools

import jax
import jax.numpy as jnp
from jax import lax
from jax.experimental import pallas as pl
from jax.experimental.pallas import tpu as pltpu

F32 = jnp.float32
BF16 = jnp.bfloat16

RW_HEAD = 64
PAIR = 2 * RW_HEAD
SB_HEAD_DIM = 128
GN_EPS = 64e-5
NORM_EPS = 1e-6
CONV_W = 3
WKV_CHUNK = 64
NEUMANN_BLOCK = 16
WKV_PAIR_GROUP = 8
LOG2_E = 1.4426950408889634
VMEM_LIMIT = 56 * 1024 * 1024


def _tile(n, pref):
    if n <= pref:
        return n
    t = pref
    while n % t:
        t //= 2
    return t


def _params(sem):
    return pltpu.CompilerParams(dimension_semantics=sem, vmem_limit_bytes=VMEM_LIMIT)


def _dot(a, b):
    return jnp.dot(a.astype(BF16), b.astype(BF16), preferred_element_type=F32)


def _dot_nt(a, b):
    return lax.dot_general(a.astype(BF16), b.astype(BF16), (((1,), (1,)), ((), ())),
                           preferred_element_type=F32)


def _dot_tn(a, b):
    return lax.dot_general(a.astype(BF16), b.astype(BF16), (((0,), (0,)), ((), ())),
                           preferred_element_type=F32)


def _rms_scale(x):
    return x * lax.rsqrt(jnp.mean(x * x, axis=-1, keepdims=True) + NORM_EPS)


def _softplus(u):
    return jnp.maximum(u, 0.0) + jnp.log1p(jnp.exp(-jnp.abs(u)))


def _row_ids(shape):
    return lax.broadcasted_iota(jnp.int32, shape, 0)


def _col_ids(shape):
    return lax.broadcasted_iota(jnp.int32, shape, 1)


def _rw_proj_kernel(x_ref, shift0_ref, ng_ref, mu_ref, w0_ref, a0_ref,
                    wr_ref, wk_ref, wv_ref, w1_ref, w2_ref, a1_ref, a2_ref, g1_ref, g2_ref,
                    r_out, k_out, v_out, lw_out, as_out, g_out, shift_out,
                    xr_s, xk_s, xv_s, hw_s, ha_s, hg_s, carry_s):
    t = pl.program_id(1)
    j = pl.program_id(2)

    @pl.when(j == 0)
    def _():
        @pl.when(t == 0)
        def _():
            carry_s[...] = shift0_ref[...]

        xn = _rms_scale(x_ref[...]) * ng_ref[...]
        tm = xn.shape[0]
        rows = _row_ids(xn.shape)
        x_prev = jnp.where(rows == 0, carry_s[...], pltpu.roll(xn, 1, axis=0))
        last = xn[tm - 1:tm, :]
        carry_s[...] = last
        shift_out[...] = last
        xx = x_prev - xn
        xr_s[...] = (xn + xx * mu_ref[0:1, :]).astype(BF16)
        xk_s[...] = (xn + xx * mu_ref[2:3, :]).astype(BF16)
        xv_s[...] = (xn + xx * mu_ref[3:4, :]).astype(BF16)
        xw = xn + xx * mu_ref[1:2, :]
        xa = xn + xx * mu_ref[4:5, :]
        xg = xn + xx * mu_ref[5:6, :]
        hw_s[...] = jnp.tanh(_dot(xw, w1_ref[...])).astype(BF16)
        ha_s[...] = _dot(xa, a1_ref[...]).astype(BF16)
        hg_s[...] = jax.nn.sigmoid(_dot(xg, g1_ref[...])).astype(BF16)

    r_out[...] = _dot(xr_s[...], wr_ref[...])
    k_out[...] = _dot(xk_s[...], wk_ref[...])
    v_out[...] = _dot(xv_s[...], wv_ref[...])
    w_log = -_softplus(-(w0_ref[...] + _dot(hw_s[...], w2_ref[...]))) - 0.5
    lw_out[...] = -jnp.exp(w_log)
    as_out[...] = jax.nn.sigmoid(a0_ref[...] + _dot(ha_s[...], a2_ref[...]))
    g_out[...] = _dot(hg_s[...], g2_ref[...])


def _rw_proj(x, shift0, ng, mu, w0, a0, wr, wk, wv, w1, w2, a1, a2, g1, g2):
    B, T, D = x.shape
    tm = _tile(T, 512)
    tn = _tile(D, 256)
    lw_dim, la_dim, lg_dim = w1.shape[1], a1.shape[1], g1.shape[1]
    row = lambda b, t, j: (0, 0)
    colv = pl.BlockSpec((1, tn), lambda b, t, j: (0, j))
    act = pl.BlockSpec((None, tm, tn), lambda b, t, j: (b, t, j))
    out_sds = jax.ShapeDtypeStruct((B, T, D), F32)
    return pl.pallas_call(
        _rw_proj_kernel,
        grid=(B, T // tm, D // tn),
        in_specs=[
            pl.BlockSpec((None, tm, D), lambda b, t, j: (b, t, 0)),
            pl.BlockSpec((None, 1, D), lambda b, t, j: (b, 0, 0)),
            pl.BlockSpec((1, D), row),
            pl.BlockSpec((6, D), row),
            colv, colv,
            pl.BlockSpec((D, tn), lambda b, t, j: (0, j)),
            pl.BlockSpec((D, tn), lambda b, t, j: (0, j)),
            pl.BlockSpec((D, tn), lambda b, t, j: (0, j)),
            pl.BlockSpec((D, lw_dim), row),
            pl.BlockSpec((lw_dim, tn), lambda b, t, j: (0, j)),
            pl.BlockSpec((D, la_dim), row),
            pl.BlockSpec((la_dim, tn), lambda b, t, j: (0, j)),
            pl.BlockSpec((D, lg_dim), row),
            pl.BlockSpec((lg_dim, tn), lambda b, t, j: (0, j)),
        ],
        out_specs=[act] * 6 + [pl.BlockSpec((None, 1, D), lambda b, t, j: (b, 0, 0))],
        out_shape=[out_sds] * 6 + [jax.ShapeDtypeStruct((B, 1, D), F32)],
        scratch_shapes=[
            pltpu.VMEM((tm, D), BF16), pltpu.VMEM((tm, D), BF16), pltpu.VMEM((tm, D), BF16),
            pltpu.VMEM((tm, lw_dim), BF16), pltpu.VMEM((tm, la_dim), BF16),
            pltpu.VMEM((tm, lg_dim), BF16), pltpu.VMEM((1, D), F32),
        ],
        compiler_params=_params(("arbitrary", "arbitrary", "arbitrary")),
        name="rw_proj",
    )(x, shift0.reshape(B, 1, D), ng.reshape(1, D), mu, w0.reshape(1, D), a0.reshape(1, D),
      wr, wk, wv, w1, w2, a1, a2, g1, g2)


def _wkv_kernel(r_ref, lw_ref, k_ref, v_ref, as_ref, g_ref, s0_ref,
                kkw_ref, kaw_ref, rk_ref, lng_ref, lnb_ref,
                o_ref, s_ref, c_s, *, chunk, n_pairs):
    C = chunk
    C2 = 2 * C

    @pl.when(pl.program_id(1) == 0)
    def _():
        s_ref[...] = s0_ref[...]

    tri = (_col_ids((C, C)) <= _row_ids((C, C))).astype(F32)
    c_s[...] = jnp.dot(tri, lw_ref[...], precision=lax.Precision.HIGHEST,
                       preferred_element_type=F32)

    lane = _col_ids((C, PAIR))
    lane2 = _col_ids((C2, PAIR))
    head0 = lane < RW_HEAD
    head0_2 = lane2 < RW_HEAD
    ones_bd = ((_row_ids((PAIR, PAIR)) // RW_HEAD) == (_col_ids((PAIR, PAIR)) // RW_HEAD))
    ones_bd_bf = ones_bd.astype(BF16)

    rr = _row_ids((C2, C2))
    cc = _col_ids((C2, C2))
    same_head = (rr // C) == (cc // C)
    strict22 = (cc % C) < (rr % C)
    nb = NEUMANN_BLOCK
    same16 = (rr // nb) == (cc // nb)
    same32 = (rr // (2 * nb)) == (cc // (2 * nb))
    m_diag16 = same16 & strict22
    m_e1 = same32 & (~same16) & strict22 if C >= 2 * nb else None
    m_e2 = same_head & (~same32) & strict22 if C >= 4 * nb else None
    eye22 = (rr == cc).astype(F32)
    r12 = _row_ids((C, C2))
    c12 = _col_ids((C, C2))
    strict12 = (c12 % C) < r12
    incl12 = (c12 % C) <= r12
    left12 = c12 < C

    def group_sum(x):
        hi = x.astype(BF16)
        lo = (x - hi.astype(F32)).astype(BF16)
        both = jnp.dot(jnp.concatenate([hi, lo], axis=0), ones_bd_bf,
                       preferred_element_type=F32)
        return both[:C] + both[C:]

    cat0 = lambda xs: jnp.concatenate(xs, axis=0)
    inv_n = 1.0 / RW_HEAD

    def pair_group(ps):
        n = range(len(ps))
        sl = [slice(p * PAIR, (p + 1) * PAIR) for p in ps]
        r = [r_ref[:, s] for s in sl]
        lw = [lw_ref[:, s] for s in sl]
        k = [k_ref[:, s] for s in sl]
        v = [v_ref[:, s] for s in sl]
        asig = [as_ref[:, s] for s in sl]
        c = [c_s[:, s] for s in sl]
        c_end = [x[C - 1:C, :] for x in c]

        kkv = [k[i] * kkw_ref[:, sl[i]] for i in n]
        ss = [group_sum(x * x) for x in kkv]
        kk = [kkv[i] / jnp.maximum(jnp.sqrt(ss[i]), 1e-12) for i in n]
        b_in = [kk[i] * asig[i] for i in n]
        k_in = [k[i] * (1.0 + (asig[i] - 1.0) * kaw_ref[:, sl[i]]) for i in n]
        bonus_s = [group_sum(r[i] * k_in[i] * rk_ref[:, sl[i]]) for i in n]

        e_neg = [jnp.exp(-x) for x in c]
        a_t = [(-kk[i]) * jnp.exp(c[i] - lw[i]) for i in n]
        r_t = [r[i] * jnp.exp(c[i]) for i in n]
        b_h = [b_in[i] * e_neg[i] for i in n]
        k_h = [k_in[i] * e_neg[i] for i in n]
        e_end = [jnp.exp(c_end[i] - c[i]) for i in n]
        RE = [cat0([b_in[i] * e_end[i], k_in[i] * e_end[i]]) for i in n]

        L = [cat0([a_t[i], r_t[i]]) for i in n]
        G0 = [_dot_nt(jnp.where(head0_2, L[i], 0.0), cat0([b_h[i], k_h[i]])) for i in n]
        G1 = [_dot_nt(jnp.where(head0_2, 0.0, L[i]), cat0([k_h[i], b_h[i]])) for i in n]
        top = [cat0([G0[i][:C], G1[i][:C]]) for i in n]

        d1 = [jnp.where(m_diag16, x, 0.0) for x in top]
        d2 = [_dot(x, x) for x in d1]
        d4 = [_dot(x, x) for x in d2]
        d8 = [_dot(x, x) for x in d4]
        pa = [_dot(eye22 + d1[i], eye22 + d2[i]) for i in n]
        pb = [_dot(eye22 + d4[i], eye22 + d8[i]) for i in n]
        tinv = [_dot(pa[i], pb[i]) for i in n]
        for m_e in (m_e1, m_e2):
            if m_e is not None:
                x = [_dot(tinv[i], jnp.where(m_e, top[i], 0.0)) for i in n]
                tinv = [tinv[i] + _dot(x[i], tinv[i]) for i in n]
        tcat = [x[:C] + x[C:] for x in tinv]

        v0 = [jnp.where(head0, x, 0.0) for x in v]
        v1 = [jnp.where(head0, 0.0, x) for x in v]
        av = [_dot(jnp.where(strict12, jnp.where(left12, top[i][C:], top[i][:C]), 0.0),
                   cat0([v1[i], v0[i]])) for i in n]
        bot0 = [jnp.where(incl12, x[C:], 0.0) for x in G0]
        bot1 = [jnp.where(incl12, x[C:], 0.0) for x in G1]

        S = [s_ref[p] for p in ps]
        P = [_dot_nt(L[i], S[i]) for i in n]
        W = [P[i][:C] + av[i] for i in n]
        U = [_dot(tcat[i], cat0([jnp.where(head0, W[i], 0.0), jnp.where(head0, 0.0, W[i])]))
             for i in n]
        Y = [P[i][C:]
             + _dot(bot0[i], cat0([jnp.where(head0, U[i], 0.0), v0[i]]))
             + _dot(bot1[i], cat0([v1[i], jnp.where(head0, 0.0, U[i])])) for i in n]
        for i in n:
            upd = _dot_tn(cat0([U[i], v[i]]), RE[i])
            s_ref[ps[i]] = S[i] * jnp.exp(c_end[i]) + jnp.where(ones_bd, upd, 0.0)

        mean = [group_sum(x) * inv_n for x in Y]
        dlt = [Y[i] - mean[i] for i in n]
        var = [group_sum(x * x) * inv_n for x in dlt]
        for i in n:
            yn = dlt[i] * lax.rsqrt(var[i] + GN_EPS) * lng_ref[:, sl[i]] + lnb_ref[:, sl[i]]
            out = (yn + bonus_s[i] * v[i]) * g_ref[:, sl[i]]
            o_ref[:, sl[i]] = out.astype(o_ref.dtype)

    group = _tile(n_pairs, WKV_PAIR_GROUP)
    for g0 in range(0, n_pairs, group):
        pair_group(list(range(g0, g0 + group)))


def _wkv(r, lw, k, v, asig, g, s0_bd, kkw, kaw, rk, lng, lnb):
    B, T, D = r.shape
    C = _tile(T, WKV_CHUNK)
    n_pairs = D // PAIR
    act = pl.BlockSpec((None, C, D), lambda b, t: (b, t, 0))
    vec = pl.BlockSpec((1, D), lambda b, t: (0, 0))
    st = pl.BlockSpec((None, n_pairs, PAIR, PAIR), lambda b, t: (b, 0, 0, 0))
    return pl.pallas_call(
        functools.partial(_wkv_kernel, chunk=C, n_pairs=n_pairs),
        grid=(B, T // C),
        in_specs=[act] * 6 + [st] + [vec] * 5,
        out_specs=[act, st],
        out_shape=[jax.ShapeDtypeStruct((B, T, D), BF16),
                   jax.ShapeDtypeStruct((B, n_pairs, PAIR, PAIR), F32)],
        scratch_shapes=[pltpu.VMEM((C, D), F32)],
        compiler_params=_params(("arbitrary", "arbitrary")),
        name="wkv",
    )(r, lw, k, v, asig, g, s0_bd, kkw.reshape(1, D), kaw.reshape(1, D), rk.reshape(1, D),
      lng.reshape(1, D), lnb.reshape(1, D))


def _state_to_pairs(s):
    B, H, n, _ = s.shape
    s = s.reshape(B, H // 2, 2, n, n)
    z = jnp.zeros((B, H // 2, n, n), s.dtype)
    top = jnp.concatenate([s[:, :, 0], z], axis=-1)
    bot = jnp.concatenate([z, s[:, :, 1]], axis=-1)
    return jnp.concatenate([top, bot], axis=-2)


def _pairs_to_state(sp):
    B, P, _, _ = sp.shape
    n = RW_HEAD
    h0 = sp[:, :, :n, :n]
    h1 = sp[:, :, n:, n:]
    return jnp.stack([h0, h1], axis=2).reshape(B, 2 * P, n, n)


def _matmul_res_kernel(a_ref, w_ref, res_ref, o_ref):
    o_ref[...] = res_ref[...] + jnp.dot(a_ref[...], w_ref[...], preferred_element_type=F32)


def _matmul_res(a, w, res):
    M, K = a.shape
    N = w.shape[1]
    tm = _tile(M, 512)
    tn = _tile(N, 512)
    return pl.pallas_call(
        _matmul_res_kernel,
        grid=(M // tm, N // tn),
        in_specs=[pl.BlockSpec((tm, K), lambda i, j: (i, 0)),
                  pl.BlockSpec((K, tn), lambda i, j: (0, j)),
                  pl.BlockSpec((tm, tn), lambda i, j: (i, j))],
        out_specs=pl.BlockSpec((tm, tn), lambda i, j: (i, j)),
        out_shape=jax.ShapeDtypeStruct((M, N), F32),
        compiler_params=_params(("parallel", "parallel")),
        name="matmul_res",
    )(a, w, res)


def _ffn_kernel(x_ref, conv0_ref, ng_ref, wg_ref, wv_ref, cw_ref, cb_ref, wd_ref, og_ref,
                o_ref, conv_out, xn_s, acc_s, carry_s, *, final_norm):
    t = pl.program_id(1)
    f = pl.program_id(2)
    nf = pl.num_programs(2)

    @pl.when(f == 0)
    def _():
        xn_s[...] = (_rms_scale(x_ref[...]) * ng_ref[...]).astype(BF16)
        acc_s[...] = jnp.zeros_like(acc_s)

    @pl.when(t == 0)
    def _():
        carry_s[f] = conv0_ref[...]

    xn = xn_s[...]
    gate = jnp.dot(xn, wg_ref[...], preferred_element_type=F32)
    val = jnp.dot(xn, wv_ref[...], preferred_element_type=F32)
    tm = gate.shape[0]
    prev = carry_s[f]
    rows = _row_ids(gate.shape)
    g1 = jnp.where(rows == 0, prev[1:2, :], pltpu.roll(gate, 1, axis=0))
    g2 = jnp.where(rows == 0, prev[0:1, :],
                   jnp.where(rows == 1, prev[1:2, :], pltpu.roll(gate, 2, axis=0)))
    c = cb_ref[...] + cw_ref[0:1, :] * g2
    c = c + cw_ref[1:2, :] * g1
    c = c + cw_ref[2:3, :] * gate
    y = (c * jax.nn.sigmoid(c)) * val
    acc_s[...] += jnp.dot(y.astype(BF16), wd_ref[...], preferred_element_type=F32)
    tail = gate[tm - (CONV_W - 1):tm, :]
    carry_s[f] = tail
    tf = tail.shape[1]
    conv_out[:, pl.ds(pl.multiple_of(f * tf, tf), tf)] = tail

    @pl.when(f == nf - 1)
    def _():
        out = x_ref[...] + acc_s[...]
        if final_norm:
            out = _rms_scale(out) * og_ref[...]
        o_ref[...] = out


def _ffn(x, conv0, ng, wup, cw, cb, wdown, og, *, final_norm):
    B, T, D = x.shape
    F = wdown.shape[0]
    tm = _tile(T, 512)
    tf = _tile(F, 512)
    nf = F // tf
    row = lambda b, t, f: (0, 0)
    return pl.pallas_call(
        functools.partial(_ffn_kernel, final_norm=final_norm),
        grid=(B, T // tm, nf),
        in_specs=[
            pl.BlockSpec((None, tm, D), lambda b, t, f: (b, t, 0)),
            pl.BlockSpec((None, CONV_W - 1, tf), lambda b, t, f: (b, 0, f)),
            pl.BlockSpec((1, D), row),
            pl.BlockSpec((D, tf), lambda b, t, f: (0, f)),
            pl.BlockSpec((D, tf), lambda b, t, f: (0, f + nf)),
            pl.BlockSpec((CONV_W, tf), lambda b, t, f: (0, f)),
            pl.BlockSpec((1, tf), lambda b, t, f: (0, f)),
            pl.BlockSpec((tf, D), lambda b, t, f: (f, 0)),
            pl.BlockSpec((1, D), row),
        ],
        out_specs=[pl.BlockSpec((None, tm, D), lambda b, t, f: (b, t, 0)),
                   pl.BlockSpec((None, CONV_W - 1, F), lambda b, t, f: (b, 0, 0))],
        out_shape=[jax.ShapeDtypeStruct((B, T, D), F32),
                   jax.ShapeDtypeStruct((B, CONV_W - 1, F), F32)],
        scratch_shapes=[pltpu.VMEM((tm, D), BF16), pltpu.VMEM((tm, D), F32),
                        pltpu.VMEM((nf, CONV_W - 1, tf), F32)],
        compiler_params=_params(("arbitrary", "arbitrary", "arbitrary")),
        name="conv_ffn",
    )(x, conv0, ng.reshape(1, D), wup, wup, cw, cb.reshape(1, F), wdown, og.reshape(1, D))


def _kvq_kernel(x_ref, gkv_ref, gq_ref, wk_ref, wv_ref, wq_ref,
                k_out, v_out, kb_out, vb_out, q_out, xkv_s, xq_s, *, q_scale):
    @pl.when(pl.program_id(1) == 0)
    def _():
        xh = _rms_scale(x_ref[...])
        xkv_s[...] = (xh * gkv_ref[...]).astype(BF16)
        xq_s[...] = (xh * gq_ref[...]).astype(BF16)

    xkv = xkv_s[...]
    k = jnp.dot(xkv, wk_ref[...], preferred_element_type=F32)
    v = jnp.dot(xkv, wv_ref[...], preferred_element_type=F32)
    k_out[...] = k
    v_out[...] = v
    kb_out[...] = k.astype(BF16)
    vb_out[...] = v.astype(BF16)
    q = jnp.dot(xq_s[...], wq_ref[...], preferred_element_type=F32)
    q_out[...] = (q * q_scale).astype(BF16)


def _kvq(x, gkv, gq, wkv, wq):
    M, D = x.shape
    N = wq.shape[1]
    tm = _tile(M, 512)
    tn = _tile(N, 512)
    nn = N // tn
    row = lambda i, j: (0, 0)
    blk = pl.BlockSpec((tm, tn), lambda i, j: (i, j))
    return pl.pallas_call(
        functools.partial(_kvq_kernel, q_scale=float(SB_HEAD_DIM) ** -0.5 * LOG2_E),
        grid=(M // tm, nn),
        in_specs=[pl.BlockSpec((tm, D), lambda i, j: (i, 0)),
                  pl.BlockSpec((1, D), row), pl.BlockSpec((1, D), row),
                  pl.BlockSpec((D, tn), lambda i, j: (0, j)),
                  pl.BlockSpec((D, tn), lambda i, j: (0, j + nn)),
                  pl.BlockSpec((D, tn), lambda i, j: (0, j))],
        out_specs=[blk] * 5,
        out_shape=[jax.ShapeDtypeStruct((M, N), F32), jax.ShapeDtypeStruct((M, N), F32),
                   jax.ShapeDtypeStruct((M, N), BF16), jax.ShapeDtypeStruct((M, N), BF16),
                   jax.ShapeDtypeStruct((M, N), BF16)],
        scratch_shapes=[pltpu.VMEM((tm, D), BF16), pltpu.VMEM((tm, D), BF16)],
        compiler_params=_params(("parallel", "arbitrary")),
        name="kvq_proj",
    )(x, gkv.reshape(1, D), gq.reshape(1, D), wkv, wkv, wq)


def _sb_block(q, kb, vb, tri, carry, valid):
    z = _dot_nt(q, kb)
    sp = jnp.maximum(z, 0.0) + jnp.log2(1.0 + jnp.exp2(jnp.minimum(z, -z)))
    if valid is not None:
        sp = jnp.where(valid, sp, 0.0)
    later = jnp.dot(sp.astype(BF16), tri, preferred_element_type=F32)
    logw = z - sp - later
    if carry is not None:
        logw = logw - carry
    w = jnp.exp2(logw)
    if valid is not None:
        w = jnp.where(valid, w, 0.0)
    out = jnp.dot(w.astype(BF16), vb.astype(BF16), preferred_element_type=F32)
    return out, later[:, 0:1] + sp[:, 0:1]


def _sb_prompt_kernel(q_ref, k_ref, v_ref, tri_ref, o_ref, acc_s, carry_s, *, nsub, tb):
    qi = pl.program_id(2)
    base = qi * (nsub * tb)
    tri = tri_ref[...]
    valid = _col_ids((tb, tb)) < _row_ids((tb, tb))
    rows = lambda i: slice(i * tb, (i + 1) * tb)

    for j in reversed(range(nsub)):
        k0 = pl.multiple_of(base + j * tb, tb)
        kb = k_ref[pl.ds(k0, tb), :]
        vb = v_ref[pl.ds(k0, tb), :]
        out, tot = _sb_block(q_ref[rows(j), :], kb, vb, tri, None, valid)
        acc_s[j] = out
        carry_s[j] = tot
        for i in range(j + 1, nsub):
            out, tot = _sb_block(q_ref[rows(i), :], kb, vb, tri, carry_s[i], None)
            acc_s[i] += out
            carry_s[i] += tot

    n_blocks = qi * nsub

    def body(t, c):
        k0 = pl.multiple_of((n_blocks - 1 - t) * tb, tb)
        kb = k_ref[pl.ds(k0, tb), :]
        vb = v_ref[pl.ds(k0, tb), :]
        for i in range(nsub):
            out, tot = _sb_block(q_ref[rows(i), :], kb, vb, tri, carry_s[i], None)
            acc_s[i] += out
            carry_s[i] += tot
        return c

    lax.fori_loop(0, n_blocks, body, 0)
    for i in range(nsub):
        o_ref[rows(i), :] = acc_s[i].astype(o_ref.dtype)


def _sb_decode_kernel(q_ref, k_ref, v_ref, kp_ref, vp_ref, trid_ref, trip_ref, o_ref,
                      *, tq, tk, n_past, unroll):
    q = q_ref[...]
    valid = _col_ids((tq, tq)) < _row_ids((tq, tq))
    acc, carry = _sb_block(q, k_ref[...], v_ref[...], trid_ref[...], None, valid)
    trip = trip_ref[...]

    def body(t, st):
        acc, carry = st
        k0 = pl.multiple_of((n_past - 1 - t) * tk, tk)
        out, tot = _sb_block(q, kp_ref[pl.ds(k0, tk), :], vp_ref[pl.ds(k0, tk), :],
                             trip, carry, None)
        return acc + out, carry + tot

    acc, carry = lax.fori_loop(0, n_past, body, (acc, carry), unroll=unroll)
    o_ref[...] = acc.astype(o_ref.dtype)


def _suffix_ones(n):
    return (_row_ids((n, n)) > _col_ids((n, n))).astype(BF16)


SB_BLOCK = 256
SB_QSUB = 4


def _sb_attention(q, k, v, k_past=None, v_past=None):
    B, T, D = q.shape
    H = D // SB_HEAD_DIM
    hd = SB_HEAD_DIM
    full = lambda n: pl.BlockSpec((n, n), lambda b, h, i: (0, 0))
    seq = pl.BlockSpec((None, T, hd), lambda b, h, i: (b, 0, h))
    if k_past is None:
        tb = _tile(T, SB_BLOCK)
        nsub = _tile(T // tb, SB_QSUB)
        tq = nsub * tb
        qspec = pl.BlockSpec((None, tq, hd), lambda b, h, i: (b, i, h))
        return pl.pallas_call(
            functools.partial(_sb_prompt_kernel, nsub=nsub, tb=tb),
            grid=(B, H, T // tq),
            in_specs=[qspec, seq, seq, full(tb)],
            out_specs=qspec,
            out_shape=jax.ShapeDtypeStruct((B, T, D), BF16),
            scratch_shapes=[pltpu.VMEM((nsub, tb, hd), F32), pltpu.VMEM((nsub, tb, 1), F32)],
            compiler_params=_params(("parallel", "parallel", "arbitrary")),
            name="sb_attention",
        )(q, k, v, _suffix_ones(tb))
    P = k_past.shape[1]
    tk = _tile(P, SB_BLOCK)
    n_past = P // tk
    qspec = pl.BlockSpec((None, T, hd), lambda b, h, i: (b, 0, h))
    past = pl.BlockSpec((None, P, hd), lambda b, h, i: (b, 0, h))
    return pl.pallas_call(
        functools.partial(_sb_decode_kernel, tq=T, tk=tk, n_past=n_past,
                          unroll=_tile(n_past, 4)),
        grid=(B, H, 1),
        in_specs=[qspec, seq, seq, past, past, full(T), full(tk)],
        out_specs=qspec,
        out_shape=jax.ShapeDtypeStruct((B, T, D), BF16),
        compiler_params=_params(("parallel", "parallel", "arbitrary")),
        name="sb_attention_decode",
    )(q, k, v, k_past, v_past, _suffix_ones(T), _suffix_ones(tk))


def _trunk(x, shift0, wkv0, conv0, k_past, v_past, p):
    B, T, D = x.shape
    M = B * T
    r, k, v, lw, asig, g, shift = _rw_proj(
        x, shift0[0], p['a_norm_g'][0], p['rw_mu'][0], p['rw_w0'][0], p['rw_a0'][0],
        p['rw_wr'], p['rw_wk'], p['rw_wv'], p['rw_w1'], p['rw_w2'], p['rw_a1'], p['rw_a2'],
        p['rw_g1'], p['rw_g2'])
    o, s_bd = _wkv(r, lw, k, v, asig, g, _state_to_pairs(wkv0[0]),
                   p['rw_kk'][0], p['rw_ka'][0], p['rw_rk'][0].reshape(-1),
                   p['rw_lnx_g'][0], p['rw_lnx_b'][0])
    x = _matmul_res(o.reshape(M, D), p['rw_wo'], x.reshape(M, D)).reshape(B, T, D)
    x, conv_a = _ffn(x, conv0[0], p['f_norm_g'][0], p['f_wup'][0], p['f_conv_w'][0],
                     p['f_conv_b'][0], p['f_wdown'][0], p['out_norm_g'], final_norm=False)
    k_sh, v_sh, kb, vb, q = _kvq(x.reshape(M, D), p['kv_norm_g'], p['b_norm_g'][0],
                                 p['w_kv'], p['sb_wq'])
    d_att = q.shape[1]
    att = _sb_attention(q.reshape(B, T, d_att), kb.reshape(B, T, d_att), vb.reshape(B, T, d_att),
                        k_past, v_past)
    x = _matmul_res(att.reshape(M, d_att), p['sb_wo'], x.reshape(M, D)).reshape(B, T, D)
    y, conv_b = _ffn(x, conv0[1], p['f_norm_g'][1], p['f_wup'][1], p['f_conv_w'][1],
                     p['f_conv_b'][1], p['f_wdown'][1], p['out_norm_g'], final_norm=True)
    H = d_att // SB_HEAD_DIM
    return (y, _pairs_to_state(s_bd)[None], shift.reshape(1, B, D), jnp.stack([conv_a, conv_b]),
            k_sh.reshape(B, T, H, SB_HEAD_DIM), v_sh.reshape(B, T, H, SB_HEAD_DIM))


def kernel(x_prompt, x_sample, cache_k, cache_v, state_wkv, state_shift, state_conv, a_norm_g, rw_mu, rw_w0, rw_w1, rw_w2, rw_a0, rw_a1, rw_a2, rw_g1, rw_g2, rw_kk, rw_ka, rw_rk, rw_wr, rw_wk, rw_wv, rw_wo, rw_lnx_g, rw_lnx_b, kv_norm_g, w_kv, b_norm_g, sb_wq, sb_wo, f_norm_g, f_wup, f_conv_w, f_conv_b, f_wdown, out_norm_g):
    bf = lambda w: w.astype(BF16)
    p = dict(a_norm_g=a_norm_g, rw_mu=rw_mu, rw_w0=rw_w0, rw_a0=rw_a0,
             rw_w1=bf(rw_w1[0]), rw_w2=bf(rw_w2[0]), rw_a1=bf(rw_a1[0]), rw_a2=bf(rw_a2[0]),
             rw_g1=bf(rw_g1[0]), rw_g2=bf(rw_g2[0]),
             rw_kk=rw_kk, rw_ka=rw_ka, rw_rk=rw_rk,
             rw_wr=bf(rw_wr[0]), rw_wk=bf(rw_wk[0]), rw_wv=bf(rw_wv[0]), rw_wo=bf(rw_wo[0]),
             rw_lnx_g=rw_lnx_g, rw_lnx_b=rw_lnx_b, kv_norm_g=kv_norm_g, w_kv=bf(w_kv),
             b_norm_g=b_norm_g, sb_wq=bf(sb_wq[0]), sb_wo=bf(sb_wo[0]), f_norm_g=f_norm_g,
             f_wup=bf(f_wup), f_conv_w=f_conv_w, f_conv_b=f_conv_b, f_wdown=bf(f_wdown),
             out_norm_g=out_norm_g)
    B, _, D = x_prompt.shape
    n_a = state_shift.shape[0]
    depth = state_conv.shape[0]
    F = state_conv.shape[-1]
    H = state_wkv.shape[2]
    shift0 = jnp.zeros((n_a, B, D), x_prompt.dtype)
    wkv0 = jnp.zeros((n_a, B, H, RW_HEAD, RW_HEAD), F32)
    conv0 = jnp.zeros((depth, B, CONV_W - 1, F), x_prompt.dtype)
    y_p, wkv_p, shift_p, conv_p, k_p, v_p = _trunk(x_prompt, shift0, wkv0, conv0, None, None, p)
    Bs, P = cache_k.shape[0], cache_k.shape[1]
    y_s, wkv_s, shift_s, conv_s, k_s, v_s = _trunk(
        x_sample, state_shift, state_wkv.astype(F32), state_conv,
        bf(cache_k).reshape(Bs, P, -1), bf(cache_v).reshape(Bs, P, -1), p)
    return (y_p, y_s, wkv_p.astype(state_wkv.dtype), shift_p, conv_p, k_p, v_p,
            wkv_s.astype(state_wkv.dtype), shift_s, conv_s, k_s, v_s)
```

```python
import functools

import jax
import jax.numpy as jnp
from jax import lax
from jax.experimental import pallas as pl
from jax.experimental.pallas import tpu as pltpu

F32 = jnp.float32
BF16 = jnp.bfloat16

RW_HEAD = 64
PAIR = 2 * RW_HEAD
SB_HEAD_DIM = 128
GN_EPS = 64e-5
NORM_EPS = 1e-6
CONV_W = 3
WKV_CHUNK = 64
NEUMANN_BLOCK = 16
WKV_PAIR_GROUP = 8
LOG2_E = 1.4426950408889634
VMEM_LIMIT = 56 * 1024 * 1024


def _tile(n, pref):
    if n <= pref:
        return n
    t = pref
    while n % t:
        t //= 2
    return t


def _params(sem):
    return pltpu.CompilerParams(dimension_semantics=sem, vmem_limit_bytes=VMEM_LIMIT)


def _dot(a, b):
    return jnp.dot(a.astype(BF16), b.astype(BF16), preferred_element_type=F32)


def _dot_nt(a, b):
    return lax.dot_general(a.astype(BF16), b.astype(BF16), (((1,), (1,)), ((), ())),
                           preferred_element_type=F32)


def _dot_tn(a, b):
    return lax.dot_general(a.astype(BF16), b.astype(BF16), (((0,), (0,)), ((), ())),
                           preferred_element_type=F32)


def _rms_scale(x):
    return x * lax.rsqrt(jnp.mean(x * x, axis=-1, keepdims=True) + NORM_EPS)


def _softplus(u):
    return jnp.maximum(u, 0.0) + jnp.log1p(jnp.exp(-jnp.abs(u)))


def _row_ids(shape):
    return lax.broadcasted_iota(jnp.int32, shape, 0)


def _col_ids(shape):
    return lax.broadcasted_iota(jnp.int32, shape, 1)


def _rw_proj_kernel(x_ref, shift0_ref, ng_ref, mu_ref, w0_ref, a0_ref,
                    wr_ref, wk_ref, wv_ref, w1_ref, w2_ref, a1_ref, a2_ref, g1_ref, g2_ref,
                    r_out, k_out, v_out, lw_out, as_out, g_out, shift_out,
                    xr_s, xk_s, xv_s, hw_s, ha_s, hg_s, carry_s):
    t = pl.program_id(1)
    j = pl.program_id(2)

    @pl.when(j == 0)
    def _():
        @pl.when(t == 0)
        def _():
            carry_s[...] = shift0_ref[...]

        xn = _rms_scale(x_ref[...]) * ng_ref[...]
        tm = xn.shape[0]
        rows = _row_ids(xn.shape)
        x_prev = jnp.where(rows == 0, carry_s[...], pltpu.roll(xn, 1, axis=0))
        last = xn[tm - 1:tm, :]
        carry_s[...] = last
        shift_out[...] = last
        xx = x_prev - xn
        xr_s[...] = (xn + xx * mu_ref[0:1, :]).astype(BF16)
        xk_s[...] = (xn + xx * mu_ref[2:3, :]).astype(BF16)
        xv_s[...] = (xn + xx * mu_ref[3:4, :]).astype(BF16)
        xw = xn + xx * mu_ref[1:2, :]
        xa = xn + xx * mu_ref[4:5, :]
        xg = xn + xx * mu_ref[5:6, :]
        hw_s[...] = jnp.tanh(_dot(xw, w1_ref[...])).astype(BF16)
        ha_s[...] = _dot(xa, a1_ref[...]).astype(BF16)
        hg_s[...] = jax.nn.sigmoid(_dot(xg, g1_ref[...])).astype(BF16)

    r_out[...] = _dot(xr_s[...], wr_ref[...])
    k_out[...] = _dot(xk_s[...], wk_ref[...])
    v_out[...] = _dot(xv_s[...], wv_ref[...])
    w_log = -_softplus(-(w0_ref[...] + _dot(hw_s[...], w2_ref[...]))) - 0.5
    lw_out[...] = -jnp.exp(w_log)
    as_out[...] = jax.nn.sigmoid(a0_ref[...] + _dot(ha_s[...], a2_ref[...]))
    g_out[...] = _dot(hg_s[...], g2_ref[...])


def _rw_proj(x, shift0, ng, mu, w0, a0, wr, wk, wv, w1, w2, a1, a2, g1, g2):
    B, T, D = x.shape
    tm = _tile(T, 512)
    tn = _tile(D, 256)
    lw_dim, la_dim, lg_dim = w1.shape[1], a1.shape[1], g1.shape[1]
    row = lambda b, t, j: (0, 0)
    colv = pl.BlockSpec((1, tn), lambda b, t, j: (0, j))
    act = pl.BlockSpec((None, tm, tn), lambda b, t, j: (b, t, j))
    out_sds = jax.ShapeDtypeStruct((B, T, D), F32)
    return pl.pallas_call(
        _rw_proj_kernel,
        grid=(B, T // tm, D // tn),
        in_specs=[
            pl.BlockSpec((None, tm, D), lambda b, t, j: (b, t, 0)),
            pl.BlockSpec((None, 1, D), lambda b, t, j: (b, 0, 0)),
            pl.BlockSpec((1, D), row),
            pl.BlockSpec((6, D), row),
            colv, colv,
            pl.BlockSpec((D, tn), lambda b, t, j: (0, j)),
            pl.BlockSpec((D, tn), lambda b, t, j: (0, j)),
            pl.BlockSpec((D, tn), lambda b, t, j: (0, j)),
            pl.BlockSpec((D, lw_dim), row),
            pl.BlockSpec((lw_dim, tn), lambda b, t, j: (0, j)),
            pl.BlockSpec((D, la_dim), row),
            pl.BlockSpec((la_dim, tn), lambda b, t, j: (0, j)),
            pl.BlockSpec((D, lg_dim), row),
            pl.BlockSpec((lg_dim, tn), lambda b, t, j: (0, j)),
        ],
        out_specs=[act] * 6 + [pl.BlockSpec((None, 1, D), lambda b, t, j: (b, 0, 0))],
        out_shape=[out_sds] * 6 + [jax.ShapeDtypeStruct((B, 1, D), F32)],
        scratch_shapes=[
            pltpu.VMEM((tm, D), BF16), pltpu.VMEM((tm, D), BF16), pltpu.VMEM((tm, D), BF16),
            pltpu.VMEM((tm, lw_dim), BF16), pltpu.VMEM((tm, la_dim), BF16),
            pltpu.VMEM((tm, lg_dim), BF16), pltpu.VMEM((1, D), F32),
        ],
        compiler_params=_params(("arbitrary", "arbitrary", "arbitrary")),
        name="rw_proj",
    )(x, shift0.reshape(B, 1, D), ng.reshape(1, D), mu, w0.reshape(1, D), a0.reshape(1, D),
      wr, wk, wv, w1, w2, a1, a2, g1, g2)


def _wkv_kernel(r_ref, lw_ref, k_ref, v_ref, as_ref, g_ref, s0_ref,
                kkw_ref, kaw_ref, rk_ref, lng_ref, lnb_ref,
                o_ref, s_ref, c_s, *, chunk, n_pairs):
    C = chunk
    C2 = 2 * C

    @pl.when(pl.program_id(1) == 0)
    def _():
        s_ref[...] = s0_ref[...]

    tri = (_col_ids((C, C)) <= _row_ids((C, C))).astype(F32)
    c_s[...] = jnp.dot(tri, lw_ref[...], precision=lax.Precision.HIGHEST,
                       preferred_element_type=F32)

    lane = _col_ids((C, PAIR))
    lane2 = _col_ids((C2, PAIR))
    head0 = lane < RW_HEAD
    head0_2 = lane2 < RW_HEAD
    ones_bd = ((_row_ids((PAIR, PAIR)) // RW_HEAD) == (_col_ids((PAIR, PAIR)) // RW_HEAD))
    ones_bd_bf = ones_bd.astype(BF16)

    rr = _row_ids((C2, C2))
    cc = _col_ids((C2, C2))
    same_head = (rr // C) == (cc // C)
    strict22 = (cc % C) < (rr % C)
    nb = NEUMANN_BLOCK
    same16 = (rr // nb) == (cc // nb)
    same32 = (rr // (2 * nb)) == (cc // (2 * nb))
    m_diag16 = same16 & strict22
    m_e1 = same32 & (~same16) & strict22 if C >= 2 * nb else None
    m_e2 = same_head & (~same32) & strict22 if C >= 4 * nb else None
    eye22 = (rr == cc).astype(F32)
    r12 = _row_ids((C, C2))
    c12 = _col_ids((C, C2))
    strict12 = (c12 % C) < r12
    incl12 = (c12 % C) <= r12
    left12 = c12 < C

    def group_sum(x):
        hi = x.astype(BF16)
        lo = (x - hi.astype(F32)).astype(BF16)
        both = jnp.dot(jnp.concatenate([hi, lo], axis=0), ones_bd_bf,
                       preferred_element_type=F32)
        return both[:C] + both[C:]

    cat0 = lambda xs: jnp.concatenate(xs, axis=0)
    inv_n = 1.0 / RW_HEAD

    def pair_group(ps):
        n = range(len(ps))
        sl = [slice(p * PAIR, (p + 1) * PAIR) for p in ps]
        r = [r_ref[:, s] for s in sl]
        lw = [lw_ref[:, s] for s in sl]
        k = [k_ref[:, s] for s in sl]
        v = [v_ref[:, s] for s in sl]
        asig = [as_ref[:, s] for s in sl]
        c = [c_s[:, s] for s in sl]
        c_end = [x[C - 1:C, :] for x in c]

        kkv = [k[i] * kkw_ref[:, sl[i]] for i in n]
        ss = [group_sum(x * x) for x in kkv]
        kk = [kkv[i] / jnp.maximum(jnp.sqrt(ss[i]), 1e-12) for i in n]
        b_in = [kk[i] * asig[i] for i in n]
        k_in = [k[i] * (1.0 + (asig[i] - 1.0) * kaw_ref[:, sl[i]]) for i in n]
        bonus_s = [group_sum(r[i] * k_in[i] * rk_ref[:, sl[i]]) for i in n]

        e_neg = [jnp.exp(-x) for x in c]
        a_t = [(-kk[i]) * jnp.exp(c[i] - lw[i]) for i in n]
        r_t = [r[i] * jnp.exp(c[i]) for i in n]
        b_h = [b_in[i] * e_neg[i] for i in n]
        k_h = [k_in[i] * e_neg[i] for i in n]
        e_end = [jnp.exp(c_end[i] - c[i]) for i in n]
        RE = [cat0([b_in[i] * e_end[i], k_in[i] * e_end[i]]) for i in n]

        L = [cat0([a_t[i], r_t[i]]) for i in n]
        G0 = [_dot_nt(jnp.where(head0_2, L[i], 0.0), cat0([b_h[i], k_h[i]])) for i in n]
        G1 = [_dot_nt(jnp.where(head0_2, 0.0, L[i]), cat0([k_h[i], b_h[i]])) for i in n]
        top = [cat0([G0[i][:C], G1[i][:C]]) for i in n]

        d1 = [jnp.where(m_diag16, x, 0.0) for x in top]
        d2 = [_dot(x, x) for x in d1]
        d4 = [_dot(x, x) for x in d2]
        d8 = [_dot(x, x) for x in d4]
        pa = [_dot(eye22 + d1[i], eye22 + d2[i]) for i in n]
        pb = [_dot(eye22 + d4[i], eye22 + d8[i]) for i in n]
        tinv = [_dot(pa[i], pb[i]) for i in n]
        for m_e in (m_e1, m_e2):
            if m_e is not None:
                x = [_dot(tinv[i], jnp.where(m_e, top[i], 0.0)) for i in n]
                tinv = [tinv[i] + _dot(x[i], tinv[i]) for i in n]
        tcat = [x[:C] + x[C:] for x in tinv]

        v0 = [jnp.where(head0, x, 0.0) for x in v]
        v1 = [jnp.where(head0, 0.0, x) for x in v]
        av = [_dot(jnp.where(strict12, jnp.where(left12, top[i][C:], top[i][:C]), 0.0),
                   cat0([v1[i], v0[i]])) for i in n]
        bot0 = [jnp.where(incl12, x[C:], 0.0) for x in G0]
        bot1 = [jnp.where(incl12, x[C:], 0.0) for x in G1]

        S = [s_ref[p] for p in ps]
        P = [_dot_nt(L[i], S[i]) for i in n]
        W = [P[i][:C] + av[i] for i in n]
        U = [_dot(tcat[i], cat0([jnp.where(head0, W[i], 0.0), jnp.where(head0, 0.0, W[i])]))
             for i in n]
        Y = [P[i][C:]
             + _dot(bot0[i], cat0([jnp.where(head0, U[i], 0.0), v0[i]]))
             + _dot(bot1[i], cat0([v1[i], jnp.where(head0, 0.0, U[i])])) for i in n]
        for i in n:
            upd = _dot_tn(cat0([U[i], v[i]]), RE[i])
            s_ref[ps[i]] = S[i] * jnp.exp(c_end[i]) + jnp.where(ones_bd, upd, 0.0)

        mean = [group_sum(x) * inv_n for x in Y]
        dlt = [Y[i] - mean[i] for i in n]
        var = [group_sum(x * x) * inv_n for x in dlt]
        for i in n:
            yn = dlt[i] * lax.rsqrt(var[i] + GN_EPS) * lng_ref[:, sl[i]] + lnb_ref[:, sl[i]]
            out = (yn + bonus_s[i] * v[i]) * g_ref[:, sl[i]]
            o_ref[:, sl[i]] = out.astype(o_ref.dtype)

    group = _tile(n_pairs, WKV_PAIR_GROUP)
    for g0 in range(0, n_pairs, group):
        pair_group(list(range(g0, g0 + group)))


def _wkv(r, lw, k, v, asig, g, s0_bd, kkw, kaw, rk, lng, lnb):
    B, T, D = r.shape
    C = _tile(T, WKV_CHUNK)
    n_pairs = D // PAIR
    act = pl.BlockSpec((None, C, D), lambda b, t: (b, t, 0))
    vec = pl.BlockSpec((1, D), lambda b, t: (0, 0))
    st = pl.BlockSpec((None, n_pairs, PAIR, PAIR), lambda b, t: (b, 0, 0, 0))
    return pl.pallas_call(
        functools.partial(_wkv_kernel, chunk=C, n_pairs=n_pairs),
        grid=(B, T // C),
        in_specs=[act] * 6 + [st] + [vec] * 5,
        out_specs=[act, st],
        out_shape=[jax.ShapeDtypeStruct((B, T, D), BF16),
                   jax.ShapeDtypeStruct((B, n_pairs, PAIR, PAIR), F32)],
        scratch_shapes=[pltpu.VMEM((C, D), F32)],
        compiler_params=_params(("arbitrary", "arbitrary")),
        name="wkv",
    )(r, lw, k, v, asig, g, s0_bd, kkw.reshape(1, D), kaw.reshape(1, D), rk.reshape(1, D),
      lng.reshape(1, D), lnb.reshape(1, D))


def _state_to_pairs(s):
    B, H, n, _ = s.shape
    s = s.reshape(B, H // 2, 2, n, n)
    z = jnp.zeros((B, H // 2, n, n), s.dtype)
    top = jnp.concatenate([s[:, :, 0], z], axis=-1)
    bot = jnp.concatenate([z, s[:, :, 1]], axis=-1)
    return jnp.concatenate([top, bot], axis=-2)


def _pairs_to_state(sp):
    B, P, _, _ = sp.shape
    n = RW_HEAD
    h0 = sp[:, :, :n, :n]
    h1 = sp[:, :, n:, n:]
    return jnp.stack([h0, h1], axis=2).reshape(B, 2 * P, n, n)


def _matmul_res_kernel(a_ref, w_ref, res_ref, o_ref):
    o_ref[...] = res_ref[...] + jnp.dot(a_ref[...], w_ref[...], preferred_element_type=F32)


def _matmul_res(a, w, res):
    M, K = a.shape
    N = w.shape[1]
    tm = _tile(M, 512)
    tn = _tile(N, 512)
    return pl.pallas_call(
        _matmul_res_kernel,
        grid=(M // tm, N // tn),
        in_specs=[pl.BlockSpec((tm, K), lambda i, j: (i, 0)),
                  pl.BlockSpec((K, tn), lambda i, j: (0, j)),
                  pl.BlockSpec((tm, tn), lambda i, j: (i, j))],
        out_specs=pl.BlockSpec((tm, tn), lambda i, j: (i, j)),
        out_shape=jax.ShapeDtypeStruct((M, N), F32),
        compiler_params=_params(("parallel", "parallel")),
        name="matmul_res",
    )(a, w, res)


def _ffn_kernel(x_ref, conv0_ref, ng_ref, wg_ref, wv_ref, cw_ref, cb_ref, wd_ref, og_ref,
                o_ref, conv_out, xn_s, acc_s, carry_s, *, final_norm):
    t = pl.program_id(1)
    f = pl.program_id(2)
    nf = pl.num_programs(2)

    @pl.when(f == 0)
    def _():
        xn_s[...] = (_rms_scale(x_ref[...]) * ng_ref[...]).astype(BF16)
        acc_s[...] = jnp.zeros_like(acc_s)

    @pl.when(t == 0)
    def _():
        carry_s[f] = conv0_ref[...]

    xn = xn_s[...]
    gate = jnp.dot(xn, wg_ref[...], preferred_element_type=F32)
    val = jnp.dot(xn, wv_ref[...], preferred_element_type=F32)
    tm = gate.shape[0]
    prev = carry_s[f]
    rows = _row_ids(gate.shape)
    g1 = jnp.where(rows == 0, prev[1:2, :], pltpu.roll(gate, 1, axis=0))
    g2 = jnp.where(rows == 0, prev[0:1, :],
                   jnp.where(rows == 1, prev[1:2, :], pltpu.roll(gate, 2, axis=0)))
    c = cb_ref[...] + cw_ref[0:1, :] * g2
    c = c + cw_ref[1:2, :] * g1
    c = c + cw_ref[2:3, :] * gate
    y = (c * jax.nn.sigmoid(c)) * val
    acc_s[...] += jnp.dot(y.astype(BF16), wd_ref[...], preferred_element_type=F32)
    tail = gate[tm - (CONV_W - 1):tm, :]
    carry_s[f] = tail
    tf = tail.shape[1]
    conv_out[:, pl.ds(pl.multiple_of(f * tf, tf), tf)] = tail

    @pl.when(f == nf - 1)
    def _():
        out = x_ref[...] + acc_s[...]
        if final_norm:
            out = _rms_scale(out) * og_ref[...]
        o_ref[...] = out


def _ffn(x, conv0, ng, wup, cw, cb, wdown, og, *, final_norm):
    B, T, D = x.shape
    F = wdown.shape[0]
    tm = _tile(T, 512)
    tf = _tile(F, 512)
    nf = F // tf
    row = lambda b, t, f: (0, 0)
    return pl.pallas_call(
        functools.partial(_ffn_kernel, final_norm=final_norm),
        grid=(B, T // tm, nf),
        in_specs=[
            pl.BlockSpec((None, tm, D), lambda b, t, f: (b, t, 0)),
            pl.BlockSpec((None, CONV_W - 1, tf), lambda b, t, f: (b, 0, f)),
            pl.BlockSpec((1, D), row),
            pl.BlockSpec((D, tf), lambda b, t, f: (0, f)),
            pl.BlockSpec((D, tf), lambda b, t, f: (0, f + nf)),
            pl.BlockSpec((CONV_W, tf), lambda b, t, f: (0, f)),
            pl.BlockSpec((1, tf), lambda b, t, f: (0, f)),
            pl.BlockSpec((tf, D), lambda b, t, f: (f, 0)),
            pl.BlockSpec((1, D), row),
        ],
        out_specs=[pl.BlockSpec((None, tm, D), lambda b, t, f: (b, t, 0)),
                   pl.BlockSpec((None, CONV_W - 1, F), lambda b, t, f: (b, 0, 0))],
        out_shape=[jax.ShapeDtypeStruct((B, T, D), F32),
                   jax.ShapeDtypeStruct((B, CONV_W - 1, F), F32)],
        scratch_shapes=[pltpu.VMEM((tm, D), BF16), pltpu.VMEM((tm, D), F32),
                        pltpu.VMEM((nf, CONV_W - 1, tf), F32)],
        compiler_params=_params(("arbitrary", "arbitrary", "arbitrary")),
        name="conv_ffn",
    )(x, conv0, ng.reshape(1, D), wup, wup, cw, cb.reshape(1, F), wdown, og.reshape(1, D))


def _kvq_kernel(x_ref, gkv_ref, gq_ref, wk_ref, wv_ref, wq_ref,
                k_out, v_out, kb_out, vb_out, q_out, xkv_s, xq_s, *, q_scale):
    @pl.when(pl.program_id(1) == 0)
    def _():
        xh = _rms_scale(x_ref[...])
        xkv_s[...] = (xh * gkv_ref[...]).astype(BF16)
        xq_s[...] = (xh * gq_ref[...]).astype(BF16)

    xkv = xkv_s[...]
    k = jnp.dot(xkv, wk_ref[...], preferred_element_type=F32)
    v = jnp.dot(xkv, wv_ref[...], preferred_element_type=F32)
    k_out[...] = k
    v_out[...] = v
    kb_out[...] = k.astype(BF16)
    vb_out[...] = v.astype(BF16)
    q = jnp.dot(xq_s[...], wq_ref[...], preferred_element_type=F32)
    q_out[...] = (q * q_scale).astype(BF16)


def _kvq(x, gkv, gq, wkv, wq):
    M, D = x.shape
    N = wq.shape[1]
    tm = _tile(M, 512)
    tn = _tile(N, 512)
    nn = N // tn
    row = lambda i, j: (0, 0)
    blk = pl.BlockSpec((tm, tn), lambda i, j: (i, j))
    return pl.pallas_call(
        functools.partial(_kvq_kernel, q_scale=float(SB_HEAD_DIM) ** -0.5 * LOG2_E),
        grid=(M // tm, nn),
        in_specs=[pl.BlockSpec((tm, D), lambda i, j: (i, 0)),
                  pl.BlockSpec((1, D), row), pl.BlockSpec((1, D), row),
                  pl.BlockSpec((D, tn), lambda i, j: (0, j)),
                  pl.BlockSpec((D, tn), lambda i, j: (0, j + nn)),
                  pl.BlockSpec((D, tn), lambda i, j: (0, j))],
        out_specs=[blk] * 5,
        out_shape=[jax.ShapeDtypeStruct((M, N), F32), jax.ShapeDtypeStruct((M, N), F32),
                   jax.ShapeDtypeStruct((M, N), BF16), jax.ShapeDtypeStruct((M, N), BF16),
                   jax.ShapeDtypeStruct((M, N), BF16)],
        scratch_shapes=[pltpu.VMEM((tm, D), BF16), pltpu.VMEM((tm, D), BF16)],
        compiler_params=_params(("parallel", "arbitrary")),
        name="kvq_proj",
    )(x, gkv.reshape(1, D), gq.reshape(1, D), wkv, wkv, wq)


SB_DEAD = 152.0


def _sb_block(q, kb, vb, tri, carry, valid, on=None):
    z = _dot_nt(q, kb)
    sp = jnp.maximum(z, 0.0) + jnp.log2(1.0 + jnp.exp2(jnp.minimum(z, -z)))
    if valid is not None:
        sp = jnp.where(valid, sp, 0.0)
    if on is not None:
        sp = jnp.where(on, sp, 0.0)
    later = jnp.dot(sp.astype(BF16), tri, preferred_element_type=F32)
    logw = z - sp - later
    if carry is not None:
        logw = logw - carry
    w = jnp.exp2(logw)
    if valid is not None:
        w = jnp.where(valid, w, 0.0)
    if on is not None:
        w = jnp.where(on, w, 0.0)
    out = jnp.dot(w.astype(BF16), vb.astype(BF16), preferred_element_type=F32)
    return out, later[:, 0:1] + sp[:, 0:1]


def _sb_prompt_kernel(q_ref, k_ref, v_ref, tri_ref, o_ref, acc_s, carry_s, *, nsub, tb):
    qi = pl.program_id(2)
    blk0 = qi * nsub
    tri = tri_ref[...]
    valid = _col_ids((tb, tb)) < _row_ids((tb, tb))
    rows = lambda i: slice(i * tb, (i + 1) * tb)

    def kv_block(j):
        k0 = pl.multiple_of(j * tb, tb)
        return k_ref[pl.ds(k0, tb), :], v_ref[pl.ds(k0, tb), :]

    kvs = [kv_block(blk0 + i) for i in range(nsub)]
    acc, car = [], []
    for i in range(nsub):
        out, tot = _sb_block(q_ref[rows(i), :], kvs[i][0], kvs[i][1], tri, None, valid)
        acc.append(out)
        car.append(tot)
    for i in range(nsub):
        if i == 0:
            kb, vb = kv_block(jnp.maximum(blk0 - 1, 0))
            out, tot = _sb_block(q_ref[rows(0), :], kb, vb, tri, car[0], None, on=blk0 > 0)
        else:
            out, tot = _sb_block(q_ref[rows(i), :], kvs[i - 1][0], kvs[i - 1][1], tri,
                                 car[i], None)
        acc_s[i] = acc[i] + out
        carry_s[i] = car[i] + tot

    def live_min(t):
        m = jnp.float32(jnp.inf)
        for i in range(nsub):
            m = jnp.where(blk0 + i - 2 - t >= 0, jnp.minimum(m, jnp.min(carry_s[i])), m)
        return m

    def cond(st):
        t, m = st
        return jnp.logical_and(blk0 + nsub - 3 - t >= 0, m < SB_DEAD)

    def body(st):
        t, _ = st
        for i in range(nsub):
            j = blk0 + i - 2 - t
            kb, vb = kv_block(jnp.maximum(j, 0))
            out, tot = _sb_block(q_ref[rows(i), :], kb, vb, tri, carry_s[i], None, on=j >= 0)
            acc_s[i] += out
            carry_s[i] += tot
        return t + 1, live_min(t + 1)

    lax.while_loop(cond, body, (jnp.int32(0), live_min(0)))
    for i in range(nsub):
        o_ref[rows(i), :] = acc_s[i].astype(o_ref.dtype)


def _sb_decode_kernel(q_ref, k_ref, v_ref, kp_ref, vp_ref, trid_ref, trip_ref, o_ref,
                      acc_s, carry_s, *, tq, tk, n_past):
    q = q_ref[...]
    valid = _col_ids((tq, tq)) < _row_ids((tq, tq))
    acc, carry = _sb_block(q, k_ref[...], v_ref[...], trid_ref[...], None, valid)
    acc_s[...] = acc
    carry_s[...] = carry
    trip = trip_ref[...]

    def cond(st):
        t, m = st
        return jnp.logical_and(t < n_past, m < SB_DEAD)

    def body(st):
        t, _ = st
        k0 = pl.multiple_of((n_past - 1 - t) * tk, tk)
        out, tot = _sb_block(q, kp_ref[pl.ds(k0, tk), :], vp_ref[pl.ds(k0, tk), :],
                             trip, carry_s[...], None)
        acc_s[...] += out
        carry_s[...] += tot
        return t + 1, jnp.min(carry_s[...])

    lax.while_loop(cond, body, (jnp.int32(0), jnp.min(carry)))
    o_ref[...] = acc_s[...].astype(o_ref.dtype)


def _suffix_ones(n):
    return (_row_ids((n, n)) > _col_ids((n, n))).astype(BF16)


SB_BLOCK = 256
SB_QSUB = 4


def _sb_attention(q, k, v, k_past=None, v_past=None):
    B, T, D = q.shape
    H = D // SB_HEAD_DIM
    hd = SB_HEAD_DIM
    full = lambda n: pl.BlockSpec((n, n), lambda b, h, i: (0, 0))
    seq = pl.BlockSpec((None, T, hd), lambda b, h, i: (b, 0, h))
    if k_past is None:
        tb = _tile(T, SB_BLOCK)
        nsub = _tile(T // tb, SB_QSUB)
        tq = nsub * tb
        qspec = pl.BlockSpec((None, tq, hd), lambda b, h, i: (b, i, h))
        return pl.pallas_call(
            functools.partial(_sb_prompt_kernel, nsub=nsub, tb=tb),
            grid=(B, H, T // tq),
            in_specs=[qspec, seq, seq, full(tb)],
            out_specs=qspec,
            out_shape=jax.ShapeDtypeStruct((B, T, D), BF16),
            scratch_shapes=[pltpu.VMEM((nsub, tb, hd), F32), pltpu.VMEM((nsub, tb, 1), F32)],
            compiler_params=_params(("parallel", "parallel", "arbitrary")),
            name="sb_attention",
        )(q, k, v, _suffix_ones(tb))
    P = k_past.shape[1]
    tk = _tile(P, SB_BLOCK)
    n_past = P // tk
    qspec = pl.BlockSpec((None, T, hd), lambda b, h, i: (b, 0, h))
    past = pl.BlockSpec((None, P, hd), lambda b, h, i: (b, 0, h))
    return pl.pallas_call(
        functools.partial(_sb_decode_kernel, tq=T, tk=tk, n_past=n_past),
        grid=(B, H, 1),
        in_specs=[qspec, seq, seq, past, past, full(T), full(tk)],
        out_specs=qspec,
        out_shape=jax.ShapeDtypeStruct((B, T, D), BF16),
        scratch_shapes=[pltpu.VMEM((T, hd), F32), pltpu.VMEM((T, 1), F32)],
        compiler_params=_params(("parallel", "parallel", "arbitrary")),
        name="sb_attention_decode",
    )(q, k, v, k_past, v_past, _suffix_ones(T), _suffix_ones(tk))


def _trunk(x, shift0, wkv0, conv0, k_past, v_past, p):
    B, T, D = x.shape
    M = B * T
    r, k, v, lw, asig, g, shift = _rw_proj(
        x, shift0[0], p['a_norm_g'][0], p['rw_mu'][0], p['rw_w0'][0], p['rw_a0'][0],
        p['rw_wr'], p['rw_wk'], p['rw_wv'], p['rw_w1'], p['rw_w2'], p['rw_a1'], p['rw_a2'],
        p['rw_g1'], p['rw_g2'])
    o, s_bd = _wkv(r, lw, k, v, asig, g, _state_to_pairs(wkv0[0]),
                   p['rw_kk'][0], p['rw_ka'][0], p['rw_rk'][0].reshape(-1),
                   p['rw_lnx_g'][0], p['rw_lnx_b'][0])
    x = _matmul_res(o.reshape(M, D), p['rw_wo'], x.reshape(M, D)).reshape(B, T, D)
    x, conv_a = _ffn(x, conv0[0], p['f_norm_g'][0], p['f_wup'][0], p['f_conv_w'][0],
                     p['f_conv_b'][0], p['f_wdown'][0], p['out_norm_g'], final_norm=False)
    k_sh, v_sh, kb, vb, q = _kvq(x.reshape(M, D), p['kv_norm_g'], p['b_norm_g'][0],
                                 p['w_kv'], p['sb_wq'])
    d_att = q.shape[1]
    att = _sb_attention(q.reshape(B, T, d_att), kb.reshape(B, T, d_att), vb.reshape(B, T, d_att),
                        k_past, v_past)
    x = _matmul_res(att.reshape(M, d_att), p['sb_wo'], x.reshape(M, D)).reshape(B, T, D)
    y, conv_b = _ffn(x, conv0[1], p['f_norm_g'][1], p['f_wup'][1], p['f_conv_w'][1],
                     p['f_conv_b'][1], p['f_wdown'][1], p['out_norm_g'], final_norm=True)
    H = d_att // SB_HEAD_DIM
    return (y, _pairs_to_state(s_bd)[None], shift.reshape(1, B, D), jnp.stack([conv_a, conv_b]),
            k_sh.reshape(B, T, H, SB_HEAD_DIM), v_sh.reshape(B, T, H, SB_HEAD_DIM))


def kernel(x_prompt, x_sample, cache_k, cache_v, state_wkv, state_shift, state_conv, a_norm_g, rw_mu, rw_w0, rw_w1, rw_w2, rw_a0, rw_a1, rw_a2, rw_g1, rw_g2, rw_kk, rw_ka, rw_rk, rw_wr, rw_wk, rw_wv, rw_wo, rw_lnx_g, rw_lnx_b, kv_norm_g, w_kv, b_norm_g, sb_wq, sb_wo, f_norm_g, f_wup, f_conv_w, f_conv_b, f_wdown, out_norm_g):
    bf = lambda w: w.astype(BF16)
    p = dict(a_norm_g=a_norm_g, rw_mu=rw_mu, rw_w0=rw_w0, rw_a0=rw_a0,
             rw_w1=bf(rw_w1[0]), rw_w2=bf(rw_w2[0]), rw_a1=bf(rw_a1[0]), rw_a2=bf(rw_a2[0]),
             rw_g1=bf(rw_g1[0]), rw_g2=bf(rw_g2[0]),
             rw_kk=rw_kk, rw_ka=rw_ka, rw_rk=rw_rk,
             rw_wr=bf(rw_wr[0]), rw_wk=bf(rw_wk[0]), rw_wv=bf(rw_wv[0]), rw_wo=bf(rw_wo[0]),
             rw_lnx_g=rw_lnx_g, rw_lnx_b=rw_lnx_b, kv_norm_g=kv_norm_g, w_kv=bf(w_kv),
             b_norm_g=b_norm_g, sb_wq=bf(sb_wq[0]), sb_wo=bf(sb_wo[0]), f_norm_g=f_norm_g,
             f_wup=bf(f_wup), f_conv_w=f_conv_w, f_conv_b=f_conv_b, f_wdown=bf(f_wdown),
             out_norm_g=out_norm_g)
    B, _, D = x_prompt.shape
    n_a = state_shift.shape[0]
    depth = state_conv.shape[0]
    F = state_conv.shape[-1]
    H = state_wkv.shape[2]
    shift0 = jnp.zeros((n_a, B, D), x_prompt.dtype)
    wkv0 = jnp.zeros((n_a, B, H, RW_HEAD, RW_HEAD), F32)
    conv0 = jnp.zeros((depth, B, CONV_W - 1, F), x_prompt.dtype)
    y_p, wkv_p, shift_p, conv_p, k_p, v_p = _trunk(x_prompt, shift0, wkv0, conv0, None, None, p)
    Bs, P = cache_k.shape[0], cache_k.shape[1]
    y_s, wkv_s, shift_s, conv_s, k_s, v_s = _trunk(
        x_sample, state_shift, state_wkv.astype(F32), state_conv,
        bf(cache_k).reshape(Bs, P, -1), bf(cache_v).reshape(Bs, P, -1), p)
    return (y_p, y_s, wkv_p.astype(state_wkv.dtype), shift_p, conv_p, k_p, v_p,
            wkv_s.astype(state_wkv.dtype), shift_s, conv_s, k_s, v_s)
```

```python
import functools

import jax
import jax.numpy as jnp
from jax import lax
from jax.experimental import pallas as pl
from jax.experimental.pallas import tpu as pltpu

F32 = jnp.float32
BF16 = jnp.bfloat16

RW_HEAD = 64
PAIR = 2 * RW_HEAD
SB_HEAD_DIM = 128
GN_EPS = 64e-5
NORM_EPS = 1e-6
CONV_W = 3
WKV_CHUNK = 64
NEUMANN_BLOCK = 16
WKV_PAIR_GROUP = 8
LOG2_E = 1.4426950408889634
ROW_TILE = 512
VMEM_LIMIT = 56 * 1024 * 1024


def _tile(n, pref):
    if n <= pref:
        return n
    t = pref
    while n % t:
        t //= 2
    return t


def _row_tiling(B, T):
    if T >= ROW_TILE:
        return B, 1, _tile(T, ROW_TILE)
    nseq = _tile(B, max(ROW_TILE // T, 1))
    return B // nseq, nseq, nseq * T


def _params(sem):
    return pltpu.CompilerParams(dimension_semantics=sem, vmem_limit_bytes=VMEM_LIMIT)


def _dot(a, b):
    return jnp.dot(a.astype(BF16), b.astype(BF16), preferred_element_type=F32)


def _dot_nt(a, b):
    return lax.dot_general(a.astype(BF16), b.astype(BF16), (((1,), (1,)), ((), ())),
                           preferred_element_type=F32)


def _dot_tn(a, b):
    return lax.dot_general(a.astype(BF16), b.astype(BF16), (((0,), (0,)), ((), ())),
                           preferred_element_type=F32)


def _rms_scale(x):
    return x * lax.rsqrt(jnp.mean(x * x, axis=-1, keepdims=True) + NORM_EPS)


def _softplus(u):
    return jnp.maximum(u, 0.0) + jnp.log1p(jnp.exp(-jnp.abs(u)))


def _row_ids(shape):
    return lax.broadcasted_iota(jnp.int32, shape, 0)


def _col_ids(shape):
    return lax.broadcasted_iota(jnp.int32, shape, 1)


def _rw_proj_kernel(x_ref, shift0_ref, ng_ref, mu_ref, w0_ref, a0_ref,
                    wr_ref, wk_ref, wv_ref, w1_ref, w2_ref, a1_ref, a2_ref, g1_ref, g2_ref,
                    r_out, k_out, v_out, lw_out, as_out, g_out, shift_out,
                    xr_s, xk_s, xv_s, hw_s, ha_s, hg_s, carry_s, *, nseq, seq_rows):
    t = pl.program_id(1)
    j = pl.program_id(2)

    @pl.when(j == 0)
    def _():
        @pl.when(t == 0)
        def _():
            carry_s[...] = shift0_ref[0]

        xn = _rms_scale(x_ref[...]) * ng_ref[...]
        rows = _row_ids(xn.shape)
        x_prev = jnp.where(rows == 0, carry_s[...], pltpu.roll(xn, 1, axis=0))
        for s in range(1, nseq):
            x_prev = jnp.where(rows == s * seq_rows, shift0_ref[s], x_prev)
        for s in range(nseq):
            shift_out[s] = xn[(s + 1) * seq_rows - 1:(s + 1) * seq_rows, :]
        carry_s[...] = xn[nseq * seq_rows - 1:nseq * seq_rows, :]
        xx = x_prev - xn
        mix = lambda i: (xn + xx * mu_ref[i:i + 1, :]).astype(BF16)
        xr_s[...] = mix(0)
        xk_s[...] = mix(2)
        xv_s[...] = mix(3)
        hw_s[...] = jnp.tanh(_dot(mix(1), w1_ref[...])).astype(BF16)
        ha_s[...] = _dot(mix(4), a1_ref[...]).astype(BF16)
        hg_s[...] = jax.nn.sigmoid(_dot(mix(5), g1_ref[...])).astype(BF16)

    r_out[...] = _dot(xr_s[...], wr_ref[...])
    k_out[...] = _dot(xk_s[...], wk_ref[...])
    v_out[...] = _dot(xv_s[...], wv_ref[...])
    w_log = -_softplus(-(w0_ref[...] + _dot(hw_s[...], w2_ref[...]))) - 0.5
    lw_out[...] = -jnp.exp(w_log)
    as_out[...] = jax.nn.sigmoid(a0_ref[...] + _dot(ha_s[...], a2_ref[...]))
    g_out[...] = _dot(hg_s[...], g2_ref[...])


def _rw_proj(x, shift0, ng, mu, w0, a0, wr, wk, wv, w1, w2, a1, a2, g1, g2):
    B, T, D = x.shape
    G, nseq, tm = _row_tiling(B, T)
    R = B * T // G
    tn = _tile(D, 512)
    lw_dim, la_dim, lg_dim = w1.shape[1], a1.shape[1], g1.shape[1]
    row = lambda b, t, j: (0, 0)
    colv = pl.BlockSpec((1, tn), lambda b, t, j: (0, j))
    act = pl.BlockSpec((None, tm, tn), lambda b, t, j: (b, t, j))
    out_sds = jax.ShapeDtypeStruct((G, R, D), F32)
    outs = pl.pallas_call(
        functools.partial(_rw_proj_kernel, nseq=nseq, seq_rows=tm // nseq),
        grid=(G, R // tm, D // tn),
        in_specs=[
            pl.BlockSpec((None, tm, D), lambda b, t, j: (b, t, 0)),
            pl.BlockSpec((nseq, 1, D), lambda b, t, j: (b, 0, 0)),
            pl.BlockSpec((1, D), row),
            pl.BlockSpec((6, D), row),
            colv, colv,
            pl.BlockSpec((D, tn), lambda b, t, j: (0, j)),
            pl.BlockSpec((D, tn), lambda b, t, j: (0, j)),
            pl.BlockSpec((D, tn), lambda b, t, j: (0, j)),
            pl.BlockSpec((D, lw_dim), row),
            pl.BlockSpec((lw_dim, tn), lambda b, t, j: (0, j)),
            pl.BlockSpec((D, la_dim), row),
            pl.BlockSpec((la_dim, tn), lambda b, t, j: (0, j)),
            pl.BlockSpec((D, lg_dim), row),
            pl.BlockSpec((lg_dim, tn), lambda b, t, j: (0, j)),
        ],
        out_specs=[act] * 6 + [pl.BlockSpec((nseq, 1, D), lambda b, t, j: (b, 0, 0))],
        out_shape=[out_sds] * 6 + [jax.ShapeDtypeStruct((B, 1, D), F32)],
        scratch_shapes=[
            pltpu.VMEM((tm, D), BF16), pltpu.VMEM((tm, D), BF16), pltpu.VMEM((tm, D), BF16),
            pltpu.VMEM((tm, lw_dim), BF16), pltpu.VMEM((tm, la_dim), BF16),
            pltpu.VMEM((tm, lg_dim), BF16), pltpu.VMEM((1, D), F32),
        ],
        compiler_params=_params(("arbitrary", "arbitrary", "arbitrary")),
        name="rw_proj",
    )(x.reshape(G, R, D), shift0.reshape(B, 1, D), ng.reshape(1, D), mu, w0.reshape(1, D),
      a0.reshape(1, D), wr, wk, wv, w1, w2, a1, a2, g1, g2)
    return [o.reshape(B, T, D) for o in outs[:6]] + [outs[6]]


def _wkv_kernel(r_ref, lw_ref, k_ref, v_ref, as_ref, g_ref, s0_ref,
                kkw_ref, kaw_ref, rk_ref, lng_ref, lnb_ref,
                o_ref, s_ref, c_s, *, chunk, n_pairs):
    C = chunk
    C2 = 2 * C

    @pl.when(pl.program_id(1) == 0)
    def _():
        s_ref[...] = s0_ref[...]

    tri = (_col_ids((C, C)) <= _row_ids((C, C))).astype(F32)
    c_s[...] = jnp.dot(tri, lw_ref[...], precision=lax.Precision.HIGHEST,
                       preferred_element_type=F32)

    lane = _col_ids((C, PAIR))
    lane2 = _col_ids((C2, PAIR))
    head0 = lane < RW_HEAD
    head0_2 = lane2 < RW_HEAD
    ones_bd = ((_row_ids((PAIR, PAIR)) // RW_HEAD) == (_col_ids((PAIR, PAIR)) // RW_HEAD))
    ones_bd_bf = ones_bd.astype(BF16)

    rr = _row_ids((C2, C2))
    cc = _col_ids((C2, C2))
    same_head = (rr // C) == (cc // C)
    strict22 = (cc % C) < (rr % C)
    nb = NEUMANN_BLOCK
    same16 = (rr // nb) == (cc // nb)
    same32 = (rr // (2 * nb)) == (cc // (2 * nb))
    m_diag16 = same16 & strict22
    m_e1 = same32 & (~same16) & strict22 if C >= 2 * nb else None
    m_e2 = same_head & (~same32) & strict22 if C >= 4 * nb else None
    eye22 = (rr == cc).astype(F32)
    r12 = _row_ids((C, C2))
    c12 = _col_ids((C, C2))
    strict12 = (c12 % C) < r12
    incl12 = (c12 % C) <= r12
    left12 = c12 < C

    def group_sum(x):
        hi = x.astype(BF16)
        lo = (x - hi.astype(F32)).astype(BF16)
        both = jnp.dot(jnp.concatenate([hi, lo], axis=0), ones_bd_bf,
                       preferred_element_type=F32)
        return both[:C] + both[C:]

    cat0 = lambda xs: jnp.concatenate(xs, axis=0)
    inv_n = 1.0 / RW_HEAD

    def pair_group(ps):
        n = range(len(ps))
        sl = [slice(p * PAIR, (p + 1) * PAIR) for p in ps]
        r = [r_ref[:, s] for s in sl]
        lw = [lw_ref[:, s] for s in sl]
        k = [k_ref[:, s] for s in sl]
        v = [v_ref[:, s] for s in sl]
        asig = [as_ref[:, s] for s in sl]
        c = [c_s[:, s] for s in sl]
        c_end = [x[C - 1:C, :] for x in c]

        kkv = [k[i] * kkw_ref[:, sl[i]] for i in n]
        ss = [group_sum(x * x) for x in kkv]
        kk = [kkv[i] / jnp.maximum(jnp.sqrt(ss[i]), 1e-12) for i in n]
        b_in = [kk[i] * asig[i] for i in n]
        k_in = [k[i] * (1.0 + (asig[i] - 1.0) * kaw_ref[:, sl[i]]) for i in n]
        bonus_s = [group_sum(r[i] * k_in[i] * rk_ref[:, sl[i]]) for i in n]

        e_neg = [jnp.exp(-x) for x in c]
        a_t = [(-kk[i]) * jnp.exp(c[i] - lw[i]) for i in n]
        r_t = [r[i] * jnp.exp(c[i]) for i in n]
        b_h = [b_in[i] * e_neg[i] for i in n]
        k_h = [k_in[i] * e_neg[i] for i in n]
        e_end = [jnp.exp(c_end[i] - c[i]) for i in n]
        RE = [cat0([b_in[i] * e_end[i], k_in[i] * e_end[i]]) for i in n]

        L = [cat0([a_t[i], r_t[i]]) for i in n]
        G0 = [_dot_nt(jnp.where(head0_2, L[i], 0.0), cat0([b_h[i], k_h[i]])) for i in n]
        G1 = [_dot_nt(jnp.where(head0_2, 0.0, L[i]), cat0([k_h[i], b_h[i]])) for i in n]
        top = [cat0([G0[i][:C], G1[i][:C]]) for i in n]

        d1 = [jnp.where(m_diag16, x, 0.0) for x in top]
        d2 = [_dot(x, x) for x in d1]
        d4 = [_dot(x, x) for x in d2]
        d8 = [_dot(x, x) for x in d4]
        pa = [_dot(eye22 + d1[i], eye22 + d2[i]) for i in n]
        pb = [_dot(eye22 + d4[i], eye22 + d8[i]) for i in n]
        tinv = [_dot(pa[i], pb[i]) for i in n]
        for m_e in (m_e1, m_e2):
            if m_e is not None:
                x = [_dot(tinv[i], jnp.where(m_e, top[i], 0.0)) for i in n]
                tinv = [tinv[i] + _dot(x[i], tinv[i]) for i in n]
        tcat = [x[:C] + x[C:] for x in tinv]

        v0 = [jnp.where(head0, x, 0.0) for x in v]
        v1 = [jnp.where(head0, 0.0, x) for x in v]
        av = [_dot(jnp.where(strict12, jnp.where(left12, top[i][C:], top[i][:C]), 0.0),
                   cat0([v1[i], v0[i]])) for i in n]
        bot0 = [jnp.where(incl12, x[C:], 0.0) for x in G0]
        bot1 = [jnp.where(incl12, x[C:], 0.0) for x in G1]

        S = [s_ref[p] for p in ps]
        P = [_dot_nt(L[i], S[i]) for i in n]
        W = [P[i][:C] + av[i] for i in n]
        U = [_dot(tcat[i], cat0([jnp.where(head0, W[i], 0.0), jnp.where(head0, 0.0, W[i])]))
             for i in n]
        Y = [P[i][C:]
             + _dot(bot0[i], cat0([jnp.where(head0, U[i], 0.0), v0[i]]))
             + _dot(bot1[i], cat0([v1[i], jnp.where(head0, 0.0, U[i])])) for i in n]
        for i in n:
            upd = _dot_tn(cat0([U[i], v[i]]), RE[i])
            s_ref[ps[i]] = S[i] * jnp.exp(c_end[i]) + jnp.where(ones_bd, upd, 0.0)

        mean = [group_sum(x) * inv_n for x in Y]
        dlt = [Y[i] - mean[i] for i in n]
        var = [group_sum(x * x) * inv_n for x in dlt]
        for i in n:
            yn = dlt[i] * lax.rsqrt(var[i] + GN_EPS) * lng_ref[:, sl[i]] + lnb_ref[:, sl[i]]
            out = (yn + bonus_s[i] * v[i]) * g_ref[:, sl[i]]
            o_ref[:, sl[i]] = out.astype(o_ref.dtype)

    group = _tile(n_pairs, WKV_PAIR_GROUP)
    for g0 in range(0, n_pairs, group):
        pair_group(list(range(g0, g0 + group)))


def _wkv(r, lw, k, v, asig, g, s0_bd, kkw, kaw, rk, lng, lnb):
    B, T, D = r.shape
    C = _tile(T, WKV_CHUNK)
    n_pairs = D // PAIR
    act = pl.BlockSpec((None, C, D), lambda b, t: (b, t, 0))
    vec = pl.BlockSpec((1, D), lambda b, t: (0, 0))
    st = pl.BlockSpec((None, n_pairs, PAIR, PAIR), lambda b, t: (b, 0, 0, 0))
    return pl.pallas_call(
        functools.partial(_wkv_kernel, chunk=C, n_pairs=n_pairs),
        grid=(B, T // C),
        in_specs=[act] * 6 + [st] + [vec] * 5,
        out_specs=[act, st],
        out_shape=[jax.ShapeDtypeStruct((B, T, D), BF16),
                   jax.ShapeDtypeStruct((B, n_pairs, PAIR, PAIR), F32)],
        scratch_shapes=[pltpu.VMEM((C, D), F32)],
        compiler_params=_params(("arbitrary", "arbitrary")),
        name="wkv",
    )(r, lw, k, v, asig, g, s0_bd, kkw.reshape(1, D), kaw.reshape(1, D), rk.reshape(1, D),
      lng.reshape(1, D), lnb.reshape(1, D))


def _state_to_pairs(s):
    B, H, n, _ = s.shape
    s = s.reshape(B, H // 2, 2, n, n)
    z = jnp.zeros((B, H // 2, n, n), s.dtype)
    top = jnp.concatenate([s[:, :, 0], z], axis=-1)
    bot = jnp.concatenate([z, s[:, :, 1]], axis=-1)
    return jnp.concatenate([top, bot], axis=-2)


def _pairs_to_state(sp):
    B, P, _, _ = sp.shape
    n = RW_HEAD
    h0 = sp[:, :, :n, :n]
    h1 = sp[:, :, n:, n:]
    return jnp.stack([h0, h1], axis=2).reshape(B, 2 * P, n, n)


def _ffn_kernel(x_ref, a_ref, wo_ref, conv0_ref, ng_ref, wg_ref, wv_ref, cw_ref, cb_ref,
                wd_ref, og_ref, o_ref, conv_out, xn_s, acc_s, carry_s,
                *, final_norm, nseq, seq_rows):
    t = pl.program_id(1)
    f = pl.program_id(2)
    nf = pl.num_programs(2)
    keep = CONV_W - 1

    @pl.when(f == 0)
    def _():
        x1 = x_ref[...] + jnp.dot(a_ref[...], wo_ref[...], preferred_element_type=F32)
        acc_s[...] = x1
        xn_s[...] = (_rms_scale(x1) * ng_ref[...]).astype(BF16)

    @pl.when(t == 0)
    def _():
        carry_s[f] = conv0_ref[0]

    xn = xn_s[...]
    gate = jnp.dot(xn, wg_ref[...], preferred_element_type=F32)
    val = jnp.dot(xn, wv_ref[...], preferred_element_type=F32)
    prev = carry_s[f]
    rows = _row_ids(gate.shape)
    g1 = jnp.where(rows == 0, prev[1:2, :], pltpu.roll(gate, 1, axis=0))
    g2 = jnp.where(rows == 0, prev[0:1, :],
                   jnp.where(rows == 1, prev[1:2, :], pltpu.roll(gate, 2, axis=0)))
    for s in range(1, nseq):
        first = s * seq_rows
        prev_s = conv0_ref[s]
        g1 = jnp.where(rows == first, prev_s[1:2, :], g1)
        g2 = jnp.where(rows == first, prev_s[0:1, :],
                       jnp.where(rows == first + 1, prev_s[1:2, :], g2))
    c = cb_ref[...] + cw_ref[0:1, :] * g2
    c = c + cw_ref[1:2, :] * g1
    c = c + cw_ref[2:3, :] * gate
    y = (c * jax.nn.sigmoid(c)) * val
    acc_s[...] += jnp.dot(y.astype(BF16), wd_ref[...], preferred_element_type=F32)
    tf = gate.shape[1]
    cols = pl.ds(pl.multiple_of(f * tf, tf), tf)
    for s in range(nseq):
        conv_out[s, :, cols] = gate[(s + 1) * seq_rows - keep:(s + 1) * seq_rows, :]
    carry_s[f] = gate[nseq * seq_rows - keep:nseq * seq_rows, :]

    @pl.when(f == nf - 1)
    def _():
        out = acc_s[...]
        if final_norm:
            out = _rms_scale(out) * og_ref[...]
        o_ref[...] = out


def _ffn(x, a, wo, conv0, ng, wup, cw, cb, wdown, og, *, final_norm):
    B, T, D = x.shape
    F = wdown.shape[0]
    Ka = a.shape[-1]
    G, nseq, tm = _row_tiling(B, T)
    R = B * T // G
    tf = _tile(F, 512)
    nf = F // tf
    keep = CONV_W - 1
    row = lambda b, t, f: (0, 0)
    out, conv = pl.pallas_call(
        functools.partial(_ffn_kernel, final_norm=final_norm, nseq=nseq, seq_rows=tm // nseq),
        grid=(G, R // tm, nf),
        in_specs=[
            pl.BlockSpec((None, tm, D), lambda b, t, f: (b, t, 0)),
            pl.BlockSpec((None, tm, Ka), lambda b, t, f: (b, t, 0)),
            pl.BlockSpec((Ka, D), row, pipeline_mode=pl.Buffered(1)),
            pl.BlockSpec((nseq, keep, tf), lambda b, t, f: (b, 0, f)),
            pl.BlockSpec((1, D), row),
            pl.BlockSpec((D, tf), lambda b, t, f: (0, f)),
            pl.BlockSpec((D, tf), lambda b, t, f: (0, f + nf)),
            pl.BlockSpec((CONV_W, tf), lambda b, t, f: (0, f)),
            pl.BlockSpec((1, tf), lambda b, t, f: (0, f)),
            pl.BlockSpec((tf, D), lambda b, t, f: (f, 0)),
            pl.BlockSpec((1, D), row),
        ],
        out_specs=[pl.BlockSpec((None, tm, D), lambda b, t, f: (b, t, 0)),
                   pl.BlockSpec((nseq, keep, F), lambda b, t, f: (b, 0, 0))],
        out_shape=[jax.ShapeDtypeStruct((G, R, D), F32),
                   jax.ShapeDtypeStruct((B, keep, F), F32)],
        scratch_shapes=[pltpu.VMEM((tm, D), BF16), pltpu.VMEM((tm, D), F32),
                        pltpu.VMEM((nf, keep, tf), F32)],
        compiler_params=_params(("arbitrary", "arbitrary", "arbitrary")),
        name="conv_ffn",
    )(x.reshape(G, R, D), a.reshape(G, R, Ka), wo, conv0, ng.reshape(1, D), wup, wup, cw,
      cb.reshape(1, F), wdown, og.reshape(1, D))
    return out.reshape(B, T, D), conv


def _kvq_kernel(x_ref, gkv_ref, gq_ref, wk_ref, wv_ref, wq_ref,
                k_out, v_out, kb_out, vb_out, q_out, xkv_s, xq_s, *, q_scale):
    @pl.when(pl.program_id(1) == 0)
    def _():
        xh = _rms_scale(x_ref[...])
        xkv_s[...] = (xh * gkv_ref[...]).astype(BF16)
        xq_s[...] = (xh * gq_ref[...]).astype(BF16)

    xkv = xkv_s[...]
    k = jnp.dot(xkv, wk_ref[...], preferred_element_type=F32)
    v = jnp.dot(xkv, wv_ref[...], preferred_element_type=F32)
    k_out[...] = k
    v_out[...] = v
    kb_out[...] = k.astype(BF16)
    vb_out[...] = v.astype(BF16)
    q = jnp.dot(xq_s[...], wq_ref[...], preferred_element_type=F32)
    q_out[...] = (q * q_scale).astype(BF16)


def _kvq(x, gkv, gq, wkv, wq):
    M, D = x.shape
    N = wq.shape[1]
    tm = _tile(M, 512)
    tn = _tile(N, 512)
    nn = N // tn
    row = lambda i, j: (0, 0)
    blk = pl.BlockSpec((tm, tn), lambda i, j: (i, j))
    return pl.pallas_call(
        functools.partial(_kvq_kernel, q_scale=float(SB_HEAD_DIM) ** -0.5 * LOG2_E),
        grid=(M // tm, nn),
        in_specs=[pl.BlockSpec((tm, D), lambda i, j: (i, 0)),
                  pl.BlockSpec((1, D), row), pl.BlockSpec((1, D), row),
                  pl.BlockSpec((D, tn), lambda i, j: (0, j)),
                  pl.BlockSpec((D, tn), lambda i, j: (0, j + nn)),
                  pl.BlockSpec((D, tn), lambda i, j: (0, j))],
        out_specs=[blk] * 5,
        out_shape=[jax.ShapeDtypeStruct((M, N), F32), jax.ShapeDtypeStruct((M, N), F32),
                   jax.ShapeDtypeStruct((M, N), BF16), jax.ShapeDtypeStruct((M, N), BF16),
                   jax.ShapeDtypeStruct((M, N), BF16)],
        scratch_shapes=[pltpu.VMEM((tm, D), BF16), pltpu.VMEM((tm, D), BF16)],
        compiler_params=_params(("parallel", "arbitrary")),
        name="kvq_proj",
    )(x, gkv.reshape(1, D), gq.reshape(1, D), wkv, wkv, wq)


SB_DEAD = 152.0


def _sb_block(q, kb, vb, tri, carry, valid, on=None):
    z = _dot_nt(q, kb)
    sp = jnp.maximum(z, 0.0) + jnp.log2(1.0 + jnp.exp2(jnp.minimum(z, -z)))
    if valid is not None:
        sp = jnp.where(valid, sp, 0.0)
    if on is not None:
        sp = jnp.where(on, sp, 0.0)
    later = jnp.dot(sp.astype(BF16), tri, preferred_element_type=F32)
    logw = z - sp - later
    if carry is not None:
        logw = logw - carry
    w = jnp.exp2(logw)
    if valid is not None:
        w = jnp.where(valid, w, 0.0)
    if on is not None:
        w = jnp.where(on, w, 0.0)
    out = jnp.dot(w.astype(BF16), vb.astype(BF16), preferred_element_type=F32)
    return out, later[:, 0:1] + sp[:, 0:1]


def _sb_prompt_kernel(q_ref, k_ref, v_ref, tri_ref, o_ref, acc_s, carry_s, *, nsub, tb):
    qi = pl.program_id(2)
    blk0 = qi * nsub
    tri = tri_ref[...]
    valid = _col_ids((tb, tb)) < _row_ids((tb, tb))
    rows = lambda i: slice(i * tb, (i + 1) * tb)

    def kv_block(j):
        k0 = pl.multiple_of(j * tb, tb)
        return k_ref[pl.ds(k0, tb), :], v_ref[pl.ds(k0, tb), :]

    kvs = [kv_block(blk0 + i) for i in range(nsub)]
    acc, car = [], []
    for i in range(nsub):
        out, tot = _sb_block(q_ref[rows(i), :], kvs[i][0], kvs[i][1], tri, None, valid)
        acc.append(out)
        car.append(tot)
    for i in range(nsub):
        if i == 0:
            kb, vb = kv_block(jnp.maximum(blk0 - 1, 0))
            out, tot = _sb_block(q_ref[rows(0), :], kb, vb, tri, car[0], None, on=blk0 > 0)
        else:
            out, tot = _sb_block(q_ref[rows(i), :], kvs[i - 1][0], kvs[i - 1][1], tri,
                                 car[i], None)
        acc_s[i] = acc[i] + out
        carry_s[i] = car[i] + tot

    def live_min(t):
        m = jnp.float32(jnp.inf)
        for i in range(nsub):
            m = jnp.where(blk0 + i - 2 - t >= 0, jnp.minimum(m, jnp.min(carry_s[i])), m)
        return m

    def cond(st):
        t, m = st
        return jnp.logical_and(blk0 + nsub - 3 - t >= 0, m < SB_DEAD)

    def body(st):
        t, _ = st
        for i in range(nsub):
            j = blk0 + i - 2 - t
            kb, vb = kv_block(jnp.maximum(j, 0))
            out, tot = _sb_block(q_ref[rows(i), :], kb, vb, tri, carry_s[i], None, on=j >= 0)
            acc_s[i] += out
            carry_s[i] += tot
        return t + 1, live_min(t + 1)

    lax.while_loop(cond, body, (jnp.int32(0), live_min(0)))
    for i in range(nsub):
        o_ref[rows(i), :] = acc_s[i].astype(o_ref.dtype)


def _sb_decode_kernel(q_ref, k_ref, v_ref, kn_ref, vn_ref, kc_hbm, vc_hbm, trid_ref, trip_ref,
                      o_ref, acc_s, carry_s, kbuf, vbuf, sems, *, tq, tk, n_old):
    b = pl.program_id(0)
    h = pl.program_id(1)
    q = q_ref[...]
    valid = _col_ids((tq, tq)) < _row_ids((tq, tq))
    trip = trip_ref[...]
    acc, carry = _sb_block(q, k_ref[...], v_ref[...], trid_ref[...], None, valid)
    out, tot = _sb_block(q, kn_ref[...], vn_ref[...], trip, carry, None)
    acc_s[...] = acc + out
    carry_s[...] = carry + tot

    def cond(st):
        t, m = st
        return jnp.logical_and(t < n_old, m < SB_DEAD)

    def body(st):
        t, _ = st
        k0 = pl.multiple_of((n_old - 1 - t) * tk, tk)
        copies = [pltpu.make_async_copy(src.at[b, pl.ds(k0, tk), h, :], dst, sems.at[i])
                  for i, (src, dst) in enumerate(((kc_hbm, kbuf), (vc_hbm, vbuf)))]
        for c in copies:
            c.start()
        for c in copies:
            c.wait()
        out, tot = _sb_block(q, kbuf[...], vbuf[...], trip, carry_s[...], None)
        acc_s[...] += out
        carry_s[...] += tot
        return t + 1, jnp.min(carry_s[...])

    lax.while_loop(cond, body, (jnp.int32(0), jnp.min(carry_s[...])))
    o_ref[...] = acc_s[...].astype(o_ref.dtype)


def _suffix_ones(n):
    return (_row_ids((n, n)) > _col_ids((n, n))).astype(BF16)


SB_BLOCK = 256
SB_QSUB = 4


def _sb_attention(q, k, v, k_past=None, v_past=None):
    B, T, D = q.shape
    H = D // SB_HEAD_DIM
    hd = SB_HEAD_DIM
    full = lambda n: pl.BlockSpec((n, n), lambda b, h, i: (0, 0))
    seq = pl.BlockSpec((None, T, hd), lambda b, h, i: (b, 0, h))
    if k_past is None:
        tb = _tile(T, SB_BLOCK)
        nsub = _tile(T // tb, SB_QSUB)
        tq = nsub * tb
        qspec = pl.BlockSpec((None, tq, hd), lambda b, h, i: (b, i, h))
        return pl.pallas_call(
            functools.partial(_sb_prompt_kernel, nsub=nsub, tb=tb),
            grid=(B, H, T // tq),
            in_specs=[qspec, seq, seq, full(tb)],
            out_specs=qspec,
            out_shape=jax.ShapeDtypeStruct((B, T, D), BF16),
            scratch_shapes=[pltpu.VMEM((nsub, tb, hd), F32), pltpu.VMEM((nsub, tb, 1), F32)],
            compiler_params=_params(("parallel", "parallel", "arbitrary")),
            name="sb_attention",
        )(q, k, v, _suffix_ones(tb))
    P = k_past.shape[1]
    tk = _tile(P, SB_BLOCK)
    near = lambda c: c[:, P - tk:].reshape(B, tk, D).astype(BF16)
    qspec = pl.BlockSpec((None, T, hd), lambda b, h, i: (b, 0, h))
    nspec = pl.BlockSpec((None, tk, hd), lambda b, h, i: (b, 0, h))
    hbm = pl.BlockSpec(memory_space=pl.ANY)
    return pl.pallas_call(
        functools.partial(_sb_decode_kernel, tq=T, tk=tk, n_old=P // tk - 1),
        grid=(B, H, 1),
        in_specs=[qspec, seq, seq, nspec, nspec, hbm, hbm, full(T), full(tk)],
        out_specs=qspec,
        out_shape=jax.ShapeDtypeStruct((B, T, D), BF16),
        scratch_shapes=[pltpu.VMEM((T, hd), F32), pltpu.VMEM((T, 1), F32),
                        pltpu.VMEM((tk, hd), F32), pltpu.VMEM((tk, hd), F32),
                        pltpu.SemaphoreType.DMA((2,))],
        compiler_params=_params(("parallel", "parallel", "arbitrary")),
        name="sb_attention_decode",
    )(q, k, v, near(k_past), near(v_past), k_past, v_past, _suffix_ones(T), _suffix_ones(tk))


def _trunk(x, shift0, wkv0, conv0, k_past, v_past, p):
    B, T, D = x.shape
    M = B * T
    r, k, v, lw, asig, g, shift = _rw_proj(
        x, shift0[0], p['a_norm_g'][0], p['rw_mu'][0], p['rw_w0'][0], p['rw_a0'][0],
        p['rw_wr'], p['rw_wk'], p['rw_wv'], p['rw_w1'], p['rw_w2'], p['rw_a1'], p['rw_a2'],
        p['rw_g1'], p['rw_g2'])
    o, s_bd = _wkv(r, lw, k, v, asig, g, _state_to_pairs(wkv0[0]),
                   p['rw_kk'][0], p['rw_ka'][0], p['rw_rk'][0].reshape(-1),
                   p['rw_lnx_g'][0], p['rw_lnx_b'][0])
    x, conv_a = _ffn(x, o, p['rw_wo'], conv0[0], p['f_norm_g'][0], p['f_wup'][0],
                     p['f_conv_w'][0], p['f_conv_b'][0], p['f_wdown'][0], p['out_norm_g'],
                     final_norm=False)
    k_sh, v_sh, kb, vb, q = _kvq(x.reshape(M, D), p['kv_norm_g'], p['b_norm_g'][0],
                                 p['w_kv'], p['sb_wq'])
    d_att = q.shape[1]
    att = _sb_attention(q.reshape(B, T, d_att), kb.reshape(B, T, d_att), vb.reshape(B, T, d_att),
                        k_past, v_past)
    y, conv_b = _ffn(x, att, p['sb_wo'], conv0[1], p['f_norm_g'][1], p['f_wup'][1],
                     p['f_conv_w'][1], p['f_conv_b'][1], p['f_wdown'][1], p['out_norm_g'],
                     final_norm=True)
    H = d_att // SB_HEAD_DIM
    return (y, _pairs_to_state(s_bd)[None], shift.reshape(1, B, D), jnp.stack([conv_a, conv_b]),
            k_sh.reshape(B, T, H, SB_HEAD_DIM), v_sh.reshape(B, T, H, SB_HEAD_DIM))


def kernel(x_prompt, x_sample, cache_k, cache_v, state_wkv, state_shift, state_conv, a_norm_g, rw_mu, rw_w0, rw_w1, rw_w2, rw_a0, rw_a1, rw_a2, rw_g1, rw_g2, rw_kk, rw_ka, rw_rk, rw_wr, rw_wk, rw_wv, rw_wo, rw_lnx_g, rw_lnx_b, kv_norm_g, w_kv, b_norm_g, sb_wq, sb_wo, f_norm_g, f_wup, f_conv_w, f_conv_b, f_wdown, out_norm_g):
    bf = lambda w: w.astype(BF16)
    p = dict(a_norm_g=a_norm_g, rw_mu=rw_mu, rw_w0=rw_w0, rw_a0=rw_a0,
             rw_w1=bf(rw_w1[0]), rw_w2=bf(rw_w2[0]), rw_a1=bf(rw_a1[0]), rw_a2=bf(rw_a2[0]),
             rw_g1=bf(rw_g1[0]), rw_g2=bf(rw_g2[0]),
             rw_kk=rw_kk, rw_ka=rw_ka, rw_rk=rw_rk,
             rw_wr=bf(rw_wr[0]), rw_wk=bf(rw_wk[0]), rw_wv=bf(rw_wv[0]), rw_wo=bf(rw_wo[0]),
             rw_lnx_g=rw_lnx_g, rw_lnx_b=rw_lnx_b, kv_norm_g=kv_norm_g, w_kv=bf(w_kv),
             b_norm_g=b_norm_g, sb_wq=bf(sb_wq[0]), sb_wo=bf(sb_wo[0]), f_norm_g=f_norm_g,
             f_wup=bf(f_wup), f_conv_w=f_conv_w, f_conv_b=f_conv_b, f_wdown=bf(f_wdown),
             out_norm_g=out_norm_g)
    B, _, D = x_prompt.shape
    n_a = state_shift.shape[0]
    depth = state_conv.shape[0]
    F = state_conv.shape[-1]
    H = state_wkv.shape[2]
    shift0 = jnp.zeros((n_a, B, D), x_prompt.dtype)
    wkv0 = jnp.zeros((n_a, B, H, RW_HEAD, RW_HEAD), F32)
    conv0 = jnp.zeros((depth, B, CONV_W - 1, F), x_prompt.dtype)
    y_p, wkv_p, shift_p, conv_p, k_p, v_p = _trunk(x_prompt, shift0, wkv0, conv0, None, None, p)
    y_s, wkv_s, shift_s, conv_s, k_s, v_s = _trunk(
        x_sample, state_shift, state_wkv.astype(F32), state_conv, cache_k, cache_v, p)
    return (y_p, y_s, wkv_p.astype(state_wkv.dtype), shift_p, conv_p, k_p, v_p,
            wkv_s.astype(state_wkv.dtype), shift_s, conv_s, k_s, v_s)
```

```python
import functools

import jax
import jax.numpy as jnp
from jax import lax
from jax.experimental import pallas as pl
from jax.experimental.pallas import tpu as pltpu

F32 = jnp.float32
BF16 = jnp.bfloat16

RW_HEAD = 64
PAIR = 2 * RW_HEAD
SB_HEAD_DIM = 128
GN_EPS = 64e-5
NORM_EPS = 1e-6
CONV_W = 3
WKV_CHUNK = 64
NEUMANN_BLOCK = 16
WKV_PAIR_GROUP = 16
LOG2_E = 1.4426950408889634
ROW_TILE = 512
VMEM_LIMIT = 56 * 1024 * 1024


def _tile(n, pref):
    if n <= pref:
        return n
    t = pref
    while n % t:
        t //= 2
    return t


def _row_tiling(B, T):
    if T >= ROW_TILE:
        return B, 1, _tile(T, ROW_TILE)
    nseq = _tile(B, max(ROW_TILE // T, 1))
    return B // nseq, nseq, nseq * T


def _params(sem):
    return pltpu.CompilerParams(dimension_semantics=sem, vmem_limit_bytes=VMEM_LIMIT)


def _dot(a, b):
    return jnp.dot(a.astype(BF16), b.astype(BF16), preferred_element_type=F32)


def _dot_nt(a, b):
    return lax.dot_general(a.astype(BF16), b.astype(BF16), (((1,), (1,)), ((), ())),
                           preferred_element_type=F32)


def _dot_tn(a, b):
    return lax.dot_general(a.astype(BF16), b.astype(BF16), (((0,), (0,)), ((), ())),
                           preferred_element_type=F32)


def _rms_scale(x):
    return x * lax.rsqrt(jnp.mean(x * x, axis=-1, keepdims=True) + NORM_EPS)


def _softplus(u):
    return jnp.maximum(u, 0.0) + jnp.log1p(jnp.exp(-jnp.abs(u)))


def _row_ids(shape):
    return lax.broadcasted_iota(jnp.int32, shape, 0)


def _col_ids(shape):
    return lax.broadcasted_iota(jnp.int32, shape, 1)


def _rw_proj_kernel(x_ref, shift0_ref, ng_ref, mu_ref, w0_ref, a0_ref,
                    wr_ref, wk_ref, wv_ref, w1_ref, w2_ref, a1_ref, a2_ref, g1_ref, g2_ref,
                    r_out, k_out, v_out, lw_out, as_out, g_out, shift_out,
                    xr_s, xk_s, xv_s, hw_s, ha_s, hg_s, carry_s, *, nseq, seq_rows):
    t = pl.program_id(1)
    j = pl.program_id(2)

    @pl.when(j == 0)
    def _():
        @pl.when(t == 0)
        def _():
            carry_s[...] = shift0_ref[0]

        xn = _rms_scale(x_ref[...]) * ng_ref[...]
        rows = _row_ids(xn.shape)
        x_prev = jnp.where(rows == 0, carry_s[...], pltpu.roll(xn, 1, axis=0))
        for s in range(1, nseq):
            x_prev = jnp.where(rows == s * seq_rows, shift0_ref[s], x_prev)
        for s in range(nseq):
            shift_out[s] = xn[(s + 1) * seq_rows - 1:(s + 1) * seq_rows, :]
        carry_s[...] = xn[nseq * seq_rows - 1:nseq * seq_rows, :]
        xx = x_prev - xn
        mix = lambda i: (xn + xx * mu_ref[i:i + 1, :]).astype(BF16)
        xr_s[...] = mix(0)
        xk_s[...] = mix(2)
        xv_s[...] = mix(3)
        hw_s[...] = jnp.tanh(_dot(mix(1), w1_ref[...])).astype(BF16)
        ha_s[...] = _dot(mix(4), a1_ref[...]).astype(BF16)
        hg_s[...] = jax.nn.sigmoid(_dot(mix(5), g1_ref[...])).astype(BF16)

    r_out[...] = _dot(xr_s[...], wr_ref[...])
    k_out[...] = _dot(xk_s[...], wk_ref[...])
    v_out[...] = _dot(xv_s[...], wv_ref[...])
    w_log = -_softplus(-(w0_ref[...] + _dot(hw_s[...], w2_ref[...]))) - 0.5
    lw_out[...] = -jnp.exp(w_log)
    as_out[...] = jax.nn.sigmoid(a0_ref[...] + _dot(ha_s[...], a2_ref[...]))
    g_out[...] = _dot(hg_s[...], g2_ref[...])


def _rw_proj(x, shift0, ng, mu, w0, a0, wr, wk, wv, w1, w2, a1, a2, g1, g2):
    B, T, D = x.shape
    G, nseq, tm = _row_tiling(B, T)
    R = B * T // G
    tn = _tile(D, 512)
    lw_dim, la_dim, lg_dim = w1.shape[1], a1.shape[1], g1.shape[1]
    row = lambda b, t, j: (0, 0)
    colv = pl.BlockSpec((1, tn), lambda b, t, j: (0, j))
    act = pl.BlockSpec((None, tm, tn), lambda b, t, j: (b, t, j))
    out_sds = jax.ShapeDtypeStruct((G, R, D), F32)
    outs = pl.pallas_call(
        functools.partial(_rw_proj_kernel, nseq=nseq, seq_rows=tm // nseq),
        grid=(G, R // tm, D // tn),
        in_specs=[
            pl.BlockSpec((None, tm, D), lambda b, t, j: (b, t, 0)),
            pl.BlockSpec((nseq, 1, D), lambda b, t, j: (b, 0, 0)),
            pl.BlockSpec((1, D), row),
            pl.BlockSpec((6, D), row),
            colv, colv,
            pl.BlockSpec((D, tn), lambda b, t, j: (0, j)),
            pl.BlockSpec((D, tn), lambda b, t, j: (0, j)),
            pl.BlockSpec((D, tn), lambda b, t, j: (0, j)),
            pl.BlockSpec((D, lw_dim), row),
            pl.BlockSpec((lw_dim, tn), lambda b, t, j: (0, j)),
            pl.BlockSpec((D, la_dim), row),
            pl.BlockSpec((la_dim, tn), lambda b, t, j: (0, j)),
            pl.BlockSpec((D, lg_dim), row),
            pl.BlockSpec((lg_dim, tn), lambda b, t, j: (0, j)),
        ],
        out_specs=[act] * 6 + [pl.BlockSpec((nseq, 1, D), lambda b, t, j: (b, 0, 0))],
        out_shape=[out_sds] * 6 + [jax.ShapeDtypeStruct((B, 1, D), F32)],
        scratch_shapes=[
            pltpu.VMEM((tm, D), BF16), pltpu.VMEM((tm, D), BF16), pltpu.VMEM((tm, D), BF16),
            pltpu.VMEM((tm, lw_dim), BF16), pltpu.VMEM((tm, la_dim), BF16),
            pltpu.VMEM((tm, lg_dim), BF16), pltpu.VMEM((1, D), F32),
        ],
        compiler_params=_params(("arbitrary", "arbitrary", "arbitrary")),
        name="rw_proj",
    )(x.reshape(G, R, D), shift0.reshape(B, 1, D), ng.reshape(1, D), mu, w0.reshape(1, D),
      a0.reshape(1, D), wr, wk, wv, w1, w2, a1, a2, g1, g2)
    return [o.reshape(B, T, D) for o in outs[:6]] + [outs[6]]


def _wkv_kernel(r_ref, lw_ref, k_ref, v_ref, as_ref, g_ref, s0_ref,
                kkw_ref, kaw_ref, rk_ref, lng_ref, lnb_ref,
                o_ref, s_ref, c_s, *, chunk, n_pairs):
    C = chunk
    C2 = 2 * C

    @pl.when(pl.program_id(1) == 0)
    def _():
        s_ref[...] = s0_ref[...]

    tri = (_col_ids((C, C)) <= _row_ids((C, C))).astype(BF16)
    lw_all = lw_ref[...]
    lw_hi = lw_all.astype(BF16)
    lw_r = lw_all - lw_hi.astype(F32)
    lw_mid = lw_r.astype(BF16)
    lw_lo = (lw_r - lw_mid.astype(F32)).astype(BF16)
    c_s[...] = (jnp.dot(tri, lw_hi, preferred_element_type=F32)
                + jnp.dot(tri, lw_mid, preferred_element_type=F32)
                + jnp.dot(tri, lw_lo, preferred_element_type=F32))

    lane = _col_ids((C, PAIR))
    lane2 = _col_ids((C2, PAIR))
    head0 = lane < RW_HEAD
    head0_2 = lane2 < RW_HEAD
    ones_bd = ((_row_ids((PAIR, PAIR)) // RW_HEAD) == (_col_ids((PAIR, PAIR)) // RW_HEAD))
    ones_bd_bf = ones_bd.astype(BF16)

    rr = _row_ids((C2, C2))
    cc = _col_ids((C2, C2))
    same_head = (rr // C) == (cc // C)
    strict22 = (cc % C) < (rr % C)
    nb = NEUMANN_BLOCK
    same16 = (rr // nb) == (cc // nb)
    same32 = (rr // (2 * nb)) == (cc // (2 * nb))
    m_diag16 = same16 & strict22
    m_e1 = same32 & (~same16) & strict22 if C >= 2 * nb else None
    m_e2 = same_head & (~same32) & strict22 if C >= 4 * nb else None
    eye22 = (rr == cc).astype(F32)
    r12 = _row_ids((C, C2))
    c12 = _col_ids((C, C2))
    strict12 = (c12 % C) < r12
    incl12 = (c12 % C) <= r12
    left12 = c12 < C

    def group_sum(x):
        return jnp.dot(x.astype(BF16), ones_bd_bf, preferred_element_type=F32)

    cat0 = lambda xs: jnp.concatenate(xs, axis=0)
    inv_n = 1.0 / RW_HEAD

    def pair_group(ps):
        n = range(len(ps))
        sl = [slice(p * PAIR, (p + 1) * PAIR) for p in ps]
        r = [r_ref[:, s] for s in sl]
        lw = [lw_ref[:, s] for s in sl]
        k = [k_ref[:, s] for s in sl]
        v = [v_ref[:, s] for s in sl]
        asig = [as_ref[:, s] for s in sl]
        c = [c_s[:, s] for s in sl]
        c_end = [x[C - 1:C, :] for x in c]

        kkv = [k[i] * kkw_ref[:, sl[i]] for i in n]
        ss = [group_sum(x * x) for x in kkv]
        kk = [kkv[i] / jnp.maximum(jnp.sqrt(ss[i]), 1e-12) for i in n]
        b_in = [kk[i] * asig[i] for i in n]
        k_in = [k[i] * (1.0 + (asig[i] - 1.0) * kaw_ref[:, sl[i]]) for i in n]
        bonus_s = [group_sum(r[i] * k_in[i] * rk_ref[:, sl[i]]) for i in n]

        e_neg = [jnp.exp(-x) for x in c]
        a_t = [(-kk[i]) * jnp.exp(c[i] - lw[i]) for i in n]
        r_t = [r[i] * jnp.exp(c[i]) for i in n]
        b_h = [b_in[i] * e_neg[i] for i in n]
        k_h = [k_in[i] * e_neg[i] for i in n]
        e_end = [jnp.exp(c_end[i] - c[i]) for i in n]
        RE = [cat0([b_in[i] * e_end[i], k_in[i] * e_end[i]]) for i in n]

        L = [cat0([a_t[i], r_t[i]]) for i in n]
        zero_h1 = lambda x: jnp.where(head0, x, 0.0)
        zero_h0 = lambda x: jnp.where(head0, 0.0, x)
        G = [_dot_nt(L[i], cat0([zero_h1(b_h[i]), zero_h1(k_h[i]),
                                 zero_h0(k_h[i]), zero_h0(b_h[i])])) for i in n]
        G0 = [x[:, :C2] for x in G]
        G1 = [x[:, C2:] for x in G]
        top = [cat0([G0[i][:C], G1[i][:C]]) for i in n]

        d1 = [jnp.where(m_diag16, x, 0.0) for x in top]
        d2 = [_dot(x, x) for x in d1]
        d4 = [_dot(x, x) for x in d2]
        d8 = [_dot(x, x) for x in d4]
        pa = [_dot(eye22 + d1[i], eye22 + d2[i]) for i in n]
        pb = [_dot(eye22 + d4[i], eye22 + d8[i]) for i in n]
        tinv = [_dot(pa[i], pb[i]) for i in n]
        for m_e in (m_e1, m_e2):
            if m_e is not None:
                x = [_dot(tinv[i], jnp.where(m_e, top[i], 0.0)) for i in n]
                tinv = [tinv[i] + _dot(x[i], tinv[i]) for i in n]
        tcat = [x[:C] + x[C:] for x in tinv]

        v0 = [jnp.where(head0, x, 0.0) for x in v]
        v1 = [jnp.where(head0, 0.0, x) for x in v]
        av = [_dot(jnp.where(strict12, jnp.where(left12, top[i][C:], top[i][:C]), 0.0),
                   cat0([v1[i], v0[i]])) for i in n]
        bot0 = [jnp.where(incl12, x[C:], 0.0) for x in G0]
        bot1 = [jnp.where(incl12, x[C:], 0.0) for x in G1]

        S = [s_ref[p] for p in ps]
        P = [_dot_nt(L[i], S[i]) for i in n]
        W = [P[i][:C] + av[i] for i in n]
        U = [_dot(tcat[i], cat0([jnp.where(head0, W[i], 0.0), jnp.where(head0, 0.0, W[i])]))
             for i in n]
        Y = [P[i][C:]
             + _dot(jnp.concatenate([bot0[i], bot1[i]], axis=1),
                    cat0([zero_h1(U[i]), v0[i], v1[i], zero_h0(U[i])])) for i in n]
        for i in n:
            upd = _dot_tn(cat0([U[i], v[i]]), RE[i])
            s_ref[ps[i]] = S[i] * jnp.exp(c_end[i]) + jnp.where(ones_bd, upd, 0.0)

        mean = [group_sum(x) * inv_n for x in Y]
        dlt = [Y[i] - mean[i] for i in n]
        var = [group_sum(x * x) * inv_n for x in dlt]
        for i in n:
            yn = dlt[i] * lax.rsqrt(var[i] + GN_EPS) * lng_ref[:, sl[i]] + lnb_ref[:, sl[i]]
            out = (yn + bonus_s[i] * v[i]) * g_ref[:, sl[i]]
            o_ref[:, sl[i]] = out.astype(o_ref.dtype)

    group = _tile(n_pairs, WKV_PAIR_GROUP)
    for g0 in range(0, n_pairs, group):
        pair_group(list(range(g0, g0 + group)))


def _wkv(r, lw, k, v, asig, g, s0_bd, kkw, kaw, rk, lng, lnb):
    B, T, D = r.shape
    C = _tile(T, WKV_CHUNK)
    n_pairs = D // PAIR
    act = pl.BlockSpec((None, C, D), lambda b, t: (b, t, 0))
    vec = pl.BlockSpec((1, D), lambda b, t: (0, 0))
    st = pl.BlockSpec((None, n_pairs, PAIR, PAIR), lambda b, t: (b, 0, 0, 0))
    return pl.pallas_call(
        functools.partial(_wkv_kernel, chunk=C, n_pairs=n_pairs),
        grid=(B, T // C),
        in_specs=[act] * 6 + [st] + [vec] * 5,
        out_specs=[act, st],
        out_shape=[jax.ShapeDtypeStruct((B, T, D), BF16),
                   jax.ShapeDtypeStruct((B, n_pairs, PAIR, PAIR), F32)],
        scratch_shapes=[pltpu.VMEM((C, D), F32)],
        compiler_params=_params(("arbitrary", "arbitrary")),
        name="wkv",
    )(r, lw, k, v, asig, g, s0_bd, kkw.reshape(1, D), kaw.reshape(1, D), rk.reshape(1, D),
      lng.reshape(1, D), lnb.reshape(1, D))


def _state_to_pairs(s):
    B, H, n, _ = s.shape
    s = s.reshape(B, H // 2, 2, n, n)
    z = jnp.zeros((B, H // 2, n, n), s.dtype)
    top = jnp.concatenate([s[:, :, 0], z], axis=-1)
    bot = jnp.concatenate([z, s[:, :, 1]], axis=-1)
    return jnp.concatenate([top, bot], axis=-2)


def _pairs_to_state(sp):
    B, P, _, _ = sp.shape
    n = RW_HEAD
    h0 = sp[:, :, :n, :n]
    h1 = sp[:, :, n:, n:]
    return jnp.stack([h0, h1], axis=2).reshape(B, 2 * P, n, n)


def _ffn_kernel(x_ref, a_ref, wo_ref, conv0_ref, ng_ref, wg_ref, wv_ref, cw_ref, cb_ref,
                wd_ref, og_ref, o_ref, conv_out, xn_s, acc_s, carry_s,
                *, final_norm, nseq, seq_rows):
    t = pl.program_id(1)
    f = pl.program_id(2)
    nf = pl.num_programs(2)
    keep = CONV_W - 1

    @pl.when(f == 0)
    def _():
        x1 = x_ref[...] + jnp.dot(a_ref[...], wo_ref[...], preferred_element_type=F32)
        acc_s[...] = x1
        xn_s[...] = (_rms_scale(x1) * ng_ref[...]).astype(BF16)

    @pl.when(t == 0)
    def _():
        carry_s[f] = conv0_ref[0]

    xn = xn_s[...]
    gate = jnp.dot(xn, wg_ref[...], preferred_element_type=F32)
    val = jnp.dot(xn, wv_ref[...], preferred_element_type=F32)
    prev = carry_s[f]
    rows = _row_ids(gate.shape)
    g1 = jnp.where(rows == 0, prev[1:2, :], pltpu.roll(gate, 1, axis=0))
    g2 = jnp.where(rows == 0, prev[0:1, :],
                   jnp.where(rows == 1, prev[1:2, :], pltpu.roll(gate, 2, axis=0)))
    for s in range(1, nseq):
        first = s * seq_rows
        prev_s = conv0_ref[s]
        g1 = jnp.where(rows == first, prev_s[1:2, :], g1)
        g2 = jnp.where(rows == first, prev_s[0:1, :],
                       jnp.where(rows == first + 1, prev_s[1:2, :], g2))
    c = cb_ref[...] + cw_ref[0:1, :] * g2
    c = c + cw_ref[1:2, :] * g1
    c = c + cw_ref[2:3, :] * gate
    y = (c * jax.nn.sigmoid(c)) * val
    acc_s[...] += jnp.dot(y.astype(BF16), wd_ref[...], preferred_element_type=F32)
    tf = gate.shape[1]
    cols = pl.ds(pl.multiple_of(f * tf, tf), tf)
    for s in range(nseq):
        conv_out[s, :, cols] = gate[(s + 1) * seq_rows - keep:(s + 1) * seq_rows, :]
    carry_s[f] = gate[nseq * seq_rows - keep:nseq * seq_rows, :]

    @pl.when(f == nf - 1)
    def _():
        out = acc_s[...]
        if final_norm:
            out = _rms_scale(out) * og_ref[...]
        o_ref[...] = out


def _ffn(x, a, wo, conv0, ng, wup, cw, cb, wdown, og, *, layer, final_norm):
    B, T, D = x.shape
    F = wdown.shape[1]
    Ka = a.shape[-1]
    G, nseq, tm = _row_tiling(B, T)
    R = B * T // G
    tf = _tile(F, 512)
    nf = F // tf
    keep = CONV_W - 1
    row = lambda b, t, f: (0, 0)
    out, conv = pl.pallas_call(
        functools.partial(_ffn_kernel, final_norm=final_norm, nseq=nseq, seq_rows=tm // nseq),
        grid=(G, R // tm, nf),
        in_specs=[
            pl.BlockSpec((None, tm, D), lambda b, t, f: (b, t, 0)),
            pl.BlockSpec((None, tm, Ka), lambda b, t, f: (b, t, 0)),
            pl.BlockSpec((Ka, D), row, pipeline_mode=pl.Buffered(1)),
            pl.BlockSpec((nseq, keep, tf), lambda b, t, f: (b, 0, f)),
            pl.BlockSpec((1, D), row),
            pl.BlockSpec((None, D, tf), lambda b, t, f: (layer, 0, f)),
            pl.BlockSpec((None, D, tf), lambda b, t, f: (layer, 0, f + nf)),
            pl.BlockSpec((CONV_W, tf), lambda b, t, f: (0, f)),
            pl.BlockSpec((1, tf), lambda b, t, f: (0, f)),
            pl.BlockSpec((None, tf, D), lambda b, t, f: (layer, f, 0)),
            pl.BlockSpec((1, D), row),
        ],
        out_specs=[pl.BlockSpec((None, tm, D), lambda b, t, f: (b, t, 0)),
                   pl.BlockSpec((nseq, keep, F), lambda b, t, f: (b, 0, 0))],
        out_shape=[jax.ShapeDtypeStruct((G, R, D), F32),
                   jax.ShapeDtypeStruct((B, keep, F), F32)],
        scratch_shapes=[pltpu.VMEM((tm, D), BF16), pltpu.VMEM((tm, D), F32),
                        pltpu.VMEM((nf, keep, tf), F32)],
        compiler_params=_params(("arbitrary", "arbitrary", "arbitrary")),
        name="conv_ffn",
    )(x.reshape(G, R, D), a.reshape(G, R, Ka), wo, conv0, ng.reshape(1, D), wup, wup, cw,
      cb.reshape(1, F), wdown, og.reshape(1, D))
    return out.reshape(B, T, D), conv


def _kvq_kernel(x_ref, gkv_ref, gq_ref, wk_ref, wv_ref, wq_ref,
                k_out, v_out, kb_out, vb_out, q_out, xkv_s, xq_s, *, q_scale):
    @pl.when(pl.program_id(1) == 0)
    def _():
        xh = _rms_scale(x_ref[...])
        xkv_s[...] = (xh * gkv_ref[...]).astype(BF16)
        xq_s[...] = (xh * gq_ref[...]).astype(BF16)

    xkv = xkv_s[...]
    k = jnp.dot(xkv, wk_ref[...], preferred_element_type=F32)
    v = jnp.dot(xkv, wv_ref[...], preferred_element_type=F32)
    k_out[...] = k
    v_out[...] = v
    kb_out[...] = k.astype(BF16)
    vb_out[...] = v.astype(BF16)
    q = jnp.dot(xq_s[...], wq_ref[...], preferred_element_type=F32)
    q_out[...] = (q * q_scale).astype(BF16)


def _kvq(x, gkv, gq, wkv, wq):
    M, D = x.shape
    N = wq.shape[1]
    tm = _tile(M, 512)
    tn = _tile(N, 512)
    nn = N // tn
    row = lambda i, j: (0, 0)
    blk = pl.BlockSpec((tm, tn), lambda i, j: (i, j))
    return pl.pallas_call(
        functools.partial(_kvq_kernel, q_scale=float(SB_HEAD_DIM) ** -0.5 * LOG2_E),
        grid=(M // tm, nn),
        in_specs=[pl.BlockSpec((tm, D), lambda i, j: (i, 0)),
                  pl.BlockSpec((1, D), row), pl.BlockSpec((1, D), row),
                  pl.BlockSpec((D, tn), lambda i, j: (0, j)),
                  pl.BlockSpec((D, tn), lambda i, j: (0, j + nn)),
                  pl.BlockSpec((D, tn), lambda i, j: (0, j))],
        out_specs=[blk] * 5,
        out_shape=[jax.ShapeDtypeStruct((M, N), F32), jax.ShapeDtypeStruct((M, N), F32),
                   jax.ShapeDtypeStruct((M, N), BF16), jax.ShapeDtypeStruct((M, N), BF16),
                   jax.ShapeDtypeStruct((M, N), BF16)],
        scratch_shapes=[pltpu.VMEM((tm, D), BF16), pltpu.VMEM((tm, D), BF16)],
        compiler_params=_params(("parallel", "arbitrary")),
        name="kvq_proj",
    )(x, gkv.reshape(1, D), gq.reshape(1, D), wkv, wkv, wq)


SB_DEAD = 152.0


def _sb_block(q, kb, vb, tri, carry, valid, on=None):
    z = _dot_nt(q, kb)
    sp = jnp.maximum(z, 0.0) + jnp.log2(1.0 + jnp.exp2(jnp.minimum(z, -z)))
    if valid is not None:
        sp = jnp.where(valid, sp, 0.0)
    if on is not None:
        sp = jnp.where(on, sp, 0.0)
    later = jnp.dot(sp.astype(BF16), tri, preferred_element_type=F32)
    logw = z - sp - later
    if carry is not None:
        logw = logw - carry
    w = jnp.exp2(logw)
    if valid is not None:
        w = jnp.where(valid, w, 0.0)
    if on is not None:
        w = jnp.where(on, w, 0.0)
    out = jnp.dot(w.astype(BF16), vb.astype(BF16), preferred_element_type=F32)
    return out, later[:, 0:1] + sp[:, 0:1]


def _sb_prompt_kernel(q_ref, k_ref, v_ref, tri_ref, o_ref, acc_s, carry_s, *, nsub, tb):
    qi = pl.program_id(2)
    blk0 = qi * nsub
    tri = tri_ref[...]
    valid = _col_ids((tb, tb)) < _row_ids((tb, tb))
    rows = lambda i: slice(i * tb, (i + 1) * tb)

    def kv_block(j):
        k0 = pl.multiple_of(j * tb, tb)
        return k_ref[pl.ds(k0, tb), :], v_ref[pl.ds(k0, tb), :]

    kvs = [kv_block(blk0 + i) for i in range(nsub)]
    acc, car = [], []
    for i in range(nsub):
        out, tot = _sb_block(q_ref[rows(i), :], kvs[i][0], kvs[i][1], tri, None, valid)
        acc.append(out)
        car.append(tot)
    for i in range(nsub):
        if i == 0:
            kb, vb = kv_block(jnp.maximum(blk0 - 1, 0))
            out, tot = _sb_block(q_ref[rows(0), :], kb, vb, tri, car[0], None, on=blk0 > 0)
        else:
            out, tot = _sb_block(q_ref[rows(i), :], kvs[i - 1][0], kvs[i - 1][1], tri,
                                 car[i], None)
        acc_s[i] = acc[i] + out
        carry_s[i] = car[i] + tot

    def live_min(t):
        m = jnp.float32(jnp.inf)
        for i in range(nsub):
            m = jnp.where(blk0 + i - 2 - t >= 0, jnp.minimum(m, jnp.min(carry_s[i])), m)
        return m

    def cond(st):
        t, m = st
        return jnp.logical_and(blk0 + nsub - 3 - t >= 0, m < SB_DEAD)

    def body(st):
        t, _ = st
        for i in range(nsub):
            j = blk0 + i - 2 - t
            kb, vb = kv_block(jnp.maximum(j, 0))
            out, tot = _sb_block(q_ref[rows(i), :], kb, vb, tri, carry_s[i], None, on=j >= 0)
            acc_s[i] += out
            carry_s[i] += tot
        return t + 1, live_min(t + 1)

    lax.while_loop(cond, body, (jnp.int32(0), live_min(0)))
    for i in range(nsub):
        o_ref[rows(i), :] = acc_s[i].astype(o_ref.dtype)


def _sb_decode_kernel(q_ref, k_ref, v_ref, kn_ref, vn_ref, kc_hbm, vc_hbm, trid_ref, trip_ref,
                      o_ref, acc_s, carry_s, kbuf, vbuf, sems, *, tq, tk, n_old):
    b = pl.program_id(0)
    h = pl.program_id(1)
    q = q_ref[...]
    valid = _col_ids((tq, tq)) < _row_ids((tq, tq))
    trip = trip_ref[...]
    acc, carry = _sb_block(q, k_ref[...], v_ref[...], trid_ref[...], None, valid)
    out, tot = _sb_block(q, kn_ref[...], vn_ref[...], trip, carry, None)
    acc_s[...] = acc + out
    carry_s[...] = carry + tot

    def cond(st):
        t, m = st
        return jnp.logical_and(t < n_old, m < SB_DEAD)

    def body(st):
        t, _ = st
        k0 = pl.multiple_of((n_old - 1 - t) * tk, tk)
        copies = [pltpu.make_async_copy(src.at[b, pl.ds(k0, tk), h, :], dst, sems.at[i])
                  for i, (src, dst) in enumerate(((kc_hbm, kbuf), (vc_hbm, vbuf)))]
        for c in copies:
            c.start()
        for c in copies:
            c.wait()
        out, tot = _sb_block(q, kbuf[...], vbuf[...], trip, carry_s[...], None)
        acc_s[...] += out
        carry_s[...] += tot
        return t + 1, jnp.min(carry_s[...])

    lax.while_loop(cond, body, (jnp.int32(0), jnp.min(carry_s[...])))
    o_ref[...] = acc_s[...].astype(o_ref.dtype)


def _suffix_ones(n):
    return (_row_ids((n, n)) > _col_ids((n, n))).astype(BF16)


SB_BLOCK = 256
SB_QSUB = 4


def _sb_attention(q, k, v, k_past=None, v_past=None):
    B, T, D = q.shape
    H = D // SB_HEAD_DIM
    hd = SB_HEAD_DIM
    full = lambda n: pl.BlockSpec((n, n), lambda b, h, i: (0, 0))
    seq = pl.BlockSpec((None, T, hd), lambda b, h, i: (b, 0, h))
    if k_past is None:
        tb = _tile(T, SB_BLOCK)
        nsub = _tile(T // tb, SB_QSUB)
        tq = nsub * tb
        qspec = pl.BlockSpec((None, tq, hd), lambda b, h, i: (b, i, h))
        return pl.pallas_call(
            functools.partial(_sb_prompt_kernel, nsub=nsub, tb=tb),
            grid=(B, H, T // tq),
            in_specs=[qspec, seq, seq, full(tb)],
            out_specs=qspec,
            out_shape=jax.ShapeDtypeStruct((B, T, D), BF16),
            scratch_shapes=[pltpu.VMEM((nsub, tb, hd), F32), pltpu.VMEM((nsub, tb, 1), F32)],
            compiler_params=_params(("parallel", "parallel", "arbitrary")),
            name="sb_attention",
        )(q, k, v, _suffix_ones(tb))
    P = k_past.shape[1]
    tk = _tile(P, SB_BLOCK)
    near = lambda c: c[:, P - tk:].reshape(B, tk, D).astype(BF16)
    qspec = pl.BlockSpec((None, T, hd), lambda b, h, i: (b, 0, h))
    nspec = pl.BlockSpec((None, tk, hd), lambda b, h, i: (b, 0, h))
    hbm = pl.BlockSpec(memory_space=pl.ANY)
    return pl.pallas_call(
        functools.partial(_sb_decode_kernel, tq=T, tk=tk, n_old=P // tk - 1),
        grid=(B, H, 1),
        in_specs=[qspec, seq, seq, nspec, nspec, hbm, hbm, full(T), full(tk)],
        out_specs=qspec,
        out_shape=jax.ShapeDtypeStruct((B, T, D), BF16),
        scratch_shapes=[pltpu.VMEM((T, hd), F32), pltpu.VMEM((T, 1), F32),
                        pltpu.VMEM((tk, hd), F32), pltpu.VMEM((tk, hd), F32),
                        pltpu.SemaphoreType.DMA((2,))],
        compiler_params=_params(("parallel", "parallel", "arbitrary")),
        name="sb_attention_decode",
    )(q, k, v, near(k_past), near(v_past), k_past, v_past, _suffix_ones(T), _suffix_ones(tk))


def _trunk(x, shift0, wkv0, conv0, k_past, v_past, p):
    B, T, D = x.shape
    M = B * T
    r, k, v, lw, asig, g, shift = _rw_proj(
        x, shift0[0], p['a_norm_g'][0], p['rw_mu'][0], p['rw_w0'][0], p['rw_a0'][0],
        p['rw_wr'], p['rw_wk'], p['rw_wv'], p['rw_w1'], p['rw_w2'], p['rw_a1'], p['rw_a2'],
        p['rw_g1'], p['rw_g2'])
    o, s_bd = _wkv(r, lw, k, v, asig, g, _state_to_pairs(wkv0[0]),
                   p['rw_kk'][0], p['rw_ka'][0], p['rw_rk'][0].reshape(-1),
                   p['rw_lnx_g'][0], p['rw_lnx_b'][0])
    x, conv_a = _ffn(x, o, p['rw_wo'], conv0[0], p['f_norm_g'][0], p['f_wup'],
                     p['f_conv_w'][0], p['f_conv_b'][0], p['f_wdown'], p['out_norm_g'],
                     layer=0, final_norm=False)
    k_sh, v_sh, kb, vb, q = _kvq(x.reshape(M, D), p['kv_norm_g'], p['b_norm_g'][0],
                                 p['w_kv'], p['sb_wq'])
    d_att = q.shape[1]
    att = _sb_attention(q.reshape(B, T, d_att), kb.reshape(B, T, d_att), vb.reshape(B, T, d_att),
                        k_past, v_past)
    y, conv_b = _ffn(x, att, p['sb_wo'], conv0[1], p['f_norm_g'][1], p['f_wup'],
                     p['f_conv_w'][1], p['f_conv_b'][1], p['f_wdown'], p['out_norm_g'],
                     layer=1, final_norm=True)
    H = d_att // SB_HEAD_DIM
    return (y, _pairs_to_state(s_bd)[None], shift.reshape(1, B, D), jnp.stack([conv_a, conv_b]),
            k_sh.reshape(B, T, H, SB_HEAD_DIM), v_sh.reshape(B, T, H, SB_HEAD_DIM))


def kernel(x_prompt, x_sample, cache_k, cache_v, state_wkv, state_shift, state_conv, a_norm_g, rw_mu, rw_w0, rw_w1, rw_w2, rw_a0, rw_a1, rw_a2, rw_g1, rw_g2, rw_kk, rw_ka, rw_rk, rw_wr, rw_wk, rw_wv, rw_wo, rw_lnx_g, rw_lnx_b, kv_norm_g, w_kv, b_norm_g, sb_wq, sb_wo, f_norm_g, f_wup, f_conv_w, f_conv_b, f_wdown, out_norm_g):
    bf = lambda w: w.astype(BF16)
    p = dict(a_norm_g=a_norm_g, rw_mu=rw_mu, rw_w0=rw_w0, rw_a0=rw_a0,
             rw_w1=bf(rw_w1[0]), rw_w2=bf(rw_w2[0]), rw_a1=bf(rw_a1[0]), rw_a2=bf(rw_a2[0]),
             rw_g1=bf(rw_g1[0]), rw_g2=bf(rw_g2[0]),
             rw_kk=rw_kk, rw_ka=rw_ka, rw_rk=rw_rk,
             rw_wr=bf(rw_wr[0]), rw_wk=bf(rw_wk[0]), rw_wv=bf(rw_wv[0]), rw_wo=bf(rw_wo[0]),
             rw_lnx_g=rw_lnx_g, rw_lnx_b=rw_lnx_b, kv_norm_g=kv_norm_g, w_kv=bf(w_kv),
             b_norm_g=b_norm_g, sb_wq=bf(sb_wq[0]), sb_wo=bf(sb_wo[0]), f_norm_g=f_norm_g,
             f_wup=bf(f_wup), f_conv_w=f_conv_w, f_conv_b=f_conv_b, f_wdown=bf(f_wdown),
             out_norm_g=out_norm_g)
    B, _, D = x_prompt.shape
    n_a = state_shift.shape[0]
    depth = state_conv.shape[0]
    F = state_conv.shape[-1]
    H = state_wkv.shape[2]
    shift0 = jnp.zeros((n_a, B, D), x_prompt.dtype)
    wkv0 = jnp.zeros((n_a, B, H, RW_HEAD, RW_HEAD), F32)
    conv0 = jnp.zeros((depth, B, CONV_W - 1, F), x_prompt.dtype)
    y_p, wkv_p, shift_p, conv_p, k_p, v_p = _trunk(x_prompt, shift0, wkv0, conv0, None, None, p)
    y_s, wkv_s, shift_s, conv_s, k_s, v_s = _trunk(
        x_sample, state_shift, state_wkv.astype(F32), state_conv, cache_k, cache_v, p)
    return (y_p, y_s, wkv_p.astype(state_wkv.dtype), shift_p, conv_p, k_p, v_p,
            wkv_s.astype(state_wkv.dtype), shift_s, conv_s, k_s, v_s)
```

```python
import functools

import jax
import jax.numpy as jnp
from jax import lax
from jax.experimental import pallas as pl
from jax.experimental.pallas import tpu as pltpu

F32 = jnp.float32
BF16 = jnp.bfloat16

RW_HEAD = 64
PAIR = 2 * RW_HEAD
SB_HEAD_DIM = 128
GN_EPS = 64e-5
NORM_EPS = 1e-6
CONV_W = 3
WKV_CHUNK = 64
NEUMANN_BLOCK = 16
WKV_CHUNKS_PER_STEP = 2
LOG2_E = 1.4426950408889634
ROW_TILE = 512
VMEM_LIMIT = 56 * 1024 * 1024


def _tile(n, pref):
    if n <= pref:
        return n
    t = pref
    while n % t:
        t //= 2
    return t


def _row_tiling(B, T):
    if T >= ROW_TILE:
        return B, 1, _tile(T, ROW_TILE)
    nseq = _tile(B, max(ROW_TILE // T, 1))
    return B // nseq, nseq, nseq * T


def _params(sem):
    return pltpu.CompilerParams(dimension_semantics=sem, vmem_limit_bytes=VMEM_LIMIT)


def _dot(a, b):
    return jnp.dot(a.astype(BF16), b.astype(BF16), preferred_element_type=F32)


def _dot_nt(a, b):
    return lax.dot_general(a.astype(BF16), b.astype(BF16), (((1,), (1,)), ((), ())),
                           preferred_element_type=F32)


def _dot_tn(a, b):
    return lax.dot_general(a.astype(BF16), b.astype(BF16), (((0,), (0,)), ((), ())),
                           preferred_element_type=F32)


def _rms_scale(x):
    return x * lax.rsqrt(jnp.mean(x * x, axis=-1, keepdims=True) + NORM_EPS)


def _softplus(u):
    return jnp.maximum(u, 0.0) + jnp.log1p(jnp.exp(-jnp.abs(u)))


def _row_ids(shape):
    return lax.broadcasted_iota(jnp.int32, shape, 0)


def _col_ids(shape):
    return lax.broadcasted_iota(jnp.int32, shape, 1)


def _rw_proj_kernel(x_ref, shift0_ref, ng_ref, mu_ref, w0_ref, a0_ref,
                    wr_ref, wk_ref, wv_ref, w1_ref, w2_ref, a1_ref, a2_ref, g1_ref, g2_ref,
                    r_out, k_out, v_out, lw_out, as_out, g_out, shift_out,
                    xr_s, xk_s, xv_s, hw_s, ha_s, hg_s, carry_s, *, nseq, seq_rows):
    t = pl.program_id(1)
    j = pl.program_id(2)

    @pl.when(j == 0)
    def _():
        @pl.when(t == 0)
        def _():
            carry_s[...] = shift0_ref[0]

        xn = _rms_scale(x_ref[...]) * ng_ref[...]
        rows = _row_ids(xn.shape)
        x_prev = jnp.where(rows == 0, carry_s[...], pltpu.roll(xn, 1, axis=0))
        for s in range(1, nseq):
            x_prev = jnp.where(rows == s * seq_rows, shift0_ref[s], x_prev)
        for s in range(nseq):
            shift_out[s] = xn[(s + 1) * seq_rows - 1:(s + 1) * seq_rows, :]
        carry_s[...] = xn[nseq * seq_rows - 1:nseq * seq_rows, :]
        xx = x_prev - xn
        mix = lambda i: (xn + xx * mu_ref[i:i + 1, :]).astype(BF16)
        xr_s[...] = mix(0)
        xk_s[...] = mix(2)
        xv_s[...] = mix(3)
        hw_s[...] = jnp.tanh(_dot(mix(1), w1_ref[...])).astype(BF16)
        ha_s[...] = _dot(mix(4), a1_ref[...]).astype(BF16)
        hg_s[...] = jax.nn.sigmoid(_dot(mix(5), g1_ref[...])).astype(BF16)

    r_out[...] = _dot(xr_s[...], wr_ref[...])
    k_out[...] = _dot(xk_s[...], wk_ref[...])
    v_out[...] = _dot(xv_s[...], wv_ref[...])
    w_log = -_softplus(-(w0_ref[...] + _dot(hw_s[...], w2_ref[...]))) - 0.5
    lw_out[...] = -jnp.exp(w_log)
    as_out[...] = jax.nn.sigmoid(a0_ref[...] + _dot(ha_s[...], a2_ref[...]))
    g_out[...] = _dot(hg_s[...], g2_ref[...])


def _rw_proj(x, shift0, ng, mu, w0, a0, wr, wk, wv, w1, w2, a1, a2, g1, g2):
    B, T, D = x.shape
    G, nseq, tm = _row_tiling(B, T)
    R = B * T // G
    tn = _tile(D, 512)
    lw_dim, la_dim, lg_dim = w1.shape[1], a1.shape[1], g1.shape[1]
    row = lambda b, t, j: (0, 0)
    colv = pl.BlockSpec((1, tn), lambda b, t, j: (0, j))
    act = pl.BlockSpec((None, tm, tn), lambda b, t, j: (b, t, j))
    out_sds = jax.ShapeDtypeStruct((G, R, D), F32)
    outs = pl.pallas_call(
        functools.partial(_rw_proj_kernel, nseq=nseq, seq_rows=tm // nseq),
        grid=(G, R // tm, D // tn),
        in_specs=[
            pl.BlockSpec((None, tm, D), lambda b, t, j: (b, t, 0)),
            pl.BlockSpec((nseq, 1, D), lambda b, t, j: (b, 0, 0)),
            pl.BlockSpec((1, D), row),
            pl.BlockSpec((6, D), row),
            colv, colv,
            pl.BlockSpec((D, tn), lambda b, t, j: (0, j)),
            pl.BlockSpec((D, tn), lambda b, t, j: (0, j)),
            pl.BlockSpec((D, tn), lambda b, t, j: (0, j)),
            pl.BlockSpec((D, lw_dim), row),
            pl.BlockSpec((lw_dim, tn), lambda b, t, j: (0, j)),
            pl.BlockSpec((D, la_dim), row),
            pl.BlockSpec((la_dim, tn), lambda b, t, j: (0, j)),
            pl.BlockSpec((D, lg_dim), row),
            pl.BlockSpec((lg_dim, tn), lambda b, t, j: (0, j)),
        ],
        out_specs=[act] * 6 + [pl.BlockSpec((nseq, 1, D), lambda b, t, j: (b, 0, 0))],
        out_shape=[out_sds] * 6 + [jax.ShapeDtypeStruct((B, 1, D), F32)],
        scratch_shapes=[
            pltpu.VMEM((tm, D), BF16), pltpu.VMEM((tm, D), BF16), pltpu.VMEM((tm, D), BF16),
            pltpu.VMEM((tm, lw_dim), BF16), pltpu.VMEM((tm, la_dim), BF16),
            pltpu.VMEM((tm, lg_dim), BF16), pltpu.VMEM((1, D), F32),
        ],
        compiler_params=_params(("arbitrary", "arbitrary", "arbitrary")),
        name="rw_proj",
    )(x.reshape(G, R, D), shift0.reshape(B, 1, D), ng.reshape(1, D), mu, w0.reshape(1, D),
      a0.reshape(1, D), wr, wk, wv, w1, w2, a1, a2, g1, g2)
    return [o.reshape(B, T, D) for o in outs[:6]] + [outs[6]]


def _wkv_kernel(r_ref, lw_ref, k_ref, v_ref, as_ref, g_ref, s0_ref,
                kkw_ref, kaw_ref, rk_ref, lng_ref, lnb_ref,
                o_ref, s_ref, c_s, *, chunk, n_sub, n_pairs):
    C = chunk
    C2 = 2 * C
    trow = lambda sc: slice(sc * C, (sc + 1) * C)

    @pl.when(pl.program_id(1) == 0)
    def _():
        s_ref[...] = s0_ref[...]

    tri = (_col_ids((C, C)) <= _row_ids((C, C))).astype(BF16)
    for sc in range(n_sub):
        lw_all = lw_ref[trow(sc), :]
        lw_hi = lw_all.astype(BF16)
        lw_r = lw_all - lw_hi.astype(F32)
        lw_mid = lw_r.astype(BF16)
        lw_lo = (lw_r - lw_mid.astype(F32)).astype(BF16)
        c_s[trow(sc), :] = (jnp.dot(tri, lw_hi, preferred_element_type=F32)
                            + jnp.dot(tri, lw_mid, preferred_element_type=F32)
                            + jnp.dot(tri, lw_lo, preferred_element_type=F32))

    head0 = _col_ids((C, PAIR)) < RW_HEAD
    ones_bd = ((_row_ids((PAIR, PAIR)) // RW_HEAD) == (_col_ids((PAIR, PAIR)) // RW_HEAD))
    ones_bd_bf = ones_bd.astype(BF16)

    rr = _row_ids((C2, C2))
    cc = _col_ids((C2, C2))
    same_head = (rr // C) == (cc // C)
    nb = NEUMANN_BLOCK
    same16 = (rr // nb) == (cc // nb)
    same32 = (rr // (2 * nb)) == (cc // (2 * nb))
    m_e1 = same32 & (~same16) if C >= 2 * nb else None
    m_e2 = same_head & (~same32) if C >= 4 * nb else None
    eye22 = (rr == cc).astype(F32)
    r12 = _row_ids((C, C2))
    c12 = _col_ids((C, C2))
    strict12 = (c12 % C) < r12
    incl12 = (c12 % C) <= r12
    left12 = c12 < C
    same16_c = ((c12 % C) // nb) == (r12 // nb)
    c16 = _col_ids((nb, C2))
    blk_of_lane = (c16 % C) // nb
    eye16 = ((c16 % nb) == _row_ids((nb, C2))).astype(F32)

    def expand(x):
        return jnp.where(same_head, jnp.concatenate([x, x], axis=0), 0.0)

    def expand16(x):
        return jnp.where(same16, jnp.concatenate([x] * (C2 // nb), axis=0), 0.0)

    def group_sum(x):
        return jnp.dot(x.astype(BF16), ones_bd_bf, preferred_element_type=F32)

    cat0 = lambda xs: jnp.concatenate(xs, axis=0)
    zero_h1 = lambda x: jnp.where(head0, x, 0.0)
    zero_h0 = lambda x: jnp.where(head0, 0.0, x)
    inv_n = 1.0 / RW_HEAD

    def state_free(items):
        n = range(len(items))
        tr = [trow(sc) for sc, _ in items]
        sl = [slice(p * PAIR, (p + 1) * PAIR) for _, p in items]
        r = [r_ref[tr[i], sl[i]] for i in n]
        lw = [lw_ref[tr[i], sl[i]] for i in n]
        k = [k_ref[tr[i], sl[i]] for i in n]
        v = [v_ref[tr[i], sl[i]] for i in n]
        asig = [as_ref[tr[i], sl[i]] for i in n]
        c = [c_s[tr[i], sl[i]] for i in n]
        c_end = [x[C - 1:C, :] for x in c]

        kkv = [k[i] * kkw_ref[:, sl[i]] for i in n]
        ss = [group_sum(x * x) for x in kkv]
        kk = [kkv[i] / jnp.maximum(jnp.sqrt(ss[i]), 1e-12) for i in n]
        b_in = [kk[i] * asig[i] for i in n]
        k_in = [k[i] * (1.0 + (asig[i] - 1.0) * kaw_ref[:, sl[i]]) for i in n]
        bonus_s = [group_sum(r[i] * k_in[i] * rk_ref[:, sl[i]]) for i in n]

        e_neg = [jnp.exp(-x) for x in c]
        a_t = [(-kk[i]) * jnp.exp(c[i] - lw[i]) for i in n]
        r_t = [r[i] * jnp.exp(c[i]) for i in n]
        b_h = [b_in[i] * e_neg[i] for i in n]
        k_h = [k_in[i] * e_neg[i] for i in n]
        e_end = [jnp.exp(c_end[i] - c[i]) for i in n]
        RE = [cat0([b_in[i] * e_end[i], k_in[i] * e_end[i]]) for i in n]

        L = [cat0([a_t[i], r_t[i]]) for i in n]
        G = [_dot_nt(L[i], cat0([zero_h1(b_h[i]), zero_h1(k_h[i]),
                                 zero_h0(k_h[i]), zero_h0(b_h[i])])) for i in n]
        G0 = [x[:, :C2] for x in G]
        G1 = [x[:, C2:] for x in G]
        g0t = [x[:C] for x in G0]
        g1t = [x[:C] for x in G1]

        n_c = [jnp.where(strict12, jnp.where(left12, g0t[i], g1t[i]), 0.0) for i in n]
        n_bd = [expand(x) for x in n_c]
        d1 = [sum(jnp.where(blk_of_lane == b, x[b * nb:(b + 1) * nb], 0.0)
                  for b in range(C // nb)) for x in n_c]
        d1_bd = [expand16(x) for x in d1]
        d2 = [_dot(d1[i], d1_bd[i]) for i in n]
        d2_bd = [expand16(x) for x in d2]
        d4 = [_dot(d2[i], d2_bd[i]) for i in n]
        d4_bd = [expand16(x) for x in d4]
        d8 = [_dot(d4[i], d4_bd[i]) for i in n]
        pa = [_dot(eye16 + d1[i], eye22 + d2_bd[i]) for i in n]
        pb = [_dot(eye16 + d4[i], eye22 + expand16(d8[i])) for i in n]
        t16 = [_dot(pa[i], expand16(pb[i])) for i in n]
        tcat = [jnp.where(same16_c, cat0([x] * (C // nb)), 0.0) for x in t16]
        for m_e in (m_e1, m_e2):
            if m_e is not None:
                t_bd = [expand(x) for x in tcat]
                x = [_dot(tcat[i], jnp.where(m_e, n_bd[i], 0.0)) for i in n]
                tcat = [tcat[i] + _dot(x[i], t_bd[i]) for i in n]

        v0 = [jnp.where(head0, x, 0.0) for x in v]
        v1 = [jnp.where(head0, 0.0, x) for x in v]
        av = [_dot(jnp.where(strict12, jnp.where(left12, g1t[i], g0t[i]), 0.0),
                   cat0([v1[i], v0[i]])) for i in n]
        bot = [jnp.concatenate([jnp.where(incl12, G0[i][C:], 0.0),
                                jnp.where(incl12, G1[i][C:], 0.0)], axis=1) for i in n]
        return dict(L=L, tcat=tcat, av=av, bot=bot, RE=RE, v=v, v0=v0, v1=v1,
                    decay=[jnp.exp(x) for x in c_end], bonus=bonus_s)

    def state_chain(sc, pre, sel):
        n = range(len(sel))
        at = lambda name: [pre[name][j] for j in sel]
        L, tcat, av, bot, RE, v, v0, v1, decay, bonus_s = (
            at(x) for x in ("L", "tcat", "av", "bot", "RE", "v", "v0", "v1", "decay", "bonus"))
        sl = [slice(p * PAIR, (p + 1) * PAIR) for p in range(n_pairs)]
        S = [s_ref[p] for p in range(n_pairs)]
        P = [_dot_nt(L[i], S[i]) for i in n]
        W = [P[i][:C] + av[i] for i in n]
        U = [_dot(tcat[i], cat0([zero_h1(W[i]), zero_h0(W[i])])) for i in n]
        Y = [P[i][C:] + _dot(bot[i], cat0([zero_h1(U[i]), v0[i], v1[i], zero_h0(U[i])]))
             for i in n]
        for i in n:
            upd = _dot_tn(cat0([U[i], v[i]]), RE[i])
            s_ref[i] = S[i] * decay[i] + jnp.where(ones_bd, upd, 0.0)

        mean = [group_sum(x) * inv_n for x in Y]
        dlt = [Y[i] - mean[i] for i in n]
        var = [group_sum(x * x) * inv_n for x in dlt]
        for i in n:
            yn = dlt[i] * lax.rsqrt(var[i] + GN_EPS) * lng_ref[:, sl[i]] + lnb_ref[:, sl[i]]
            out = (yn + bonus_s[i] * v[i]) * g_ref[trow(sc), sl[i]]
            o_ref[trow(sc), sl[i]] = out.astype(o_ref.dtype)

    items = [(sc, p) for sc in range(n_sub) for p in range(n_pairs)]
    pre = state_free(items)
    for sc in range(n_sub):
        state_chain(sc, pre, [sc * n_pairs + p for p in range(n_pairs)])


def _wkv(r, lw, k, v, asig, g, s0_bd, kkw, kaw, rk, lng, lnb):
    B, T, D = r.shape
    C = _tile(T, WKV_CHUNK)
    n_sub = _tile(T // C, WKV_CHUNKS_PER_STEP)
    n_pairs = D // PAIR
    act = pl.BlockSpec((None, n_sub * C, D), lambda b, t: (b, t, 0))
    vec = pl.BlockSpec((1, D), lambda b, t: (0, 0))
    st = pl.BlockSpec((None, n_pairs, PAIR, PAIR), lambda b, t: (b, 0, 0, 0))
    return pl.pallas_call(
        functools.partial(_wkv_kernel, chunk=C, n_sub=n_sub, n_pairs=n_pairs),
        grid=(B, T // (n_sub * C)),
        in_specs=[act] * 6 + [st] + [vec] * 5,
        out_specs=[act, st],
        out_shape=[jax.ShapeDtypeStruct((B, T, D), BF16),
                   jax.ShapeDtypeStruct((B, n_pairs, PAIR, PAIR), F32)],
        scratch_shapes=[pltpu.VMEM((n_sub * C, D), F32)],
        compiler_params=_params(("arbitrary", "arbitrary")),
        name="wkv",
    )(r, lw, k, v, asig, g, s0_bd, kkw.reshape(1, D), kaw.reshape(1, D), rk.reshape(1, D),
      lng.reshape(1, D), lnb.reshape(1, D))


def _state_to_pairs(s):
    B, H, n, _ = s.shape
    s = s.reshape(B, H // 2, 2, n, n)
    z = jnp.zeros((B, H // 2, n, n), s.dtype)
    top = jnp.concatenate([s[:, :, 0], z], axis=-1)
    bot = jnp.concatenate([z, s[:, :, 1]], axis=-1)
    return jnp.concatenate([top, bot], axis=-2)


def _pairs_to_state(sp):
    B, P, _, _ = sp.shape
    n = RW_HEAD
    h0 = sp[:, :, :n, :n]
    h1 = sp[:, :, n:, n:]
    return jnp.stack([h0, h1], axis=2).reshape(B, 2 * P, n, n)


def _ffn_kernel(x_ref, a_ref, wo_ref, conv0_ref, ng_ref, wg_ref, wv_ref, cw_ref, cb_ref,
                wd_ref, og_ref, o_ref, conv_out, xn_s, acc_s, carry_s,
                *, final_norm, nseq, seq_rows):
    t = pl.program_id(1)
    f = pl.program_id(2)
    nf = pl.num_programs(2)
    keep = CONV_W - 1

    @pl.when(f == 0)
    def _():
        x1 = x_ref[...] + jnp.dot(a_ref[...], wo_ref[...], preferred_element_type=F32)
        acc_s[...] = x1
        xn_s[...] = (_rms_scale(x1) * ng_ref[...]).astype(BF16)

    @pl.when(t == 0)
    def _():
        carry_s[f] = conv0_ref[0]

    xn = xn_s[...]
    gate = jnp.dot(xn, wg_ref[...], preferred_element_type=F32)
    val = jnp.dot(xn, wv_ref[...], preferred_element_type=F32)
    prev = carry_s[f]
    rows = _row_ids(gate.shape)
    g1 = jnp.where(rows == 0, prev[1:2, :], pltpu.roll(gate, 1, axis=0))
    g2 = jnp.where(rows == 0, prev[0:1, :],
                   jnp.where(rows == 1, prev[1:2, :], pltpu.roll(gate, 2, axis=0)))
    for s in range(1, nseq):
        first = s * seq_rows
        prev_s = conv0_ref[s]
        g1 = jnp.where(rows == first, prev_s[1:2, :], g1)
        g2 = jnp.where(rows == first, prev_s[0:1, :],
                       jnp.where(rows == first + 1, prev_s[1:2, :], g2))
    c = cb_ref[...] + cw_ref[0:1, :] * g2
    c = c + cw_ref[1:2, :] * g1
    c = c + cw_ref[2:3, :] * gate
    y = (c * jax.nn.sigmoid(c)) * val
    acc_s[...] += jnp.dot(y.astype(BF16), wd_ref[...], preferred_element_type=F32)
    tf = gate.shape[1]
    cols = pl.ds(pl.multiple_of(f * tf, tf), tf)
    for s in range(nseq):
        conv_out[s, :, cols] = gate[(s + 1) * seq_rows - keep:(s + 1) * seq_rows, :]
    carry_s[f] = gate[nseq * seq_rows - keep:nseq * seq_rows, :]

    @pl.when(f == nf - 1)
    def _():
        out = acc_s[...]
        if final_norm:
            out = _rms_scale(out) * og_ref[...]
        o_ref[...] = out


def _ffn(x, a, wo, conv0, ng, wup, cw, cb, wdown, og, *, layer, final_norm):
    B, T, D = x.shape
    F = wdown.shape[1]
    Ka = a.shape[-1]
    G, nseq, tm = _row_tiling(B, T)
    R = B * T // G
    tf = _tile(F, 512)
    nf = F // tf
    keep = CONV_W - 1
    row = lambda b, t, f: (0, 0)
    out, conv = pl.pallas_call(
        functools.partial(_ffn_kernel, final_norm=final_norm, nseq=nseq, seq_rows=tm // nseq),
        grid=(G, R // tm, nf),
        in_specs=[
            pl.BlockSpec((None, tm, D), lambda b, t, f: (b, t, 0)),
            pl.BlockSpec((None, tm, Ka), lambda b, t, f: (b, t, 0)),
            pl.BlockSpec((Ka, D), row, pipeline_mode=pl.Buffered(1)),
            pl.BlockSpec((nseq, keep, tf), lambda b, t, f: (b, 0, f)),
            pl.BlockSpec((1, D), row),
            pl.BlockSpec((None, D, tf), lambda b, t, f: (layer, 0, f)),
            pl.BlockSpec((None, D, tf), lambda b, t, f: (layer, 0, f + nf)),
            pl.BlockSpec((CONV_W, tf), lambda b, t, f: (0, f)),
            pl.BlockSpec((1, tf), lambda b, t, f: (0, f)),
            pl.BlockSpec((None, tf, D), lambda b, t, f: (layer, f, 0)),
            pl.BlockSpec((1, D), row),
        ],
        out_specs=[pl.BlockSpec((None, tm, D), lambda b, t, f: (b, t, 0)),
                   pl.BlockSpec((nseq, keep, F), lambda b, t, f: (b, 0, 0))],
        out_shape=[jax.ShapeDtypeStruct((G, R, D), F32),
                   jax.ShapeDtypeStruct((B, keep, F), F32)],
        scratch_shapes=[pltpu.VMEM((tm, D), BF16), pltpu.VMEM((tm, D), F32),
                        pltpu.VMEM((nf, keep, tf), F32)],
        compiler_params=_params(("arbitrary", "arbitrary", "arbitrary")),
        name="conv_ffn",
    )(x.reshape(G, R, D), a.reshape(G, R, Ka), wo, conv0, ng.reshape(1, D), wup, wup, cw,
      cb.reshape(1, F), wdown, og.reshape(1, D))
    return out.reshape(B, T, D), conv


def _kvq_kernel(x_ref, gkv_ref, gq_ref, wk_ref, wv_ref, wq_ref,
                k_out, v_out, kb_out, vb_out, q_out, xkv_s, xq_s, *, q_scale):
    @pl.when(pl.program_id(1) == 0)
    def _():
        xh = _rms_scale(x_ref[...])
        xkv_s[...] = (xh * gkv_ref[...]).astype(BF16)
        xq_s[...] = (xh * gq_ref[...]).astype(BF16)

    xkv = xkv_s[...]
    k = jnp.dot(xkv, wk_ref[...], preferred_element_type=F32)
    v = jnp.dot(xkv, wv_ref[...], preferred_element_type=F32)
    k_out[...] = k
    v_out[...] = v
    kb_out[...] = k.astype(BF16)
    vb_out[...] = v.astype(BF16)
    q = jnp.dot(xq_s[...], wq_ref[...], preferred_element_type=F32)
    q_out[...] = (q * q_scale).astype(BF16)


def _kvq(x, gkv, gq, wkv, wq):
    M, D = x.shape
    N = wq.shape[1]
    tm = _tile(M, 512)
    tn = _tile(N, 512)
    nn = N // tn
    row = lambda i, j: (0, 0)
    blk = pl.BlockSpec((tm, tn), lambda i, j: (i, j))
    return pl.pallas_call(
        functools.partial(_kvq_kernel, q_scale=float(SB_HEAD_DIM) ** -0.5 * LOG2_E),
        grid=(M // tm, nn),
        in_specs=[pl.BlockSpec((tm, D), lambda i, j: (i, 0)),
                  pl.BlockSpec((1, D), row), pl.BlockSpec((1, D), row),
                  pl.BlockSpec((D, tn), lambda i, j: (0, j)),
                  pl.BlockSpec((D, tn), lambda i, j: (0, j + nn)),
                  pl.BlockSpec((D, tn), lambda i, j: (0, j))],
        out_specs=[blk] * 5,
        out_shape=[jax.ShapeDtypeStruct((M, N), F32), jax.ShapeDtypeStruct((M, N), F32),
                   jax.ShapeDtypeStruct((M, N), BF16), jax.ShapeDtypeStruct((M, N), BF16),
                   jax.ShapeDtypeStruct((M, N), BF16)],
        scratch_shapes=[pltpu.VMEM((tm, D), BF16), pltpu.VMEM((tm, D), BF16)],
        compiler_params=_params(("parallel", "arbitrary")),
        name="kvq_proj",
    )(x, gkv.reshape(1, D), gq.reshape(1, D), wkv, wkv, wq)


SB_DEAD = 152.0


def _sb_block(q, kb, vb, tri, carry, valid, on=None):
    z = _dot_nt(q, kb)
    sp = jnp.maximum(z, 0.0) + jnp.log2(1.0 + jnp.exp2(jnp.minimum(z, -z)))
    if valid is not None:
        sp = jnp.where(valid, sp, 0.0)
    if on is not None:
        sp = jnp.where(on, sp, 0.0)
    later = jnp.dot(sp.astype(BF16), tri, preferred_element_type=F32)
    logw = z - sp - later
    if carry is not None:
        logw = logw - carry
    w = jnp.exp2(logw)
    if valid is not None:
        w = jnp.where(valid, w, 0.0)
    if on is not None:
        w = jnp.where(on, w, 0.0)
    out = jnp.dot(w.astype(BF16), vb.astype(BF16), preferred_element_type=F32)
    return out, later[:, 0:1] + sp[:, 0:1]


def _sb_prompt_kernel(q_ref, k_ref, v_ref, tri_ref, o_ref, acc_s, carry_s, *, nsub, tb):
    qi = pl.program_id(2)
    blk0 = qi * nsub
    tri = tri_ref[...]
    valid = _col_ids((tb, tb)) < _row_ids((tb, tb))
    rows = lambda i: slice(i * tb, (i + 1) * tb)

    def kv_block(j):
        k0 = pl.multiple_of(j * tb, tb)
        return k_ref[pl.ds(k0, tb), :], v_ref[pl.ds(k0, tb), :]

    kvs = [kv_block(blk0 + i) for i in range(nsub)]
    acc, car = [], []
    for i in range(nsub):
        out, tot = _sb_block(q_ref[rows(i), :], kvs[i][0], kvs[i][1], tri, None, valid)
        acc.append(out)
        car.append(tot)
    for i in range(nsub):
        if i == 0:
            kb, vb = kv_block(jnp.maximum(blk0 - 1, 0))
            out, tot = _sb_block(q_ref[rows(0), :], kb, vb, tri, car[0], None, on=blk0 > 0)
        else:
            out, tot = _sb_block(q_ref[rows(i), :], kvs[i - 1][0], kvs[i - 1][1], tri,
                                 car[i], None)
        acc_s[i] = acc[i] + out
        carry_s[i] = car[i] + tot

    def live_min(t):
        m = jnp.float32(jnp.inf)
        for i in range(nsub):
            m = jnp.where(blk0 + i - 2 - t >= 0, jnp.minimum(m, jnp.min(carry_s[i])), m)
        return m

    def cond(st):
        t, m = st
        return jnp.logical_and(blk0 + nsub - 3 - t >= 0, m < SB_DEAD)

    def body(st):
        t, _ = st
        for i in range(nsub):
            j = blk0 + i - 2 - t
            kb, vb = kv_block(jnp.maximum(j, 0))
            out, tot = _sb_block(q_ref[rows(i), :], kb, vb, tri, carry_s[i], None, on=j >= 0)
            acc_s[i] += out
            carry_s[i] += tot
        return t + 1, live_min(t + 1)

    lax.while_loop(cond, body, (jnp.int32(0), live_min(0)))
    for i in range(nsub):
        o_ref[rows(i), :] = acc_s[i].astype(o_ref.dtype)


def _sb_decode_kernel(q_ref, k_ref, v_ref, kn_ref, vn_ref, kc_hbm, vc_hbm, trid_ref, trip_ref,
                      o_ref, acc_s, carry_s, kbuf, vbuf, sems, *, tq, tk, n_old):
    b = pl.program_id(0)
    h = pl.program_id(1)
    q = q_ref[...]
    valid = _col_ids((tq, tq)) < _row_ids((tq, tq))
    trip = trip_ref[...]
    acc, carry = _sb_block(q, k_ref[...], v_ref[...], trid_ref[...], None, valid)
    out, tot = _sb_block(q, kn_ref[...], vn_ref[...], trip, carry, None)
    acc_s[...] = acc + out
    carry_s[...] = carry + tot

    def cond(st):
        t, m = st
        return jnp.logical_and(t < n_old, m < SB_DEAD)

    def body(st):
        t, _ = st
        k0 = pl.multiple_of((n_old - 1 - t) * tk, tk)
        copies = [pltpu.make_async_copy(src.at[b, pl.ds(k0, tk), h, :], dst, sems.at[i])
                  for i, (src, dst) in enumerate(((kc_hbm, kbuf), (vc_hbm, vbuf)))]
        for c in copies:
            c.start()
        for c in copies:
            c.wait()
        out, tot = _sb_block(q, kbuf[...], vbuf[...], trip, carry_s[...], None)
        acc_s[...] += out
        carry_s[...] += tot
        return t + 1, jnp.min(carry_s[...])

    lax.while_loop(cond, body, (jnp.int32(0), jnp.min(carry_s[...])))
    o_ref[...] = acc_s[...].astype(o_ref.dtype)


def _suffix_ones(n):
    return (_row_ids((n, n)) > _col_ids((n, n))).astype(BF16)


SB_BLOCK = 256
SB_QSUB = 4


def _sb_attention(q, k, v, k_past=None, v_past=None):
    B, T, D = q.shape
    H = D // SB_HEAD_DIM
    hd = SB_HEAD_DIM
    full = lambda n: pl.BlockSpec((n, n), lambda b, h, i: (0, 0))
    seq = pl.BlockSpec((None, T, hd), lambda b, h, i: (b, 0, h))
    if k_past is None:
        tb = _tile(T, SB_BLOCK)
        nsub = _tile(T // tb, SB_QSUB)
        tq = nsub * tb
        qspec = pl.BlockSpec((None, tq, hd), lambda b, h, i: (b, i, h))
        return pl.pallas_call(
            functools.partial(_sb_prompt_kernel, nsub=nsub, tb=tb),
            grid=(B, H, T // tq),
            in_specs=[qspec, seq, seq, full(tb)],
            out_specs=qspec,
            out_shape=jax.ShapeDtypeStruct((B, T, D), BF16),
            scratch_shapes=[pltpu.VMEM((nsub, tb, hd), F32), pltpu.VMEM((nsub, tb, 1), F32)],
            compiler_params=_params(("parallel", "parallel", "arbitrary")),
            name="sb_attention",
        )(q, k, v, _suffix_ones(tb))
    P = k_past.shape[1]
    tk = _tile(P, SB_BLOCK)
    near = lambda c: c[:, P - tk:].reshape(B, tk, D).astype(BF16)
    qspec = pl.BlockSpec((None, T, hd), lambda b, h, i: (b, 0, h))
    nspec = pl.BlockSpec((None, tk, hd), lambda b, h, i: (b, 0, h))
    hbm = pl.BlockSpec(memory_space=pl.ANY)
    return pl.pallas_call(
        functools.partial(_sb_decode_kernel, tq=T, tk=tk, n_old=P // tk - 1),
        grid=(B, H, 1),
        in_specs=[qspec, seq, seq, nspec, nspec, hbm, hbm, full(T), full(tk)],
        out_specs=qspec,
        out_shape=jax.ShapeDtypeStruct((B, T, D), BF16),
        scratch_shapes=[pltpu.VMEM((T, hd), F32), pltpu.VMEM((T, 1), F32),
                        pltpu.VMEM((tk, hd), F32), pltpu.VMEM((tk, hd), F32),
                        pltpu.SemaphoreType.DMA((2,))],
        compiler_params=_params(("parallel", "parallel", "arbitrary")),
        name="sb_attention_decode",
    )(q, k, v, near(k_past), near(v_past), k_past, v_past, _suffix_ones(T), _suffix_ones(tk))


def _trunk(x, shift0, wkv0, conv0, k_past, v_past, p):
    B, T, D = x.shape
    M = B * T
    r, k, v, lw, asig, g, shift = _rw_proj(
        x, shift0[0], p['a_norm_g'][0], p['rw_mu'][0], p['rw_w0'][0], p['rw_a0'][0],
        p['rw_wr'], p['rw_wk'], p['rw_wv'], p['rw_w1'], p['rw_w2'], p['rw_a1'], p['rw_a2'],
        p['rw_g1'], p['rw_g2'])
    o, s_bd = _wkv(r, lw, k, v, asig, g, _state_to_pairs(wkv0[0]),
                   p['rw_kk'][0], p['rw_ka'][0], p['rw_rk'][0].reshape(-1),
                   p['rw_lnx_g'][0], p['rw_lnx_b'][0])
    x, conv_a = _ffn(x, o, p['rw_wo'], conv0[0], p['f_norm_g'][0], p['f_wup'],
                     p['f_conv_w'][0], p['f_conv_b'][0], p['f_wdown'], p['out_norm_g'],
                     layer=0, final_norm=False)
    k_sh, v_sh, kb, vb, q = _kvq(x.reshape(M, D), p['kv_norm_g'], p['b_norm_g'][0],
                                 p['w_kv'], p['sb_wq'])
    d_att = q.shape[1]
    att = _sb_attention(q.reshape(B, T, d_att), kb.reshape(B, T, d_att), vb.reshape(B, T, d_att),
                        k_past, v_past)
    y, conv_b = _ffn(x, att, p['sb_wo'], conv0[1], p['f_norm_g'][1], p['f_wup'],
                     p['f_conv_w'][1], p['f_conv_b'][1], p['f_wdown'], p['out_norm_g'],
                     layer=1, final_norm=True)
    H = d_att // SB_HEAD_DIM
    return (y, _pairs_to_state(s_bd)[None], shift.reshape(1, B, D), jnp.stack([conv_a, conv_b]),
            k_sh.reshape(B, T, H, SB_HEAD_DIM), v_sh.reshape(B, T, H, SB_HEAD_DIM))


def kernel(x_prompt, x_sample, cache_k, cache_v, state_wkv, state_shift, state_conv, a_norm_g, rw_mu, rw_w0, rw_w1, rw_w2, rw_a0, rw_a1, rw_a2, rw_g1, rw_g2, rw_kk, rw_ka, rw_rk, rw_wr, rw_wk, rw_wv, rw_wo, rw_lnx_g, rw_lnx_b, kv_norm_g, w_kv, b_norm_g, sb_wq, sb_wo, f_norm_g, f_wup, f_conv_w, f_conv_b, f_wdown, out_norm_g):
    bf = lambda w: w.astype(BF16)
    p = dict(a_norm_g=a_norm_g, rw_mu=rw_mu, rw_w0=rw_w0, rw_a0=rw_a0,
             rw_w1=bf(rw_w1[0]), rw_w2=bf(rw_w2[0]), rw_a1=bf(rw_a1[0]), rw_a2=bf(rw_a2[0]),
             rw_g1=bf(rw_g1[0]), rw_g2=bf(rw_g2[0]),
             rw_kk=rw_kk, rw_ka=rw_ka, rw_rk=rw_rk,
             rw_wr=bf(rw_wr[0]), rw_wk=bf(rw_wk[0]), rw_wv=bf(rw_wv[0]), rw_wo=bf(rw_wo[0]),
             rw_lnx_g=rw_lnx_g, rw_lnx_b=rw_lnx_b, kv_norm_g=kv_norm_g, w_kv=bf(w_kv),
             b_norm_g=b_norm_g, sb_wq=bf(sb_wq[0]), sb_wo=bf(sb_wo[0]), f_norm_g=f_norm_g,
             f_wup=bf(f_wup), f_conv_w=f_conv_w, f_conv_b=f_conv_b, f_wdown=bf(f_wdown),
             out_norm_g=out_norm_g)
    B, _, D = x_prompt.shape
    n_a = state_shift.shape[0]
    depth = state_conv.shape[0]
    F = state_conv.shape[-1]
    H = state_wkv.shape[2]
    shift0 = jnp.zeros((n_a, B, D), x_prompt.dtype)
    wkv0 = jnp.zeros((n_a, B, H, RW_HEAD, RW_HEAD), F32)
    conv0 = jnp.zeros((depth, B, CONV_W - 1, F), x_prompt.dtype)
    y_p, wkv_p, shift_p, conv_p, k_p, v_p = _trunk(x_prompt, shift0, wkv0, conv0, None, None, p)
    y_s, wkv_s, shift_s, conv_s, k_s, v_s = _trunk(
        x_sample, state_shift, state_wkv.astype(F32), state_conv, cache_k, cache_v, p)
    return (y_p, y_s, wkv_p.astype(state_wkv.dtype), shift_p, conv_p, k_p, v_p,
            wkv_s.astype(state_wkv.dtype), shift_s, conv_s, k_s, v_s)
```

```python
import functools

import jax
import jax.numpy as jnp
from jax import lax
from jax.experimental import pallas as pl
from jax.experimental.pallas import tpu as pltpu

F32 = jnp.float32
BF16 = jnp.bfloat16

RW_HEAD = 64
PAIR = 2 * RW_HEAD
SB_HEAD_DIM = 128
GN_EPS = 64e-5
NORM_EPS = 1e-6
CONV_W = 3
BF16_ROWS = 16
WKV_CHUNK = 64
NEUMANN_BLOCK = 16
WKV_CHUNKS_PER_STEP = 2
LOG2_E = 1.4426950408889634
ROW_TILE = 512
VMEM_LIMIT = 56 * 1024 * 1024


def _tile(n, pref):
    if n <= pref:
        return n
    t = pref
    while n % t:
        t //= 2
    return t


def _row_tiling(B, T):
    if T >= ROW_TILE:
        return B, 1, _tile(T, ROW_TILE)
    nseq = _tile(B, max(ROW_TILE // T, 1))
    return B // nseq, nseq, nseq * T


def _params(sem):
    return pltpu.CompilerParams(dimension_semantics=sem, vmem_limit_bytes=VMEM_LIMIT)


def _dot(a, b):
    return jnp.dot(a.astype(BF16), b.astype(BF16), preferred_element_type=F32)


def _dot_nt(a, b):
    return lax.dot_general(a.astype(BF16), b.astype(BF16), (((1,), (1,)), ((), ())),
                           preferred_element_type=F32)


def _dot_tn(a, b):
    return lax.dot_general(a.astype(BF16), b.astype(BF16), (((0,), (0,)), ((), ())),
                           preferred_element_type=F32)


def _rms_scale(x):
    return x * lax.rsqrt(jnp.mean(x * x, axis=-1, keepdims=True) + NORM_EPS)


def _softplus(u):
    return jnp.maximum(u, 0.0) + jnp.log1p(jnp.exp(-jnp.abs(u)))


def _row_ids(shape):
    return lax.broadcasted_iota(jnp.int32, shape, 0)


def _col_ids(shape):
    return lax.broadcasted_iota(jnp.int32, shape, 1)


def _rw_proj_kernel(x_ref, shift0_ref, ng_ref, mu_ref, w0_ref, a0_ref,
                    wr_ref, wk_ref, wv_ref, w1_ref, w2_ref, a1_ref, a2_ref, g1_ref, g2_ref,
                    r_out, k_out, v_out, lw_out, as_out, g_out, shift_out,
                    xr_s, xk_s, xv_s, hw_s, ha_s, hg_s, carry_s, *, nseq, seq_rows):
    t = pl.program_id(1)
    j = pl.program_id(2)

    @pl.when(j == 0)
    def _():
        @pl.when(t == 0)
        def _():
            carry_s[...] = shift0_ref[0]

        xn = _rms_scale(x_ref[...]) * ng_ref[...]
        rows = _row_ids(xn.shape)
        x_prev = jnp.where(rows == 0, carry_s[...], pltpu.roll(xn, 1, axis=0))
        for s in range(1, nseq):
            x_prev = jnp.where(rows == s * seq_rows, shift0_ref[s], x_prev)
        for s in range(nseq):
            shift_out[s] = xn[(s + 1) * seq_rows - 1:(s + 1) * seq_rows, :]
        carry_s[...] = xn[nseq * seq_rows - 1:nseq * seq_rows, :]
        xx = x_prev - xn
        mix = lambda i: (xn + xx * mu_ref[i:i + 1, :]).astype(BF16)
        xr_s[...] = mix(0)
        xk_s[...] = mix(2)
        xv_s[...] = mix(3)
        hw_s[...] = jnp.tanh(_dot(mix(1), w1_ref[...])).astype(BF16)
        ha_s[...] = _dot(mix(4), a1_ref[...]).astype(BF16)
        hg_s[...] = jax.nn.sigmoid(_dot(mix(5), g1_ref[...])).astype(BF16)

    r_out[...] = _dot(xr_s[...], wr_ref[...])
    k_out[...] = _dot(xk_s[...], wk_ref[...])
    v_out[...] = _dot(xv_s[...], wv_ref[...])
    w_log = -_softplus(-(w0_ref[...] + _dot(hw_s[...], w2_ref[...]))) - 0.5
    lw_out[...] = -jnp.exp(w_log)
    as_out[...] = jax.nn.sigmoid(a0_ref[...] + _dot(ha_s[...], a2_ref[...]))
    g_out[...] = _dot(hg_s[...], g2_ref[...])


def _rw_proj(x, shift0, ng, mu, w0, a0, wr, wk, wv, w1, w2, a1, a2, g1, g2):
    B, T, D = x.shape
    G, nseq, tm = _row_tiling(B, T)
    R = B * T // G
    tn = _tile(D, 512)
    lw_dim, la_dim, lg_dim = w1.shape[1], a1.shape[1], g1.shape[1]
    row = lambda b, t, j: (0, 0)
    colv = pl.BlockSpec((1, tn), lambda b, t, j: (0, j))
    act = pl.BlockSpec((None, tm, tn), lambda b, t, j: (b, t, j))
    out_sds = jax.ShapeDtypeStruct((G, R, D), F32)
    outs = pl.pallas_call(
        functools.partial(_rw_proj_kernel, nseq=nseq, seq_rows=tm // nseq),
        grid=(G, R // tm, D // tn),
        in_specs=[
            pl.BlockSpec((None, tm, D), lambda b, t, j: (b, t, 0)),
            pl.BlockSpec((nseq, 1, D), lambda b, t, j: (b, 0, 0)),
            pl.BlockSpec((1, D), row),
            pl.BlockSpec((6, D), row),
            colv, colv,
            pl.BlockSpec((D, tn), lambda b, t, j: (0, j)),
            pl.BlockSpec((D, tn), lambda b, t, j: (0, j)),
            pl.BlockSpec((D, tn), lambda b, t, j: (0, j)),
            pl.BlockSpec((D, lw_dim), row),
            pl.BlockSpec((lw_dim, tn), lambda b, t, j: (0, j)),
            pl.BlockSpec((D, la_dim), row),
            pl.BlockSpec((la_dim, tn), lambda b, t, j: (0, j)),
            pl.BlockSpec((D, lg_dim), row),
            pl.BlockSpec((lg_dim, tn), lambda b, t, j: (0, j)),
        ],
        out_specs=[act] * 6 + [pl.BlockSpec((nseq, 1, D), lambda b, t, j: (b, 0, 0))],
        out_shape=[out_sds] * 6 + [jax.ShapeDtypeStruct((B, 1, D), F32)],
        scratch_shapes=[
            pltpu.VMEM((tm, D), BF16), pltpu.VMEM((tm, D), BF16), pltpu.VMEM((tm, D), BF16),
            pltpu.VMEM((tm, lw_dim), BF16), pltpu.VMEM((tm, la_dim), BF16),
            pltpu.VMEM((tm, lg_dim), BF16), pltpu.VMEM((1, D), F32),
        ],
        compiler_params=_params(("arbitrary", "arbitrary", "arbitrary")),
        name="rw_proj",
    )(x.reshape(G, R, D), shift0.reshape(B, 1, D), ng.reshape(1, D), mu, w0.reshape(1, D),
      a0.reshape(1, D), wr, wk, wv, w1, w2, a1, a2, g1, g2)
    return [o.reshape(B, T, D) for o in outs[:6]] + [outs[6]]


def _wkv_kernel(r_ref, lw_ref, k_ref, v_ref, as_ref, g_ref, s0_ref,
                kkw_ref, kaw_ref, rk_ref, lng_ref, lnb_ref, *rest,
                chunk, n_sub, n_pairs, n_cast):
    cast_in = rest[:n_cast]
    o_ref, s_ref = rest[n_cast:n_cast + 2]
    cast_out = rest[n_cast + 2:2 * n_cast + 2]
    c_s = rest[-1]
    C = chunk
    C2 = 2 * C
    trow = lambda sc: slice(sc * C, (sc + 1) * C)

    for w_in, w_out in zip(cast_in, cast_out):
        w_out[...] = w_in[...].astype(w_out.dtype)

    @pl.when(pl.program_id(1) == 0)
    def _():
        s_ref[...] = s0_ref[...]

    tri = (_col_ids((C, C)) <= _row_ids((C, C))).astype(BF16)
    for sc in range(n_sub):
        lw_all = lw_ref[trow(sc), :]
        lw_hi = lw_all.astype(BF16)
        lw_r = lw_all - lw_hi.astype(F32)
        lw_mid = lw_r.astype(BF16)
        lw_lo = (lw_r - lw_mid.astype(F32)).astype(BF16)
        c_s[trow(sc), :] = (jnp.dot(tri, lw_hi, preferred_element_type=F32)
                            + jnp.dot(tri, lw_mid, preferred_element_type=F32)
                            + jnp.dot(tri, lw_lo, preferred_element_type=F32))

    head0 = _col_ids((C, PAIR)) < RW_HEAD
    ones_bd = ((_row_ids((PAIR, PAIR)) // RW_HEAD) == (_col_ids((PAIR, PAIR)) // RW_HEAD))
    ones_bd_bf = ones_bd.astype(BF16)

    rr = _row_ids((C2, C2))
    cc = _col_ids((C2, C2))
    same_head = (rr // C) == (cc // C)
    nb = NEUMANN_BLOCK
    same16 = (rr // nb) == (cc // nb)
    same32 = (rr // (2 * nb)) == (cc // (2 * nb))
    m_e1 = same32 & (~same16) if C >= 2 * nb else None
    m_e2 = same_head & (~same32) if C >= 4 * nb else None
    eye22 = (rr == cc).astype(F32)
    r12 = _row_ids((C, C2))
    c12 = _col_ids((C, C2))
    strict12 = (c12 % C) < r12
    incl12 = (c12 % C) <= r12
    left12 = c12 < C
    same16_c = ((c12 % C) // nb) == (r12 // nb)
    c16 = _col_ids((nb, C2))
    blk_of_lane = (c16 % C) // nb
    eye16 = ((c16 % nb) == _row_ids((nb, C2))).astype(F32)

    def expand(x):
        return jnp.where(same_head, jnp.concatenate([x, x], axis=0), 0.0)

    def expand16(x):
        return jnp.where(same16, jnp.concatenate([x] * (C2 // nb), axis=0), 0.0)

    def group_sum(x):
        return jnp.dot(x.astype(BF16), ones_bd_bf, preferred_element_type=F32)

    cat0 = lambda xs: jnp.concatenate(xs, axis=0)
    zero_h1 = lambda x: jnp.where(head0, x, 0.0)
    zero_h0 = lambda x: jnp.where(head0, 0.0, x)
    inv_n = 1.0 / RW_HEAD

    def state_free(items):
        n = range(len(items))
        tr = [trow(sc) for sc, _ in items]
        sl = [slice(p * PAIR, (p + 1) * PAIR) for _, p in items]
        r = [r_ref[tr[i], sl[i]] for i in n]
        lw = [lw_ref[tr[i], sl[i]] for i in n]
        k = [k_ref[tr[i], sl[i]] for i in n]
        v = [v_ref[tr[i], sl[i]] for i in n]
        asig = [as_ref[tr[i], sl[i]] for i in n]
        c = [c_s[tr[i], sl[i]] for i in n]
        c_end = [x[C - 1:C, :] for x in c]

        kkv = [k[i] * kkw_ref[:, sl[i]] for i in n]
        ss = [group_sum(x * x) for x in kkv]
        kk = [kkv[i] / jnp.maximum(jnp.sqrt(ss[i]), 1e-12) for i in n]
        b_in = [kk[i] * asig[i] for i in n]
        k_in = [k[i] * (1.0 + (asig[i] - 1.0) * kaw_ref[:, sl[i]]) for i in n]
        bonus_s = [group_sum(r[i] * k_in[i] * rk_ref[:, sl[i]]) for i in n]

        e_neg = [jnp.exp(-x) for x in c]
        a_t = [(-kk[i]) * jnp.exp(c[i] - lw[i]) for i in n]
        r_t = [r[i] * jnp.exp(c[i]) for i in n]
        b_h = [b_in[i] * e_neg[i] for i in n]
        k_h = [k_in[i] * e_neg[i] for i in n]
        e_end = [jnp.exp(c_end[i] - c[i]) for i in n]
        RE = [cat0([b_in[i] * e_end[i], k_in[i] * e_end[i]]) for i in n]

        L = [cat0([a_t[i], r_t[i]]) for i in n]
        G = [_dot_nt(L[i], cat0([zero_h1(b_h[i]), zero_h1(k_h[i]),
                                 zero_h0(k_h[i]), zero_h0(b_h[i])])) for i in n]
        G0 = [x[:, :C2] for x in G]
        G1 = [x[:, C2:] for x in G]
        g0t = [x[:C] for x in G0]
        g1t = [x[:C] for x in G1]

        n_c = [jnp.where(strict12, jnp.where(left12, g0t[i], g1t[i]), 0.0) for i in n]
        n_bd = [expand(x) for x in n_c]
        d1 = [sum(jnp.where(blk_of_lane == b, x[b * nb:(b + 1) * nb], 0.0)
                  for b in range(C // nb)) for x in n_c]
        d1_bd = [expand16(x) for x in d1]
        d2 = [_dot(d1[i], d1_bd[i]) for i in n]
        d2_bd = [expand16(x) for x in d2]
        d4 = [_dot(d2[i], d2_bd[i]) for i in n]
        d4_bd = [expand16(x) for x in d4]
        d8 = [_dot(d4[i], d4_bd[i]) for i in n]
        pa = [_dot(eye16 + d1[i], eye22 + d2_bd[i]) for i in n]
        pb = [_dot(eye16 + d4[i], eye22 + expand16(d8[i])) for i in n]
        t16 = [_dot(pa[i], expand16(pb[i])) for i in n]
        tcat = [jnp.where(same16_c, cat0([x] * (C // nb)), 0.0) for x in t16]
        for m_e in (m_e1, m_e2):
            if m_e is not None:
                t_bd = [expand(x) for x in tcat]
                x = [_dot(tcat[i], jnp.where(m_e, n_bd[i], 0.0)) for i in n]
                tcat = [tcat[i] + _dot(x[i], t_bd[i]) for i in n]

        v0 = [jnp.where(head0, x, 0.0) for x in v]
        v1 = [jnp.where(head0, 0.0, x) for x in v]
        av = [_dot(jnp.where(strict12, jnp.where(left12, g1t[i], g0t[i]), 0.0),
                   cat0([v1[i], v0[i]])) for i in n]
        bot = [jnp.concatenate([jnp.where(incl12, G0[i][C:], 0.0),
                                jnp.where(incl12, G1[i][C:], 0.0)], axis=1) for i in n]
        return dict(L=L, tcat=tcat, av=av, bot=bot, RE=RE, v=v, v0=v0, v1=v1,
                    decay=[jnp.exp(x) for x in c_end], bonus=bonus_s)

    def state_chain(sc, pre, sel):
        n = range(len(sel))
        at = lambda name: [pre[name][j] for j in sel]
        L, tcat, av, bot, RE, v, v0, v1, decay, bonus_s = (
            at(x) for x in ("L", "tcat", "av", "bot", "RE", "v", "v0", "v1", "decay", "bonus"))
        sl = [slice(p * PAIR, (p + 1) * PAIR) for p in range(n_pairs)]
        S = [s_ref[p] for p in range(n_pairs)]
        P = [_dot_nt(L[i], S[i]) for i in n]
        W = [P[i][:C] + av[i] for i in n]
        U = [_dot(tcat[i], cat0([zero_h1(W[i]), zero_h0(W[i])])) for i in n]
        Y = [P[i][C:] + _dot(bot[i], cat0([zero_h1(U[i]), v0[i], v1[i], zero_h0(U[i])]))
             for i in n]
        for i in n:
            upd = _dot_tn(cat0([U[i], v[i]]), RE[i])
            s_ref[i] = S[i] * decay[i] + jnp.where(ones_bd, upd, 0.0)

        mean = [group_sum(x) * inv_n for x in Y]
        dlt = [Y[i] - mean[i] for i in n]
        var = [group_sum(x * x) * inv_n for x in dlt]
        for i in n:
            yn = dlt[i] * lax.rsqrt(var[i] + GN_EPS) * lng_ref[:, sl[i]] + lnb_ref[:, sl[i]]
            out = (yn + bonus_s[i] * v[i]) * g_ref[trow(sc), sl[i]]
            o_ref[trow(sc), sl[i]] = out.astype(o_ref.dtype)

    items = [(sc, p) for sc in range(n_sub) for p in range(n_pairs)]
    pre = state_free(items)
    for sc in range(n_sub):
        state_chain(sc, pre, [sc * n_pairs + p for p in range(n_pairs)])


def _cast_slab(rows, n_steps):
    share = 1
    while rows * share % (n_steps * BF16_ROWS):
        share *= 2
    return rows * share // n_steps, share


def _wkv(r, lw, k, v, asig, g, s0_bd, kkw, kaw, rk, lng, lnb, cast=()):
    B, T, D = r.shape
    C = _tile(T, WKV_CHUNK)
    n_sub = _tile(T // C, WKV_CHUNKS_PER_STEP)
    n_pairs = D // PAIR
    steps_t = T // (n_sub * C)
    act = pl.BlockSpec((None, n_sub * C, D), lambda b, t: (b, t, 0))
    vec = pl.BlockSpec((1, D), lambda b, t: (0, 0))
    st = pl.BlockSpec((None, n_pairs, PAIR, PAIR), lambda b, t: (b, 0, 0, 0))
    cast_specs = []
    for w in cast:
        slab, share = _cast_slab(w.shape[0], B * steps_t)
        cast_specs.append(pl.BlockSpec(
            (slab, w.shape[1]), lambda b, t, share=share: ((b * steps_t + t) // share, 0)))
    outs = pl.pallas_call(
        functools.partial(_wkv_kernel, chunk=C, n_sub=n_sub, n_pairs=n_pairs,
                          n_cast=len(cast)),
        grid=(B, steps_t),
        in_specs=[act] * 6 + [st] + [vec] * 5 + cast_specs,
        out_specs=[act, st] + cast_specs,
        out_shape=[jax.ShapeDtypeStruct((B, T, D), BF16),
                   jax.ShapeDtypeStruct((B, n_pairs, PAIR, PAIR), F32)]
        + [jax.ShapeDtypeStruct(w.shape, BF16) for w in cast],
        scratch_shapes=[pltpu.VMEM((n_sub * C, D), F32)],
        compiler_params=_params(("arbitrary", "arbitrary")),
        name="wkv",
    )(r, lw, k, v, asig, g, s0_bd, kkw.reshape(1, D), kaw.reshape(1, D), rk.reshape(1, D),
      lng.reshape(1, D), lnb.reshape(1, D), *cast)
    return outs[0], outs[1], outs[2:]


def _state_to_pairs(s):
    B, H, n, _ = s.shape
    s = s.reshape(B, H // 2, 2, n, n)
    z = jnp.zeros((B, H // 2, n, n), s.dtype)
    top = jnp.concatenate([s[:, :, 0], z], axis=-1)
    bot = jnp.concatenate([z, s[:, :, 1]], axis=-1)
    return jnp.concatenate([top, bot], axis=-2)


def _pairs_to_state(sp):
    B, P, _, _ = sp.shape
    n = RW_HEAD
    h0 = sp[:, :, :n, :n]
    h1 = sp[:, :, n:, n:]
    return jnp.stack([h0, h1], axis=2).reshape(B, 2 * P, n, n)


def _ffn_kernel(x_ref, a_ref, wo_ref, conv0_ref, ng_ref, wg_ref, wv_ref, cw_ref, cb_ref,
                wd_ref, og_ref, o_ref, conv_out, xn_s, acc_s, carry_s,
                *, final_norm, nseq, seq_rows):
    t = pl.program_id(1)
    f = pl.program_id(2)
    nf = pl.num_programs(2)
    keep = CONV_W - 1

    @pl.when(f == 0)
    def _():
        x1 = x_ref[...] + jnp.dot(a_ref[...], wo_ref[...], preferred_element_type=F32)
        acc_s[...] = x1
        xn_s[...] = (_rms_scale(x1) * ng_ref[...]).astype(BF16)

    @pl.when(t == 0)
    def _():
        carry_s[f] = conv0_ref[0]

    xn = xn_s[...]
    gate = jnp.dot(xn, wg_ref[...], preferred_element_type=F32)
    val = jnp.dot(xn, wv_ref[...], preferred_element_type=F32)
    prev = carry_s[f]
    rows = _row_ids(gate.shape)
    g1 = jnp.where(rows == 0, prev[1:2, :], pltpu.roll(gate, 1, axis=0))
    g2 = jnp.where(rows == 0, prev[0:1, :],
                   jnp.where(rows == 1, prev[1:2, :], pltpu.roll(gate, 2, axis=0)))
    for s in range(1, nseq):
        first = s * seq_rows
        prev_s = conv0_ref[s]
        g1 = jnp.where(rows == first, prev_s[1:2, :], g1)
        g2 = jnp.where(rows == first, prev_s[0:1, :],
                       jnp.where(rows == first + 1, prev_s[1:2, :], g2))
    c = cb_ref[...] + cw_ref[0:1, :] * g2
    c = c + cw_ref[1:2, :] * g1
    c = c + cw_ref[2:3, :] * gate
    y = (c * jax.nn.sigmoid(c)) * val
    acc_s[...] += jnp.dot(y.astype(BF16), wd_ref[...], preferred_element_type=F32)
    tf = gate.shape[1]
    cols = pl.ds(pl.multiple_of(f * tf, tf), tf)
    for s in range(nseq):
        conv_out[s, :, cols] = gate[(s + 1) * seq_rows - keep:(s + 1) * seq_rows, :]
    carry_s[f] = gate[nseq * seq_rows - keep:nseq * seq_rows, :]

    @pl.when(f == nf - 1)
    def _():
        out = acc_s[...]
        if final_norm:
            out = _rms_scale(out) * og_ref[...]
        o_ref[...] = out


def _ffn(x, a, wo, conv0, ng, wup, cw, cb, wdown, og, *, layer, final_norm):
    B, T, D = x.shape
    F = wdown.shape[1]
    Ka = a.shape[-1]
    G, nseq, tm = _row_tiling(B, T)
    R = B * T // G
    tf = _tile(F, 512)
    nf = F // tf
    keep = CONV_W - 1
    row = lambda b, t, f: (0, 0)
    out, conv = pl.pallas_call(
        functools.partial(_ffn_kernel, final_norm=final_norm, nseq=nseq, seq_rows=tm // nseq),
        grid=(G, R // tm, nf),
        in_specs=[
            pl.BlockSpec((None, tm, D), lambda b, t, f: (b, t, 0)),
            pl.BlockSpec((None, tm, Ka), lambda b, t, f: (b, t, 0)),
            pl.BlockSpec((Ka, D), row, pipeline_mode=pl.Buffered(1)),
            pl.BlockSpec((nseq, keep, tf), lambda b, t, f: (b, 0, f)),
            pl.BlockSpec((1, D), row),
            pl.BlockSpec((None, D, tf), lambda b, t, f: (layer, 0, f)),
            pl.BlockSpec((None, D, tf), lambda b, t, f: (layer, 0, f + nf)),
            pl.BlockSpec((CONV_W, tf), lambda b, t, f: (0, f)),
            pl.BlockSpec((1, tf), lambda b, t, f: (0, f)),
            pl.BlockSpec((None, tf, D), lambda b, t, f: (layer, f, 0)),
            pl.BlockSpec((1, D), row),
        ],
        out_specs=[pl.BlockSpec((None, tm, D), lambda b, t, f: (b, t, 0)),
                   pl.BlockSpec((nseq, keep, F), lambda b, t, f: (b, 0, 0))],
        out_shape=[jax.ShapeDtypeStruct((G, R, D), F32),
                   jax.ShapeDtypeStruct((B, keep, F), F32)],
        scratch_shapes=[pltpu.VMEM((tm, D), BF16), pltpu.VMEM((tm, D), F32),
                        pltpu.VMEM((nf, keep, tf), F32)],
        compiler_params=_params(("arbitrary", "arbitrary", "arbitrary")),
        name="conv_ffn",
    )(x.reshape(G, R, D), a.reshape(G, R, Ka), wo, conv0, ng.reshape(1, D), wup, wup, cw,
      cb.reshape(1, F), wdown, og.reshape(1, D))
    return out.reshape(B, T, D), conv


def _kvq_kernel(x_ref, gkv_ref, gq_ref, wk_ref, wv_ref, wq_ref,
                k_out, v_out, kb_out, vb_out, q_out, xkv_s, xq_s, *, q_scale):
    @pl.when(pl.program_id(1) == 0)
    def _():
        xh = _rms_scale(x_ref[...])
        xkv_s[...] = (xh * gkv_ref[...]).astype(BF16)
        xq_s[...] = (xh * gq_ref[...]).astype(BF16)

    xkv = xkv_s[...]
    k = jnp.dot(xkv, wk_ref[...], preferred_element_type=F32)
    v = jnp.dot(xkv, wv_ref[...], preferred_element_type=F32)
    k_out[...] = k
    v_out[...] = v
    kb_out[...] = k.astype(BF16)
    vb_out[...] = v.astype(BF16)
    q = jnp.dot(xq_s[...], wq_ref[...], preferred_element_type=F32)
    q_out[...] = (q * q_scale).astype(BF16)


def _kvq(x, gkv, gq, wkv, wq):
    M, D = x.shape
    N = wq.shape[1]
    tm = _tile(M, ROW_TILE)
    tn = _tile(N, 512)
    nn = N // tn
    row = lambda i, j: (0, 0)
    blk = pl.BlockSpec((tm, tn), lambda i, j: (i, j))
    return pl.pallas_call(
        functools.partial(_kvq_kernel, q_scale=float(SB_HEAD_DIM) ** -0.5 * LOG2_E),
        grid=(M // tm, nn),
        in_specs=[pl.BlockSpec((tm, D), lambda i, j: (i, 0)),
                  pl.BlockSpec((1, D), row), pl.BlockSpec((1, D), row),
                  pl.BlockSpec((D, tn), lambda i, j: (0, j)),
                  pl.BlockSpec((D, tn), lambda i, j: (0, j + nn)),
                  pl.BlockSpec((D, tn), lambda i, j: (0, j))],
        out_specs=[blk] * 5,
        out_shape=[jax.ShapeDtypeStruct((M, N), F32), jax.ShapeDtypeStruct((M, N), F32),
                   jax.ShapeDtypeStruct((M, N), BF16), jax.ShapeDtypeStruct((M, N), BF16),
                   jax.ShapeDtypeStruct((M, N), BF16)],
        scratch_shapes=[pltpu.VMEM((tm, D), BF16), pltpu.VMEM((tm, D), BF16)],
        compiler_params=_params(("parallel", "arbitrary")),
        name="kvq_proj",
    )(x, gkv.reshape(1, D), gq.reshape(1, D), wkv, wkv, wq)


SB_DEAD = 152.0


def _sb_block(q, kb, vb, tri, carry, valid, on=None):
    z = _dot_nt(q, kb)
    sp = jnp.maximum(z, 0.0) + jnp.log2(1.0 + jnp.exp2(jnp.minimum(z, -z)))
    if valid is not None:
        sp = jnp.where(valid, sp, 0.0)
    if on is not None:
        sp = jnp.where(on, sp, 0.0)
    later = jnp.dot(sp.astype(BF16), tri, preferred_element_type=F32)
    logw = z - sp - later
    if carry is not None:
        logw = logw - carry
    w = jnp.exp2(logw)
    if valid is not None:
        w = jnp.where(valid, w, 0.0)
    if on is not None:
        w = jnp.where(on, w, 0.0)
    out = jnp.dot(w.astype(BF16), vb.astype(BF16), preferred_element_type=F32)
    return out, later[:, 0:1] + sp[:, 0:1]


def _sb_prompt_kernel(q_ref, k_ref, v_ref, tri_ref, o_ref, acc_s, carry_s, *, nsub, tb):
    qi = pl.program_id(2)
    blk0 = qi * nsub
    tri = tri_ref[...]
    valid = _col_ids((tb, tb)) < _row_ids((tb, tb))
    rows = lambda i: slice(i * tb, (i + 1) * tb)

    def kv_block(j):
        k0 = pl.multiple_of(j * tb, tb)
        return k_ref[pl.ds(k0, tb), :], v_ref[pl.ds(k0, tb), :]

    kvs = [kv_block(blk0 + i) for i in range(nsub)]
    acc, car = [], []
    for i in range(nsub):
        out, tot = _sb_block(q_ref[rows(i), :], kvs[i][0], kvs[i][1], tri, None, valid)
        acc.append(out)
        car.append(tot)
    for i in range(nsub):
        if i == 0:
            kb, vb = kv_block(jnp.maximum(blk0 - 1, 0))
            out, tot = _sb_block(q_ref[rows(0), :], kb, vb, tri, car[0], None, on=blk0 > 0)
        else:
            out, tot = _sb_block(q_ref[rows(i), :], kvs[i - 1][0], kvs[i - 1][1], tri,
                                 car[i], None)
        acc_s[i] = acc[i] + out
        carry_s[i] = car[i] + tot

    def live_min(t):
        m = jnp.float32(jnp.inf)
        for i in range(nsub):
            m = jnp.where(blk0 + i - 2 - t >= 0, jnp.minimum(m, jnp.min(carry_s[i])), m)
        return m

    def cond(st):
        t, m = st
        return jnp.logical_and(blk0 + nsub - 3 - t >= 0, m < SB_DEAD)

    def body(st):
        t, _ = st
        for i in range(nsub):
            j = blk0 + i - 2 - t
            kb, vb = kv_block(jnp.maximum(j, 0))
            out, tot = _sb_block(q_ref[rows(i), :], kb, vb, tri, carry_s[i], None, on=j >= 0)
            acc_s[i] += out
            carry_s[i] += tot
        return t + 1, live_min(t + 1)

    lax.while_loop(cond, body, (jnp.int32(0), live_min(0)))
    for i in range(nsub):
        o_ref[rows(i), :] = acc_s[i].astype(o_ref.dtype)


def _sb_decode_kernel(q_ref, k_ref, v_ref, kn_ref, vn_ref, kc_hbm, vc_hbm, trid_ref, trip_ref,
                      o_ref, acc_s, carry_s, kbuf, vbuf, sems, *, tq, tk, n_old):
    b = pl.program_id(0)
    h = pl.program_id(1)
    q = q_ref[...]
    valid = _col_ids((tq, tq)) < _row_ids((tq, tq))
    trip = trip_ref[...]
    acc, carry = _sb_block(q, k_ref[...], v_ref[...], trid_ref[...], None, valid)
    out, tot = _sb_block(q, kn_ref[...], vn_ref[...], trip, carry, None)
    acc_s[...] = acc + out
    carry_s[...] = carry + tot

    def cond(st):
        t, m = st
        return jnp.logical_and(t < n_old, m < SB_DEAD)

    def body(st):
        t, _ = st
        k0 = pl.multiple_of((n_old - 1 - t) * tk, tk)
        copies = [pltpu.make_async_copy(src.at[b, pl.ds(k0, tk), h, :], dst, sems.at[i])
                  for i, (src, dst) in enumerate(((kc_hbm, kbuf), (vc_hbm, vbuf)))]
        for c in copies:
            c.start()
        for c in copies:
            c.wait()
        out, tot = _sb_block(q, kbuf[...], vbuf[...], trip, carry_s[...], None)
        acc_s[...] += out
        carry_s[...] += tot
        return t + 1, jnp.min(carry_s[...])

    lax.while_loop(cond, body, (jnp.int32(0), jnp.min(carry_s[...])))
    o_ref[...] = acc_s[...].astype(o_ref.dtype)


def _suffix_ones(n):
    return (_row_ids((n, n)) > _col_ids((n, n))).astype(BF16)


SB_BLOCK = 256
SB_QSUB = 4


def _sb_attention(q, k, v, k_past=None, v_past=None):
    B, T, D = q.shape
    H = D // SB_HEAD_DIM
    hd = SB_HEAD_DIM
    full = lambda n: pl.BlockSpec((n, n), lambda b, h, i: (0, 0))
    seq = pl.BlockSpec((None, T, hd), lambda b, h, i: (b, 0, h))
    if k_past is None:
        tb = _tile(T, SB_BLOCK)
        nsub = _tile(T // tb, SB_QSUB)
        tq = nsub * tb
        qspec = pl.BlockSpec((None, tq, hd), lambda b, h, i: (b, i, h))
        return pl.pallas_call(
            functools.partial(_sb_prompt_kernel, nsub=nsub, tb=tb),
            grid=(B, H, T // tq),
            in_specs=[qspec, seq, seq, full(tb)],
            out_specs=qspec,
            out_shape=jax.ShapeDtypeStruct((B, T, D), BF16),
            scratch_shapes=[pltpu.VMEM((nsub, tb, hd), F32), pltpu.VMEM((nsub, tb, 1), F32)],
            compiler_params=_params(("parallel", "parallel", "arbitrary")),
            name="sb_attention",
        )(q, k, v, _suffix_ones(tb))
    P = k_past.shape[1]
    tk = _tile(P, SB_BLOCK)
    near = lambda c: c[:, P - tk:].reshape(B, tk, D).astype(BF16)
    qspec = pl.BlockSpec((None, T, hd), lambda b, h, i: (b, 0, h))
    nspec = pl.BlockSpec((None, tk, hd), lambda b, h, i: (b, 0, h))
    hbm = pl.BlockSpec(memory_space=pl.ANY)
    return pl.pallas_call(
        functools.partial(_sb_decode_kernel, tq=T, tk=tk, n_old=P // tk - 1),
        grid=(B, H, 1),
        in_specs=[qspec, seq, seq, nspec, nspec, hbm, hbm, full(T), full(tk)],
        out_specs=qspec,
        out_shape=jax.ShapeDtypeStruct((B, T, D), BF16),
        scratch_shapes=[pltpu.VMEM((T, hd), F32), pltpu.VMEM((T, 1), F32),
                        pltpu.VMEM((tk, hd), F32), pltpu.VMEM((tk, hd), F32),
                        pltpu.SemaphoreType.DMA((2,))],
        compiler_params=_params(("parallel", "parallel", "arbitrary")),
        name="sb_attention_decode",
    )(q, k, v, near(k_past), near(v_past), k_past, v_past, _suffix_ones(T), _suffix_ones(tk))


LATE_WEIGHTS = ('rw_wo', 'f_wup', 'f_wdown', 'w_kv', 'sb_wq', 'sb_wo')


def _trunk(x, shift0, wkv0, conv0, k_past, v_past, p, late_f32=None):
    B, T, D = x.shape
    M = B * T
    r, k, v, lw, asig, g, shift = _rw_proj(
        x, shift0[0], p['a_norm_g'][0], p['rw_mu'][0], p['rw_w0'][0], p['rw_a0'][0],
        p['rw_wr'], p['rw_wk'], p['rw_wv'], p['rw_w1'], p['rw_w2'], p['rw_a1'], p['rw_a2'],
        p['rw_g1'], p['rw_g2'])
    cast = () if late_f32 is None else tuple(
        late_f32[n].reshape(-1, late_f32[n].shape[-1]) for n in LATE_WEIGHTS)
    o, s_bd, casted = _wkv(r, lw, k, v, asig, g, _state_to_pairs(wkv0[0]),
                           p['rw_kk'][0], p['rw_ka'][0], p['rw_rk'][0].reshape(-1),
                           p['rw_lnx_g'][0], p['rw_lnx_b'][0], cast=cast)
    if late_f32 is not None:
        for n, w in zip(LATE_WEIGHTS, casted):
            p[n] = w.reshape(late_f32[n].shape)
    x, conv_a = _ffn(x, o, p['rw_wo'], conv0[0], p['f_norm_g'][0], p['f_wup'],
                     p['f_conv_w'][0], p['f_conv_b'][0], p['f_wdown'], p['out_norm_g'],
                     layer=0, final_norm=False)
    k_sh, v_sh, kb, vb, q = _kvq(x.reshape(M, D), p['kv_norm_g'], p['b_norm_g'][0],
                                 p['w_kv'], p['sb_wq'])
    d_att = q.shape[1]
    att = _sb_attention(q.reshape(B, T, d_att), kb.reshape(B, T, d_att), vb.reshape(B, T, d_att),
                        k_past, v_past)
    y, conv_b = _ffn(x, att, p['sb_wo'], conv0[1], p['f_norm_g'][1], p['f_wup'],
                     p['f_conv_w'][1], p['f_conv_b'][1], p['f_wdown'], p['out_norm_g'],
                     layer=1, final_norm=True)
    H = d_att // SB_HEAD_DIM
    return (y, _pairs_to_state(s_bd)[None], shift.reshape(1, B, D), jnp.stack([conv_a, conv_b]),
            k_sh.reshape(B, T, H, SB_HEAD_DIM), v_sh.reshape(B, T, H, SB_HEAD_DIM))


def kernel(x_prompt, x_sample, cache_k, cache_v, state_wkv, state_shift, state_conv, a_norm_g, rw_mu, rw_w0, rw_w1, rw_w2, rw_a0, rw_a1, rw_a2, rw_g1, rw_g2, rw_kk, rw_ka, rw_rk, rw_wr, rw_wk, rw_wv, rw_wo, rw_lnx_g, rw_lnx_b, kv_norm_g, w_kv, b_norm_g, sb_wq, sb_wo, f_norm_g, f_wup, f_conv_w, f_conv_b, f_wdown, out_norm_g):
    bf = lambda w: w.astype(BF16)
    p = dict(a_norm_g=a_norm_g, rw_mu=rw_mu, rw_w0=rw_w0, rw_a0=rw_a0,
             rw_w1=bf(rw_w1[0]), rw_w2=bf(rw_w2[0]), rw_a1=bf(rw_a1[0]), rw_a2=bf(rw_a2[0]),
             rw_g1=bf(rw_g1[0]), rw_g2=bf(rw_g2[0]),
             rw_kk=rw_kk, rw_ka=rw_ka, rw_rk=rw_rk,
             rw_wr=bf(rw_wr[0]), rw_wk=bf(rw_wk[0]), rw_wv=bf(rw_wv[0]),
             rw_lnx_g=rw_lnx_g, rw_lnx_b=rw_lnx_b, kv_norm_g=kv_norm_g,
             b_norm_g=b_norm_g, f_norm_g=f_norm_g,
             f_conv_w=f_conv_w, f_conv_b=f_conv_b, out_norm_g=out_norm_g)
    late_f32 = dict(rw_wo=rw_wo[0], f_wup=f_wup, f_wdown=f_wdown, w_kv=w_kv,
                    sb_wq=sb_wq[0], sb_wo=sb_wo[0])
    B, _, D = x_prompt.shape
    n_a = state_shift.shape[0]
    depth = state_conv.shape[0]
    F = state_conv.shape[-1]
    H = state_wkv.shape[2]
    shift0 = jnp.zeros((n_a, B, D), x_prompt.dtype)
    wkv0 = jnp.zeros((n_a, B, H, RW_HEAD, RW_HEAD), F32)
    conv0 = jnp.zeros((depth, B, CONV_W - 1, F), x_prompt.dtype)
    y_p, wkv_p, shift_p, conv_p, k_p, v_p = _trunk(x_prompt, shift0, wkv0, conv0, None, None, p,
                                                   late_f32=late_f32)
    y_s, wkv_s, shift_s, conv_s, k_s, v_s = _trunk(
        x_sample, state_shift, state_wkv.astype(F32), state_conv, cache_k, cache_v, p)
    return (y_p, y_s, wkv_p.astype(state_wkv.dtype), shift_p, conv_p, k_p, v_p,
            wkv_s.astype(state_wkv.dtype), shift_s, conv_s, k_s, v_s)
```

```python
import functools

import jax
import jax.numpy as jnp
from jax import lax
from jax.experimental import pallas as pl
from jax.experimental.pallas import tpu as pltpu

F32 = jnp.float32
BF16 = jnp.bfloat16

RW_HEAD = 64
PAIR = 2 * RW_HEAD
SB_HEAD_DIM = 128
GN_EPS = 64e-5
NORM_EPS = 1e-6
CONV_W = 3
BF16_ROWS = 16
WKV_CHUNK = 64
NEUMANN_BLOCK = 16
WKV_CHUNKS_PER_STEP = 2
LOG2_E = 1.4426950408889634
ROW_TILE = 512
VMEM_LIMIT = 56 * 1024 * 1024


def _tile(n, pref):
    if n <= pref:
        return n
    t = pref
    while n % t:
        t //= 2
    return t


def _row_tiling(B, T):
    if T >= ROW_TILE:
        return B, 1, _tile(T, ROW_TILE)
    nseq = _tile(B, max(ROW_TILE // T, 1))
    return B // nseq, nseq, nseq * T


def _params(sem):
    return pltpu.CompilerParams(dimension_semantics=sem, vmem_limit_bytes=VMEM_LIMIT)


def _dot(a, b):
    return jnp.dot(a.astype(BF16), b.astype(BF16), preferred_element_type=F32)


def _dot_nt(a, b):
    return lax.dot_general(a.astype(BF16), b.astype(BF16), (((1,), (1,)), ((), ())),
                           preferred_element_type=F32)


def _dot_tn(a, b):
    return lax.dot_general(a.astype(BF16), b.astype(BF16), (((0,), (0,)), ((), ())),
                           preferred_element_type=F32)


def _rms_scale(x):
    return x * lax.rsqrt(jnp.mean(x * x, axis=-1, keepdims=True) + NORM_EPS)


def _softplus(u):
    return jnp.maximum(u, 0.0) + jnp.log1p(jnp.exp(-jnp.abs(u)))


def _row_ids(shape):
    return lax.broadcasted_iota(jnp.int32, shape, 0)


def _col_ids(shape):
    return lax.broadcasted_iota(jnp.int32, shape, 1)


def _rw_proj_kernel(x_ref, shift0_ref, ng_ref, mu_ref, w0_ref, a0_ref,
                    wr_ref, wk_ref, wv_ref, w1_ref, w2_ref, a1_ref, a2_ref, g1_ref, g2_ref,
                    r_out, k_out, v_out, lw_out, as_out, g_out, shift_out,
                    xr_s, xk_s, xv_s, hw_s, ha_s, hg_s, carry_s, *, nseq, seq_rows):
    t = pl.program_id(1)
    j = pl.program_id(2)

    @pl.when(j == 0)
    def _():
        @pl.when(t == 0)
        def _():
            carry_s[...] = shift0_ref[0]

        xn = _rms_scale(x_ref[...]) * ng_ref[...]
        rows = _row_ids(xn.shape)
        x_prev = jnp.where(rows == 0, carry_s[...], pltpu.roll(xn, 1, axis=0))
        for s in range(1, nseq):
            x_prev = jnp.where(rows == s * seq_rows, shift0_ref[s], x_prev)
        for s in range(nseq):
            shift_out[s] = xn[(s + 1) * seq_rows - 1:(s + 1) * seq_rows, :]
        carry_s[...] = xn[nseq * seq_rows - 1:nseq * seq_rows, :]
        xx = x_prev - xn
        mix = lambda i: (xn + xx * mu_ref[i:i + 1, :]).astype(BF16)
        xr_s[...] = mix(0)
        xk_s[...] = mix(2)
        xv_s[...] = mix(3)
        hw_s[...] = jnp.tanh(_dot(mix(1), w1_ref[...])).astype(BF16)
        ha_s[...] = _dot(mix(4), a1_ref[...]).astype(BF16)
        hg_s[...] = jax.nn.sigmoid(_dot(mix(5), g1_ref[...])).astype(BF16)

    r_out[...] = _dot(xr_s[...], wr_ref[...])
    k_out[...] = _dot(xk_s[...], wk_ref[...])
    v_out[...] = _dot(xv_s[...], wv_ref[...])
    w_log = -_softplus(-(w0_ref[...] + _dot(hw_s[...], w2_ref[...]))) - 0.5
    lw_out[...] = -jnp.exp(w_log)
    as_out[...] = jax.nn.sigmoid(a0_ref[...] + _dot(ha_s[...], a2_ref[...]))
    g_out[...] = _dot(hg_s[...], g2_ref[...])


def _rw_proj(x, shift0, ng, mu, w0, a0, wr, wk, wv, w1, w2, a1, a2, g1, g2):
    B, T, D = x.shape
    G, nseq, tm = _row_tiling(B, T)
    R = B * T // G
    tn = _tile(D, 512)
    lw_dim, la_dim, lg_dim = w1.shape[1], a1.shape[1], g1.shape[1]
    row = lambda b, t, j: (0, 0)
    colv = pl.BlockSpec((1, tn), lambda b, t, j: (0, j))
    act = pl.BlockSpec((None, tm, tn), lambda b, t, j: (b, t, j))
    out_sds = jax.ShapeDtypeStruct((G, R, D), F32)
    outs = pl.pallas_call(
        functools.partial(_rw_proj_kernel, nseq=nseq, seq_rows=tm // nseq),
        grid=(G, R // tm, D // tn),
        in_specs=[
            pl.BlockSpec((None, tm, D), lambda b, t, j: (b, t, 0)),
            pl.BlockSpec((nseq, 1, D), lambda b, t, j: (b, 0, 0)),
            pl.BlockSpec((1, D), row),
            pl.BlockSpec((6, D), row),
            colv, colv,
            pl.BlockSpec((D, tn), lambda b, t, j: (0, j)),
            pl.BlockSpec((D, tn), lambda b, t, j: (0, j)),
            pl.BlockSpec((D, tn), lambda b, t, j: (0, j)),
            pl.BlockSpec((D, lw_dim), row),
            pl.BlockSpec((lw_dim, tn), lambda b, t, j: (0, j)),
            pl.BlockSpec((D, la_dim), row),
            pl.BlockSpec((la_dim, tn), lambda b, t, j: (0, j)),
            pl.BlockSpec((D, lg_dim), row),
            pl.BlockSpec((lg_dim, tn), lambda b, t, j: (0, j)),
        ],
        out_specs=[act] * 6 + [pl.BlockSpec((nseq, 1, D), lambda b, t, j: (b, 0, 0))],
        out_shape=[out_sds] * 6 + [jax.ShapeDtypeStruct((B, 1, D), F32)],
        scratch_shapes=[
            pltpu.VMEM((tm, D), BF16), pltpu.VMEM((tm, D), BF16), pltpu.VMEM((tm, D), BF16),
            pltpu.VMEM((tm, lw_dim), BF16), pltpu.VMEM((tm, la_dim), BF16),
            pltpu.VMEM((tm, lg_dim), BF16), pltpu.VMEM((1, D), F32),
        ],
        compiler_params=_params(("arbitrary", "arbitrary", "arbitrary")),
        name="rw_proj",
    )(x.reshape(G, R, D), shift0.reshape(B, 1, D), ng.reshape(1, D), mu, w0.reshape(1, D),
      a0.reshape(1, D), wr, wk, wv, w1, w2, a1, a2, g1, g2)
    return [o.reshape(B, T, D) for o in outs[:6]] + [outs[6]]


def _wkv_kernel(r_ref, lw_ref, k_ref, v_ref, as_ref, g_ref, s0_ref,
                kkw_ref, kaw_ref, rk_ref, lng_ref, lnb_ref, *rest,
                chunk, n_sub, n_pairs, n_cast):
    cast_in = rest[:n_cast]
    o_ref, s_ref = rest[n_cast:n_cast + 2]
    cast_out = rest[n_cast + 2:2 * n_cast + 2]
    c_s = rest[-1]
    C = chunk
    C2 = 2 * C
    trow = lambda sc: slice(sc * C, (sc + 1) * C)

    for w_in, w_out in zip(cast_in, cast_out):
        w_out[...] = w_in[...].astype(w_out.dtype)

    @pl.when(pl.program_id(1) == 0)
    def _():
        s_ref[...] = s0_ref[...]

    tri = (_col_ids((C, C)) <= _row_ids((C, C))).astype(BF16)
    for sc in range(n_sub):
        lw_all = lw_ref[trow(sc), :]
        lw_hi = lw_all.astype(BF16)
        lw_r = lw_all - lw_hi.astype(F32)
        lw_mid = lw_r.astype(BF16)
        lw_lo = (lw_r - lw_mid.astype(F32)).astype(BF16)
        c_s[trow(sc), :] = (jnp.dot(tri, lw_hi, preferred_element_type=F32)
                            + jnp.dot(tri, lw_mid, preferred_element_type=F32)
                            + jnp.dot(tri, lw_lo, preferred_element_type=F32))

    head0 = _col_ids((C, PAIR)) < RW_HEAD
    ones_bd = ((_row_ids((PAIR, PAIR)) // RW_HEAD) == (_col_ids((PAIR, PAIR)) // RW_HEAD))
    ones_bd_bf = ones_bd.astype(BF16)

    rr = _row_ids((C2, C2))
    cc = _col_ids((C2, C2))
    same_head = (rr // C) == (cc // C)
    nb = NEUMANN_BLOCK
    same16 = (rr // nb) == (cc // nb)
    same32 = (rr // (2 * nb)) == (cc // (2 * nb))
    m_e1 = same32 & (~same16) if C >= 2 * nb else None
    m_e2 = same_head & (~same32) if C >= 4 * nb else None
    eye22 = (rr == cc).astype(F32)
    r12 = _row_ids((C, C2))
    c12 = _col_ids((C, C2))
    strict12 = (c12 % C) < r12
    incl12 = (c12 % C) <= r12
    left12 = c12 < C
    same16_c = ((c12 % C) // nb) == (r12 // nb)
    c16 = _col_ids((nb, C2))
    blk_of_lane = (c16 % C) // nb
    eye16 = ((c16 % nb) == _row_ids((nb, C2))).astype(F32)

    def expand(x):
        return jnp.where(same_head, jnp.concatenate([x, x], axis=0), 0.0)

    def expand16(x):
        return jnp.where(same16, jnp.concatenate([x] * (C2 // nb), axis=0), 0.0)

    def group_sum(x):
        return jnp.dot(x.astype(BF16), ones_bd_bf, preferred_element_type=F32)

    cat0 = lambda xs: jnp.concatenate(xs, axis=0)
    zero_h1 = lambda x: jnp.where(head0, x, 0.0)
    zero_h0 = lambda x: jnp.where(head0, 0.0, x)
    inv_n = 1.0 / RW_HEAD

    def state_free(items):
        n = range(len(items))
        tr = [trow(sc) for sc, _ in items]
        sl = [slice(p * PAIR, (p + 1) * PAIR) for _, p in items]
        r = [r_ref[tr[i], sl[i]] for i in n]
        lw = [lw_ref[tr[i], sl[i]] for i in n]
        k = [k_ref[tr[i], sl[i]] for i in n]
        v = [v_ref[tr[i], sl[i]] for i in n]
        asig = [as_ref[tr[i], sl[i]] for i in n]
        c = [c_s[tr[i], sl[i]] for i in n]
        c_end = [x[C - 1:C, :] for x in c]

        kkv = [k[i] * kkw_ref[:, sl[i]] for i in n]
        ss = [group_sum(x * x) for x in kkv]
        kk = [kkv[i] / jnp.maximum(jnp.sqrt(ss[i]), 1e-12) for i in n]
        b_in = [kk[i] * asig[i] for i in n]
        k_in = [k[i] * (1.0 + (asig[i] - 1.0) * kaw_ref[:, sl[i]]) for i in n]
        bonus_s = [group_sum(r[i] * k_in[i] * rk_ref[:, sl[i]]) for i in n]

        e_neg = [jnp.exp(-x) for x in c]
        a_t = [(-kk[i]) * jnp.exp(c[i] - lw[i]) for i in n]
        r_t = [r[i] * jnp.exp(c[i]) for i in n]
        b_h = [b_in[i] * e_neg[i] for i in n]
        k_h = [k_in[i] * e_neg[i] for i in n]
        e_end = [jnp.exp(c_end[i] - c[i]) for i in n]
        RE = [cat0([b_in[i] * e_end[i], k_in[i] * e_end[i]]) for i in n]

        L = [cat0([a_t[i], r_t[i]]) for i in n]
        G = [_dot_nt(L[i], cat0([zero_h1(b_h[i]), zero_h1(k_h[i]),
                                 zero_h0(k_h[i]), zero_h0(b_h[i])])) for i in n]
        G0 = [x[:, :C2] for x in G]
        G1 = [x[:, C2:] for x in G]
        g0t = [x[:C] for x in G0]
        g1t = [x[:C] for x in G1]

        n_c = [jnp.where(strict12, jnp.where(left12, g0t[i], g1t[i]), 0.0) for i in n]
        n_bd = [expand(x) for x in n_c]
        d1 = [sum(jnp.where(blk_of_lane == b, x[b * nb:(b + 1) * nb], 0.0)
                  for b in range(C // nb)) for x in n_c]
        d1_bd = [expand16(x) for x in d1]
        d2 = [_dot(d1[i], d1_bd[i]) for i in n]
        d2_bd = [expand16(x) for x in d2]
        d4 = [_dot(d2[i], d2_bd[i]) for i in n]
        d4_bd = [expand16(x) for x in d4]
        d8 = [_dot(d4[i], d4_bd[i]) for i in n]
        pa = [_dot(eye16 + d1[i], eye22 + d2_bd[i]) for i in n]
        pb = [_dot(eye16 + d4[i], eye22 + expand16(d8[i])) for i in n]
        t16 = [_dot(pa[i], expand16(pb[i])) for i in n]
        tcat = [jnp.where(same16_c, cat0([x] * (C // nb)), 0.0) for x in t16]
        for m_e in (m_e1, m_e2):
            if m_e is not None:
                t_bd = [expand(x) for x in tcat]
                x = [_dot(tcat[i], jnp.where(m_e, n_bd[i], 0.0)) for i in n]
                tcat = [tcat[i] + _dot(x[i], t_bd[i]) for i in n]

        v0 = [jnp.where(head0, x, 0.0) for x in v]
        v1 = [jnp.where(head0, 0.0, x) for x in v]
        av = [_dot(jnp.where(strict12, jnp.where(left12, g1t[i], g0t[i]), 0.0),
                   cat0([v1[i], v0[i]])) for i in n]
        bot = [jnp.concatenate([jnp.where(incl12, G0[i][C:], 0.0),
                                jnp.where(incl12, G1[i][C:], 0.0)], axis=1) for i in n]
        return dict(L=L, tcat=tcat, av=av, bot=bot, RE=RE, v=v, v0=v0, v1=v1,
                    decay=[jnp.exp(x) for x in c_end], bonus=bonus_s)

    def state_chain(sc, pre, sel):
        n = range(len(sel))
        at = lambda name: [pre[name][j] for j in sel]
        L, tcat, av, bot, RE, v, v0, v1, decay, bonus_s = (
            at(x) for x in ("L", "tcat", "av", "bot", "RE", "v", "v0", "v1", "decay", "bonus"))
        sl = [slice(p * PAIR, (p + 1) * PAIR) for p in range(n_pairs)]
        S = [s_ref[p] for p in range(n_pairs)]
        P = [_dot_nt(L[i], S[i]) for i in n]
        W = [P[i][:C] + av[i] for i in n]
        U = [_dot(tcat[i], cat0([zero_h1(W[i]), zero_h0(W[i])])) for i in n]
        Y = [P[i][C:] + _dot(bot[i], cat0([zero_h1(U[i]), v0[i], v1[i], zero_h0(U[i])]))
             for i in n]
        for i in n:
            upd = _dot_tn(cat0([U[i], v[i]]), RE[i])
            s_ref[i] = S[i] * decay[i] + jnp.where(ones_bd, upd, 0.0)

        mean = [group_sum(x) * inv_n for x in Y]
        dlt = [Y[i] - mean[i] for i in n]
        var = [group_sum(x * x) * inv_n for x in dlt]
        for i in n:
            yn = dlt[i] * lax.rsqrt(var[i] + GN_EPS) * lng_ref[:, sl[i]] + lnb_ref[:, sl[i]]
            out = (yn + bonus_s[i] * v[i]) * g_ref[trow(sc), sl[i]]
            o_ref[trow(sc), sl[i]] = out.astype(o_ref.dtype)

    items = [(sc, p) for sc in range(n_sub) for p in range(n_pairs)]
    pre = state_free(items)
    for sc in range(n_sub):
        state_chain(sc, pre, [sc * n_pairs + p for p in range(n_pairs)])


def _cast_slab(rows, n_steps):
    share = 1
    while rows * share % (n_steps * BF16_ROWS):
        share *= 2
    return rows * share // n_steps, share


def _wkv(r, lw, k, v, asig, g, s0_bd, kkw, kaw, rk, lng, lnb, cast=()):
    B, T, D = r.shape
    C = _tile(T, WKV_CHUNK)
    n_sub = _tile(T // C, WKV_CHUNKS_PER_STEP)
    n_pairs = D // PAIR
    steps_t = T // (n_sub * C)
    act = pl.BlockSpec((None, n_sub * C, D), lambda b, t: (b, t, 0))
    vec = pl.BlockSpec((1, D), lambda b, t: (0, 0))
    st = pl.BlockSpec((None, n_pairs, PAIR, PAIR), lambda b, t: (b, 0, 0, 0))
    cast_specs = []
    for w in cast:
        slab, share = _cast_slab(w.shape[0], B * steps_t)
        cast_specs.append(pl.BlockSpec(
            (slab, w.shape[1]), lambda b, t, share=share: ((b * steps_t + t) // share, 0)))
    outs = pl.pallas_call(
        functools.partial(_wkv_kernel, chunk=C, n_sub=n_sub, n_pairs=n_pairs,
                          n_cast=len(cast)),
        grid=(B, steps_t),
        in_specs=[act] * 6 + [st] + [vec] * 5 + cast_specs,
        out_specs=[act, st] + cast_specs,
        out_shape=[jax.ShapeDtypeStruct((B, T, D), BF16),
                   jax.ShapeDtypeStruct((B, n_pairs, PAIR, PAIR), F32)]
        + [jax.ShapeDtypeStruct(w.shape, BF16) for w in cast],
        scratch_shapes=[pltpu.VMEM((n_sub * C, D), F32)],
        compiler_params=_params(("arbitrary", "arbitrary")),
        name="wkv",
    )(r, lw, k, v, asig, g, s0_bd, kkw.reshape(1, D), kaw.reshape(1, D), rk.reshape(1, D),
      lng.reshape(1, D), lnb.reshape(1, D), *cast)
    return outs[0], outs[1], outs[2:]


def _state_to_pairs(s):
    B, H, n, _ = s.shape
    s = s.reshape(B, H // 2, 2, n, n)
    z = jnp.zeros((B, H // 2, n, n), s.dtype)
    top = jnp.concatenate([s[:, :, 0], z], axis=-1)
    bot = jnp.concatenate([z, s[:, :, 1]], axis=-1)
    return jnp.concatenate([top, bot], axis=-2)


def _pairs_to_state(sp):
    B, P, _, _ = sp.shape
    n = RW_HEAD
    h0 = sp[:, :, :n, :n]
    h1 = sp[:, :, n:, n:]
    return jnp.stack([h0, h1], axis=2).reshape(B, 2 * P, n, n)


def _ffn_kernel(x_ref, a_ref, wo_ref, conv0_ref, ng_ref, wg_ref, wv_ref, cw_ref, cb_ref,
                wd_ref, og_ref, o_ref, conv_out, xn_s, acc_s, carry_s,
                *, final_norm, nseq, seq_rows):
    t = pl.program_id(1)
    f = pl.program_id(2)
    nf = pl.num_programs(2)
    keep = CONV_W - 1

    @pl.when(f == 0)
    def _():
        x1 = x_ref[...] + jnp.dot(a_ref[...], wo_ref[...], preferred_element_type=F32)
        acc_s[...] = x1
        xn_s[...] = (_rms_scale(x1) * ng_ref[...]).astype(BF16)

    @pl.when(t == 0)
    def _():
        carry_s[f] = conv0_ref[0]

    xn = xn_s[...]
    gate = jnp.dot(xn, wg_ref[...], preferred_element_type=F32)
    val = jnp.dot(xn, wv_ref[...], preferred_element_type=F32)
    prev = carry_s[f]
    rows = _row_ids(gate.shape)
    g1 = jnp.where(rows == 0, prev[1:2, :], pltpu.roll(gate, 1, axis=0))
    g2 = jnp.where(rows == 0, prev[0:1, :],
                   jnp.where(rows == 1, prev[1:2, :], pltpu.roll(gate, 2, axis=0)))
    for s in range(1, nseq):
        first = s * seq_rows
        prev_s = conv0_ref[s]
        g1 = jnp.where(rows == first, prev_s[1:2, :], g1)
        g2 = jnp.where(rows == first, prev_s[0:1, :],
                       jnp.where(rows == first + 1, prev_s[1:2, :], g2))
    c = cb_ref[...] + cw_ref[0:1, :] * g2
    c = c + cw_ref[1:2, :] * g1
    c = c + cw_ref[2:3, :] * gate
    y = (c * jax.nn.sigmoid(c)) * val
    acc_s[...] += jnp.dot(y.astype(BF16), wd_ref[...], preferred_element_type=F32)
    tf = gate.shape[1]
    cols = pl.ds(pl.multiple_of(f * tf, tf), tf)
    for s in range(nseq):
        conv_out[s, :, cols] = gate[(s + 1) * seq_rows - keep:(s + 1) * seq_rows, :]
    carry_s[f] = gate[nseq * seq_rows - keep:nseq * seq_rows, :]

    @pl.when(f == nf - 1)
    def _():
        out = acc_s[...]
        if final_norm:
            out = _rms_scale(out) * og_ref[...]
        o_ref[...] = out


def _ffn(x, a, wo, conv0, ng, wup, cw, cb, wdown, og, *, layer, final_norm):
    B, T, D = x.shape
    F = wdown.shape[1]
    Ka = a.shape[-1]
    G, nseq, tm = _row_tiling(B, T)
    R = B * T // G
    tf = _tile(F, 512)
    nf = F // tf
    keep = CONV_W - 1
    row = lambda b, t, f: (0, 0)
    out, conv = pl.pallas_call(
        functools.partial(_ffn_kernel, final_norm=final_norm, nseq=nseq, seq_rows=tm // nseq),
        grid=(G, R // tm, nf),
        in_specs=[
            pl.BlockSpec((None, tm, D), lambda b, t, f: (b, t, 0)),
            pl.BlockSpec((None, tm, Ka), lambda b, t, f: (b, t, 0)),
            pl.BlockSpec((Ka, D), row, pipeline_mode=pl.Buffered(1)),
            pl.BlockSpec((nseq, keep, tf), lambda b, t, f: (b, 0, f)),
            pl.BlockSpec((1, D), row),
            pl.BlockSpec((None, D, tf), lambda b, t, f: (layer, 0, f)),
            pl.BlockSpec((None, D, tf), lambda b, t, f: (layer, 0, f + nf)),
            pl.BlockSpec((CONV_W, tf), lambda b, t, f: (0, f)),
            pl.BlockSpec((1, tf), lambda b, t, f: (0, f)),
            pl.BlockSpec((None, tf, D), lambda b, t, f: (layer, f, 0)),
            pl.BlockSpec((1, D), row),
        ],
        out_specs=[pl.BlockSpec((None, tm, D), lambda b, t, f: (b, t, 0)),
                   pl.BlockSpec((nseq, keep, F), lambda b, t, f: (b, 0, 0))],
        out_shape=[jax.ShapeDtypeStruct((G, R, D), F32),
                   jax.ShapeDtypeStruct((B, keep, F), F32)],
        scratch_shapes=[pltpu.VMEM((tm, D), BF16), pltpu.VMEM((tm, D), F32),
                        pltpu.VMEM((nf, keep, tf), F32)],
        compiler_params=_params(("arbitrary", "arbitrary", "arbitrary")),
        name="conv_ffn",
    )(x.reshape(G, R, D), a.reshape(G, R, Ka), wo, conv0, ng.reshape(1, D), wup, wup, cw,
      cb.reshape(1, F), wdown, og.reshape(1, D))
    return out.reshape(B, T, D), conv


def _kvq_kernel(x_ref, gkv_ref, gq_ref, wk_ref, wv_ref, wq_ref,
                k_out, v_out, kb_out, vb_out, q_out, xkv_s, xq_s, *, q_scale):
    @pl.when(pl.program_id(1) == 0)
    def _():
        xh = _rms_scale(x_ref[...])
        xkv_s[...] = (xh * gkv_ref[...]).astype(BF16)
        xq_s[...] = (xh * gq_ref[...]).astype(BF16)

    xkv = xkv_s[...]
    k = jnp.dot(xkv, wk_ref[...], preferred_element_type=F32)
    v = jnp.dot(xkv, wv_ref[...], preferred_element_type=F32)
    k_out[...] = k
    v_out[...] = v
    kb_out[...] = k.astype(BF16)
    vb_out[...] = v.astype(BF16)
    q = jnp.dot(xq_s[...], wq_ref[...], preferred_element_type=F32)
    q_out[...] = (q * q_scale).astype(BF16)


def _kvq(x, gkv, gq, wkv, wq):
    M, D = x.shape
    N = wq.shape[1]
    tm = _tile(M, ROW_TILE)
    tn = _tile(N, 512)
    nn = N // tn
    row = lambda i, j: (0, 0)
    blk = pl.BlockSpec((tm, tn), lambda i, j: (i, j))
    return pl.pallas_call(
        functools.partial(_kvq_kernel, q_scale=float(SB_HEAD_DIM) ** -0.5 * LOG2_E),
        grid=(M // tm, nn),
        in_specs=[pl.BlockSpec((tm, D), lambda i, j: (i, 0)),
                  pl.BlockSpec((1, D), row), pl.BlockSpec((1, D), row),
                  pl.BlockSpec((D, tn), lambda i, j: (0, j)),
                  pl.BlockSpec((D, tn), lambda i, j: (0, j + nn)),
                  pl.BlockSpec((D, tn), lambda i, j: (0, j))],
        out_specs=[blk] * 5,
        out_shape=[jax.ShapeDtypeStruct((M, N), F32), jax.ShapeDtypeStruct((M, N), F32),
                   jax.ShapeDtypeStruct((M, N), BF16), jax.ShapeDtypeStruct((M, N), BF16),
                   jax.ShapeDtypeStruct((M, N), BF16)],
        scratch_shapes=[pltpu.VMEM((tm, D), BF16), pltpu.VMEM((tm, D), BF16)],
        compiler_params=_params(("parallel", "arbitrary")),
        name="kvq_proj",
    )(x, gkv.reshape(1, D), gq.reshape(1, D), wkv, wkv, wq)


SB_DEAD = 152.0


def _sb_block(q, kb, vb, tri, carry, valid, on=None):
    z = _dot_nt(q, kb)
    sp = jnp.maximum(z, 0.0) + jnp.log2(1.0 + jnp.exp2(jnp.minimum(z, -z)))
    if valid is not None:
        sp = jnp.where(valid, sp, 0.0)
    if on is not None:
        sp = jnp.where(on, sp, 0.0)
    later = jnp.dot(sp.astype(BF16), tri, preferred_element_type=F32)
    logw = z - sp - later
    if carry is not None:
        logw = logw - carry
    w = jnp.exp2(logw)
    if valid is not None:
        w = jnp.where(valid, w, 0.0)
    if on is not None:
        w = jnp.where(on, w, 0.0)
    out = jnp.dot(w.astype(BF16), vb.astype(BF16), preferred_element_type=F32)
    return out, later[:, 0:1] + sp[:, 0:1]


def _sb_prompt_kernel(q_ref, k_ref, v_ref, tri_ref, o_ref, acc_s, carry_s, *, nsub, tb):
    qi = pl.program_id(2)
    blk0 = qi * nsub
    tri = tri_ref[...]
    valid = _col_ids((tb, tb)) < _row_ids((tb, tb))
    rows = lambda i: slice(i * tb, (i + 1) * tb)

    def kv_block(j):
        k0 = pl.multiple_of(j * tb, tb)
        return k_ref[pl.ds(k0, tb), :], v_ref[pl.ds(k0, tb), :]

    kvs = [kv_block(blk0 + i) for i in range(nsub)]
    acc, car = [], []
    for i in range(nsub):
        out, tot = _sb_block(q_ref[rows(i), :], kvs[i][0], kvs[i][1], tri, None, valid)
        acc.append(out)
        car.append(tot)
    for i in range(nsub):
        if i == 0:
            kb, vb = kv_block(jnp.maximum(blk0 - 1, 0))
            out, tot = _sb_block(q_ref[rows(0), :], kb, vb, tri, car[0], None, on=blk0 > 0)
        else:
            out, tot = _sb_block(q_ref[rows(i), :], kvs[i - 1][0], kvs[i - 1][1], tri,
                                 car[i], None)
        acc_s[i] = acc[i] + out
        carry_s[i] = car[i] + tot

    def live_min(t):
        m = jnp.float32(jnp.inf)
        for i in range(nsub):
            m = jnp.where(blk0 + i - 2 - t >= 0, jnp.minimum(m, jnp.min(carry_s[i])), m)
        return m

    def cond(st):
        t, m = st
        return jnp.logical_and(blk0 + nsub - 3 - t >= 0, m < SB_DEAD)

    def body(st):
        t, _ = st
        for i in range(nsub):
            j = blk0 + i - 2 - t
            kb, vb = kv_block(jnp.maximum(j, 0))
            out, tot = _sb_block(q_ref[rows(i), :], kb, vb, tri, carry_s[i], None, on=j >= 0)
            acc_s[i] += out
            carry_s[i] += tot
        return t + 1, live_min(t + 1)

    lax.while_loop(cond, body, (jnp.int32(0), live_min(0)))
    for i in range(nsub):
        o_ref[rows(i), :] = acc_s[i].astype(o_ref.dtype)


def _sb_decode_kernel(q_ref, k_ref, v_ref, kc_hbm, vc_hbm, trid_ref, trip_ref, o_ref,
                      acc_s, carry_s, near_k, near_v, near_sem, old_k, old_v, old_sem,
                      *, tq, tk, n_heads, n_old):
    b = pl.program_id(0)
    slot = b % 2
    hd = SB_HEAD_DIM
    newest = n_old * tk
    heads = range(n_heads)
    lanes = lambda h: slice(h * hd, (h + 1) * hd)

    def block_copies(row, k0, k_dst, v_dst, sem):
        cps = []
        for h in heads:
            cps.append(pltpu.make_async_copy(kc_hbm.at[row, pl.ds(k0, tk), h, :], k_dst.at[h],
                                             sem.at[0]))
            cps.append(pltpu.make_async_copy(vc_hbm.at[row, pl.ds(k0, tk), h, :], v_dst.at[h],
                                             sem.at[1]))
        return cps

    @pl.when(b == 0)
    def _():
        for c in block_copies(0, newest, near_k.at[0], near_v.at[0], near_sem.at[0]):
            c.start()

    @pl.when(b + 1 < pl.num_programs(0))
    def _():
        for c in block_copies(b + 1, newest, near_k.at[1 - slot], near_v.at[1 - slot],
                              near_sem.at[1 - slot]):
            c.start()

    valid = _col_ids((tq, tq)) < _row_ids((tq, tq))
    trid = trid_ref[...]
    trip = trip_ref[...]
    q = [q_ref[:, lanes(h)] for h in heads]
    own = [_sb_block(q[h], k_ref[:, lanes(h)], v_ref[:, lanes(h)], trid, None, valid)
           for h in heads]
    for c in block_copies(b, newest, near_k.at[slot], near_v.at[slot], near_sem.at[slot]):
        c.wait()
    for h in heads:
        acc, carry = own[h]
        out, tot = _sb_block(q[h], near_k[slot, h], near_v[slot, h], trip, carry, None)
        acc_s[h] = acc + out
        carry_s[h] = carry + tot

    def cond(st):
        t, m = st
        return jnp.logical_and(t < n_old, m < SB_DEAD)

    def body(st):
        t, _ = st
        k0 = pl.multiple_of((n_old - 1 - t) * tk, tk)
        copies = block_copies(b, k0, old_k, old_v, old_sem)
        for c in copies:
            c.start()
        for c in copies:
            c.wait()
        for h in heads:
            out, tot = _sb_block(q_ref[:, lanes(h)], old_k[h], old_v[h], trip, carry_s[h], None)
            acc_s[h] += out
            carry_s[h] += tot
        return t + 1, jnp.min(carry_s[...])

    lax.while_loop(cond, body, (jnp.int32(0), jnp.min(carry_s[...])))
    for h in heads:
        o_ref[:, lanes(h)] = acc_s[h].astype(o_ref.dtype)


def _suffix_ones(n):
    return (_row_ids((n, n)) > _col_ids((n, n))).astype(BF16)


SB_BLOCK = 256
SB_QSUB = 4


def _sb_attention(q, k, v, k_past=None, v_past=None):
    B, T, D = q.shape
    H = D // SB_HEAD_DIM
    hd = SB_HEAD_DIM
    full = lambda n: pl.BlockSpec((n, n), lambda b, h, i: (0, 0))
    seq = pl.BlockSpec((None, T, hd), lambda b, h, i: (b, 0, h))
    if k_past is None:
        tb = _tile(T, SB_BLOCK)
        nsub = _tile(T // tb, SB_QSUB)
        tq = nsub * tb
        qspec = pl.BlockSpec((None, tq, hd), lambda b, h, i: (b, i, h))
        return pl.pallas_call(
            functools.partial(_sb_prompt_kernel, nsub=nsub, tb=tb),
            grid=(B, H, T // tq),
            in_specs=[qspec, seq, seq, full(tb)],
            out_specs=qspec,
            out_shape=jax.ShapeDtypeStruct((B, T, D), BF16),
            scratch_shapes=[pltpu.VMEM((nsub, tb, hd), F32), pltpu.VMEM((nsub, tb, 1), F32)],
            compiler_params=_params(("parallel", "parallel", "arbitrary")),
            name="sb_attention",
        )(q, k, v, _suffix_ones(tb))
    P = k_past.shape[1]
    tk = _tile(P, SB_BLOCK)
    row = pl.BlockSpec((None, T, D), lambda b: (b, 0, 0))
    tri = lambda n: pl.BlockSpec((n, n), lambda b: (0, 0))
    hbm = pl.BlockSpec(memory_space=pl.ANY)
    return pl.pallas_call(
        functools.partial(_sb_decode_kernel, tq=T, tk=tk, n_heads=H, n_old=P // tk - 1),
        grid=(B,),
        in_specs=[row, row, row, hbm, hbm, tri(T), tri(tk)],
        out_specs=row,
        out_shape=jax.ShapeDtypeStruct((B, T, D), BF16),
        scratch_shapes=[pltpu.VMEM((H, T, hd), F32), pltpu.VMEM((H, T, 1), F32),
                        pltpu.VMEM((2, H, tk, hd), F32), pltpu.VMEM((2, H, tk, hd), F32),
                        pltpu.SemaphoreType.DMA((2, 2)),
                        pltpu.VMEM((H, tk, hd), F32), pltpu.VMEM((H, tk, hd), F32),
                        pltpu.SemaphoreType.DMA((2,))],
        compiler_params=_params(("arbitrary",)),
        name="sb_attention_decode",
    )(q, k, v, k_past, v_past, _suffix_ones(T), _suffix_ones(tk))


LATE_WEIGHTS = ('rw_wo', 'f_wup', 'f_wdown', 'w_kv', 'sb_wq', 'sb_wo')


def _trunk(x, shift0, wkv0, conv0, k_past, v_past, p, late_f32=None):
    B, T, D = x.shape
    M = B * T
    r, k, v, lw, asig, g, shift = _rw_proj(
        x, shift0[0], p['a_norm_g'][0], p['rw_mu'][0], p['rw_w0'][0], p['rw_a0'][0],
        p['rw_wr'], p['rw_wk'], p['rw_wv'], p['rw_w1'], p['rw_w2'], p['rw_a1'], p['rw_a2'],
        p['rw_g1'], p['rw_g2'])
    cast = () if late_f32 is None else tuple(
        late_f32[n].reshape(-1, late_f32[n].shape[-1]) for n in LATE_WEIGHTS)
    o, s_bd, casted = _wkv(r, lw, k, v, asig, g, _state_to_pairs(wkv0[0]),
                           p['rw_kk'][0], p['rw_ka'][0], p['rw_rk'][0].reshape(-1),
                           p['rw_lnx_g'][0], p['rw_lnx_b'][0], cast=cast)
    if late_f32 is not None:
        for n, w in zip(LATE_WEIGHTS, casted):
            p[n] = w.reshape(late_f32[n].shape)
    x, conv_a = _ffn(x, o, p['rw_wo'], conv0[0], p['f_norm_g'][0], p['f_wup'],
                     p['f_conv_w'][0], p['f_conv_b'][0], p['f_wdown'], p['out_norm_g'],
                     layer=0, final_norm=False)
    k_sh, v_sh, kb, vb, q = _kvq(x.reshape(M, D), p['kv_norm_g'], p['b_norm_g'][0],
                                 p['w_kv'], p['sb_wq'])
    d_att = q.shape[1]
    att = _sb_attention(q.reshape(B, T, d_att), kb.reshape(B, T, d_att), vb.reshape(B, T, d_att),
                        k_past, v_past)
    y, conv_b = _ffn(x, att, p['sb_wo'], conv0[1], p['f_norm_g'][1], p['f_wup'],
                     p['f_conv_w'][1], p['f_conv_b'][1], p['f_wdown'], p['out_norm_g'],
                     layer=1, final_norm=True)
    H = d_att // SB_HEAD_DIM
    return (y, _pairs_to_state(s_bd)[None], shift.reshape(1, B, D), jnp.stack([conv_a, conv_b]),
            k_sh.reshape(B, T, H, SB_HEAD_DIM), v_sh.reshape(B, T, H, SB_HEAD_DIM))


def kernel(x_prompt, x_sample, cache_k, cache_v, state_wkv, state_shift, state_conv, a_norm_g, rw_mu, rw_w0, rw_w1, rw_w2, rw_a0, rw_a1, rw_a2, rw_g1, rw_g2, rw_kk, rw_ka, rw_rk, rw_wr, rw_wk, rw_wv, rw_wo, rw_lnx_g, rw_lnx_b, kv_norm_g, w_kv, b_norm_g, sb_wq, sb_wo, f_norm_g, f_wup, f_conv_w, f_conv_b, f_wdown, out_norm_g):
    bf = lambda w: w.astype(BF16)
    p = dict(a_norm_g=a_norm_g, rw_mu=rw_mu, rw_w0=rw_w0, rw_a0=rw_a0,
             rw_w1=bf(rw_w1[0]), rw_w2=bf(rw_w2[0]), rw_a1=bf(rw_a1[0]), rw_a2=bf(rw_a2[0]),
             rw_g1=bf(rw_g1[0]), rw_g2=bf(rw_g2[0]),
             rw_kk=rw_kk, rw_ka=rw_ka, rw_rk=rw_rk,
             rw_wr=bf(rw_wr[0]), rw_wk=bf(rw_wk[0]), rw_wv=bf(rw_wv[0]),
             rw_lnx_g=rw_lnx_g, rw_lnx_b=rw_lnx_b, kv_norm_g=kv_norm_g,
             b_norm_g=b_norm_g, f_norm_g=f_norm_g,
             f_conv_w=f_conv_w, f_conv_b=f_conv_b, out_norm_g=out_norm_g)
    late_f32 = dict(rw_wo=rw_wo[0], f_wup=f_wup, f_wdown=f_wdown, w_kv=w_kv,
                    sb_wq=sb_wq[0], sb_wo=sb_wo[0])
    B, _, D = x_prompt.shape
    n_a = state_shift.shape[0]
    depth = state_conv.shape[0]
    F = state_conv.shape[-1]
    H = state_wkv.shape[2]
    shift0 = jnp.zeros((n_a, B, D), x_prompt.dtype)
    wkv0 = jnp.zeros((n_a, B, H, RW_HEAD, RW_HEAD), F32)
    conv0 = jnp.zeros((depth, B, CONV_W - 1, F), x_prompt.dtype)
    y_p, wkv_p, shift_p, conv_p, k_p, v_p = _trunk(x_prompt, shift0, wkv0, conv0, None, None, p,
                                                   late_f32=late_f32)
    y_s, wkv_s, shift_s, conv_s, k_s, v_s = _trunk(
        x_sample, state_shift, state_wkv.astype(F32), state_conv, cache_k, cache_v, p)
    return (y_p, y_s, wkv_p.astype(state_wkv.dtype), shift_p, conv_p, k_p, v_p,
            wkv_s.astype(state_wkv.dtype), shift_s, conv_s, k_s, v_s)
```

```python
import functools

import jax
import jax.numpy as jnp
from jax import lax
from jax.experimental import pallas as pl
from jax.experimental.pallas import tpu as pltpu

F32 = jnp.float32
BF16 = jnp.bfloat16

RW_HEAD = 64
PAIR = 2 * RW_HEAD
SB_HEAD_DIM = 128
GN_EPS = 64e-5
NORM_EPS = 1e-6
CONV_W = 3
RW_PROJ_DTYPES = (BF16, BF16, BF16, F32, BF16, BF16)
BF16_ROWS = 16
WKV_CHUNK = 64
NEUMANN_BLOCK = 16
WKV_CHUNKS_PER_STEP = 2
LOG2_E = 1.4426950408889634
ROW_TILE = 512
VMEM_LIMIT = 56 * 1024 * 1024


def _tile(n, pref):
    if n <= pref:
        return n
    t = pref
    while n % t:
        t //= 2
    return t


def _row_tiling(B, T):
    if T >= ROW_TILE:
        return B, 1, _tile(T, ROW_TILE)
    nseq = _tile(B, max(ROW_TILE // T, 1))
    return B // nseq, nseq, nseq * T


def _params(sem):
    return pltpu.CompilerParams(dimension_semantics=sem, vmem_limit_bytes=VMEM_LIMIT)


def _dot(a, b):
    return jnp.dot(a.astype(BF16), b.astype(BF16), preferred_element_type=F32)


def _dot_nt(a, b):
    return lax.dot_general(a.astype(BF16), b.astype(BF16), (((1,), (1,)), ((), ())),
                           preferred_element_type=F32)


def _dot_tn(a, b):
    return lax.dot_general(a.astype(BF16), b.astype(BF16), (((0,), (0,)), ((), ())),
                           preferred_element_type=F32)


def _rms_scale(x):
    return x * lax.rsqrt(jnp.mean(x * x, axis=-1, keepdims=True) + NORM_EPS)


def _softplus(u):
    return jnp.maximum(u, 0.0) + jnp.log1p(jnp.exp(-jnp.abs(u)))


def _row_ids(shape):
    return lax.broadcasted_iota(jnp.int32, shape, 0)


def _col_ids(shape):
    return lax.broadcasted_iota(jnp.int32, shape, 1)


def _rw_proj_kernel(x_ref, shift0_ref, ng_ref, mu_ref, w0_ref, a0_ref,
                    wr_ref, wk_ref, wv_ref, w1_ref, w2_ref, a1_ref, a2_ref, g1_ref, g2_ref,
                    r_out, k_out, v_out, lw_out, as_out, g_out, shift_out,
                    xr_s, xk_s, xv_s, hw_s, ha_s, hg_s, carry_s, *, nseq, seq_rows):
    t = pl.program_id(1)
    j = pl.program_id(2)

    @pl.when(j == 0)
    def _():
        @pl.when(t == 0)
        def _():
            carry_s[...] = shift0_ref[0]

        xn = _rms_scale(x_ref[...]) * ng_ref[...]
        rows = _row_ids(xn.shape)
        x_prev = jnp.where(rows == 0, carry_s[...], pltpu.roll(xn, 1, axis=0))
        for s in range(1, nseq):
            x_prev = jnp.where(rows == s * seq_rows, shift0_ref[s], x_prev)
        for s in range(nseq):
            shift_out[s] = xn[(s + 1) * seq_rows - 1:(s + 1) * seq_rows, :]
        carry_s[...] = xn[nseq * seq_rows - 1:nseq * seq_rows, :]
        xx = x_prev - xn
        mix = lambda i: (xn + xx * mu_ref[i:i + 1, :]).astype(BF16)
        xr_s[...] = mix(0)
        xk_s[...] = mix(2)
        xv_s[...] = mix(3)
        hw_s[...] = jnp.tanh(_dot(mix(1), w1_ref[...])).astype(BF16)
        ha_s[...] = _dot(mix(4), a1_ref[...]).astype(BF16)
        hg_s[...] = jax.nn.sigmoid(_dot(mix(5), g1_ref[...])).astype(BF16)

    r_out[...] = _dot(xr_s[...], wr_ref[...]).astype(r_out.dtype)
    k_out[...] = _dot(xk_s[...], wk_ref[...]).astype(k_out.dtype)
    v_out[...] = _dot(xv_s[...], wv_ref[...]).astype(v_out.dtype)
    w_log = -_softplus(-(w0_ref[...] + _dot(hw_s[...], w2_ref[...]))) - 0.5
    lw_out[...] = -jnp.exp(w_log)
    as_out[...] = jax.nn.sigmoid(a0_ref[...] + _dot(ha_s[...], a2_ref[...])).astype(as_out.dtype)
    g_out[...] = _dot(hg_s[...], g2_ref[...]).astype(g_out.dtype)


def _rw_proj(x, shift0, ng, mu, w0, a0, wr, wk, wv, w1, w2, a1, a2, g1, g2):
    B, T, D = x.shape
    G, nseq, tm = _row_tiling(B, T)
    R = B * T // G
    tn = _tile(D, 512)
    lw_dim, la_dim, lg_dim = w1.shape[1], a1.shape[1], g1.shape[1]
    row = lambda b, t, j: (0, 0)
    colv = pl.BlockSpec((1, tn), lambda b, t, j: (0, j))
    act = pl.BlockSpec((None, tm, tn), lambda b, t, j: (b, t, j))
    outs = pl.pallas_call(
        functools.partial(_rw_proj_kernel, nseq=nseq, seq_rows=tm // nseq),
        grid=(G, R // tm, D // tn),
        in_specs=[
            pl.BlockSpec((None, tm, D), lambda b, t, j: (b, t, 0)),
            pl.BlockSpec((nseq, 1, D), lambda b, t, j: (b, 0, 0)),
            pl.BlockSpec((1, D), row),
            pl.BlockSpec((6, D), row),
            colv, colv,
            pl.BlockSpec((D, tn), lambda b, t, j: (0, j)),
            pl.BlockSpec((D, tn), lambda b, t, j: (0, j)),
            pl.BlockSpec((D, tn), lambda b, t, j: (0, j)),
            pl.BlockSpec((D, lw_dim), row),
            pl.BlockSpec((lw_dim, tn), lambda b, t, j: (0, j)),
            pl.BlockSpec((D, la_dim), row),
            pl.BlockSpec((la_dim, tn), lambda b, t, j: (0, j)),
            pl.BlockSpec((D, lg_dim), row),
            pl.BlockSpec((lg_dim, tn), lambda b, t, j: (0, j)),
        ],
        out_specs=[act] * 6 + [pl.BlockSpec((nseq, 1, D), lambda b, t, j: (b, 0, 0))],
        out_shape=[jax.ShapeDtypeStruct((G, R, D), dt) for dt in RW_PROJ_DTYPES]
        + [jax.ShapeDtypeStruct((B, 1, D), F32)],
        scratch_shapes=[
            pltpu.VMEM((tm, D), BF16), pltpu.VMEM((tm, D), BF16), pltpu.VMEM((tm, D), BF16),
            pltpu.VMEM((tm, lw_dim), BF16), pltpu.VMEM((tm, la_dim), BF16),
            pltpu.VMEM((tm, lg_dim), BF16), pltpu.VMEM((1, D), F32),
        ],
        compiler_params=_params(("arbitrary", "arbitrary", "arbitrary")),
        name="rw_proj",
    )(x.reshape(G, R, D), shift0.reshape(B, 1, D), ng.reshape(1, D), mu, w0.reshape(1, D),
      a0.reshape(1, D), wr, wk, wv, w1, w2, a1, a2, g1, g2)
    return [o.reshape(B, T, D) for o in outs[:6]] + [outs[6]]


def _wkv_kernel(r_ref, lw_ref, k_ref, v_ref, as_ref, g_ref, s0_ref,
                kkw_ref, kaw_ref, rk_ref, lng_ref, lnb_ref, *rest,
                chunk, n_sub, n_pairs, n_cast):
    cast_in = rest[:n_cast]
    o_ref, s_ref = rest[n_cast:n_cast + 2]
    cast_out = rest[n_cast + 2:2 * n_cast + 2]
    c_s = rest[-1]
    C = chunk
    C2 = 2 * C
    trow = lambda sc: slice(sc * C, (sc + 1) * C)

    for w_in, w_out in zip(cast_in, cast_out):
        w_out[...] = w_in[...].astype(w_out.dtype)

    @pl.when(pl.program_id(1) == 0)
    def _():
        s_ref[...] = s0_ref[...]

    tri = (_col_ids((C, C)) <= _row_ids((C, C))).astype(BF16)
    for sc in range(n_sub):
        lw_all = lw_ref[trow(sc), :]
        lw_hi = lw_all.astype(BF16)
        lw_r = lw_all - lw_hi.astype(F32)
        lw_mid = lw_r.astype(BF16)
        lw_lo = (lw_r - lw_mid.astype(F32)).astype(BF16)
        c_s[trow(sc), :] = (jnp.dot(tri, lw_hi, preferred_element_type=F32)
                            + jnp.dot(tri, lw_mid, preferred_element_type=F32)
                            + jnp.dot(tri, lw_lo, preferred_element_type=F32))

    head0 = _col_ids((C, PAIR)) < RW_HEAD
    ones_bd = ((_row_ids((PAIR, PAIR)) // RW_HEAD) == (_col_ids((PAIR, PAIR)) // RW_HEAD))
    ones_bd_bf = ones_bd.astype(BF16)

    rr = _row_ids((C2, C2))
    cc = _col_ids((C2, C2))
    same_head = (rr // C) == (cc // C)
    nb = NEUMANN_BLOCK
    same16 = (rr // nb) == (cc // nb)
    same32 = (rr // (2 * nb)) == (cc // (2 * nb))
    m_e1 = same32 & (~same16) if C >= 2 * nb else None
    m_e2 = same_head & (~same32) if C >= 4 * nb else None
    eye22 = (rr == cc).astype(F32)
    r12 = _row_ids((C, C2))
    c12 = _col_ids((C, C2))
    strict12 = (c12 % C) < r12
    incl12 = (c12 % C) <= r12
    left12 = c12 < C
    same16_c = ((c12 % C) // nb) == (r12 // nb)
    c16 = _col_ids((nb, C2))
    blk_of_lane = (c16 % C) // nb
    eye16 = ((c16 % nb) == _row_ids((nb, C2))).astype(F32)

    def expand(x):
        return jnp.where(same_head, jnp.concatenate([x, x], axis=0), 0.0)

    def expand16(x):
        return jnp.where(same16, jnp.concatenate([x] * (C2 // nb), axis=0), 0.0)

    def group_sum(x):
        return jnp.dot(x.astype(BF16), ones_bd_bf, preferred_element_type=F32)

    cat0 = lambda xs: jnp.concatenate(xs, axis=0)
    zero_h1 = lambda x: jnp.where(head0, x, 0.0)
    zero_h0 = lambda x: jnp.where(head0, 0.0, x)
    inv_n = 1.0 / RW_HEAD

    def state_free(items):
        n = range(len(items))
        tr = [trow(sc) for sc, _ in items]
        sl = [slice(p * PAIR, (p + 1) * PAIR) for _, p in items]
        r = [r_ref[tr[i], sl[i]].astype(F32) for i in n]
        lw = [lw_ref[tr[i], sl[i]] for i in n]
        k = [k_ref[tr[i], sl[i]].astype(F32) for i in n]
        v = [v_ref[tr[i], sl[i]].astype(F32) for i in n]
        asig = [as_ref[tr[i], sl[i]].astype(F32) for i in n]
        c = [c_s[tr[i], sl[i]] for i in n]
        c_end = [x[C - 1:C, :] for x in c]

        kkv = [k[i] * kkw_ref[:, sl[i]] for i in n]
        ss = [group_sum(x * x) for x in kkv]
        kk = [kkv[i] / jnp.maximum(jnp.sqrt(ss[i]), 1e-12) for i in n]
        b_in = [kk[i] * asig[i] for i in n]
        k_in = [k[i] * (1.0 + (asig[i] - 1.0) * kaw_ref[:, sl[i]]) for i in n]
        bonus_s = [group_sum(r[i] * k_in[i] * rk_ref[:, sl[i]]) for i in n]

        e_neg = [jnp.exp(-x) for x in c]
        a_t = [(-kk[i]) * jnp.exp(c[i] - lw[i]) for i in n]
        r_t = [r[i] * jnp.exp(c[i]) for i in n]
        b_h = [b_in[i] * e_neg[i] for i in n]
        k_h = [k_in[i] * e_neg[i] for i in n]
        e_end = [jnp.exp(c_end[i] - c[i]) for i in n]
        RE = [cat0([b_in[i] * e_end[i], k_in[i] * e_end[i]]) for i in n]

        L = [cat0([a_t[i], r_t[i]]) for i in n]
        G = [_dot_nt(L[i], cat0([zero_h1(b_h[i]), zero_h1(k_h[i]),
                                 zero_h0(k_h[i]), zero_h0(b_h[i])])) for i in n]
        G0 = [x[:, :C2] for x in G]
        G1 = [x[:, C2:] for x in G]
        g0t = [x[:C] for x in G0]
        g1t = [x[:C] for x in G1]

        n_c = [jnp.where(strict12, jnp.where(left12, g0t[i], g1t[i]), 0.0) for i in n]
        n_bd = [expand(x) for x in n_c]
        d1 = [sum(jnp.where(blk_of_lane == b, x[b * nb:(b + 1) * nb], 0.0)
                  for b in range(C // nb)) for x in n_c]
        d1_bd = [expand16(x) for x in d1]
        d2 = [_dot(d1[i], d1_bd[i]) for i in n]
        d2_bd = [expand16(x) for x in d2]
        d4 = [_dot(d2[i], d2_bd[i]) for i in n]
        d4_bd = [expand16(x) for x in d4]
        d8 = [_dot(d4[i], d4_bd[i]) for i in n]
        pa = [_dot(eye16 + d1[i], eye22 + d2_bd[i]) for i in n]
        pb = [_dot(eye16 + d4[i], eye22 + expand16(d8[i])) for i in n]
        t16 = [_dot(pa[i], expand16(pb[i])) for i in n]
        tcat = [jnp.where(same16_c, cat0([x] * (C // nb)), 0.0) for x in t16]
        for m_e in (m_e1, m_e2):
            if m_e is not None:
                t_bd = [expand(x) for x in tcat]
                x = [_dot(tcat[i], jnp.where(m_e, n_bd[i], 0.0)) for i in n]
                tcat = [tcat[i] + _dot(x[i], t_bd[i]) for i in n]

        v0 = [jnp.where(head0, x, 0.0) for x in v]
        v1 = [jnp.where(head0, 0.0, x) for x in v]
        av = [_dot(jnp.where(strict12, jnp.where(left12, g1t[i], g0t[i]), 0.0),
                   cat0([v1[i], v0[i]])) for i in n]
        bot = [jnp.concatenate([jnp.where(incl12, G0[i][C:], 0.0),
                                jnp.where(incl12, G1[i][C:], 0.0)], axis=1) for i in n]
        return dict(L=L, tcat=tcat, av=av, bot=bot, RE=RE, v=v, v0=v0, v1=v1,
                    decay=[jnp.exp(x) for x in c_end], bonus=bonus_s)

    def state_chain(sc, pre, sel):
        n = range(len(sel))
        at = lambda name: [pre[name][j] for j in sel]
        L, tcat, av, bot, RE, v, v0, v1, decay, bonus_s = (
            at(x) for x in ("L", "tcat", "av", "bot", "RE", "v", "v0", "v1", "decay", "bonus"))
        sl = [slice(p * PAIR, (p + 1) * PAIR) for p in range(n_pairs)]
        S = [s_ref[p] for p in range(n_pairs)]
        P = [_dot_nt(L[i], S[i]) for i in n]
        W = [P[i][:C] + av[i] for i in n]
        U = [_dot(tcat[i], cat0([zero_h1(W[i]), zero_h0(W[i])])) for i in n]
        Y = [P[i][C:] + _dot(bot[i], cat0([zero_h1(U[i]), v0[i], v1[i], zero_h0(U[i])]))
             for i in n]
        for i in n:
            upd = _dot_tn(cat0([U[i], v[i]]), RE[i])
            s_ref[i] = S[i] * decay[i] + jnp.where(ones_bd, upd, 0.0)

        mean = [group_sum(x) * inv_n for x in Y]
        dlt = [Y[i] - mean[i] for i in n]
        var = [group_sum(x * x) * inv_n for x in dlt]
        for i in n:
            yn = dlt[i] * lax.rsqrt(var[i] + GN_EPS) * lng_ref[:, sl[i]] + lnb_ref[:, sl[i]]
            out = (yn + bonus_s[i] * v[i]) * g_ref[trow(sc), sl[i]].astype(F32)
            o_ref[trow(sc), sl[i]] = out.astype(o_ref.dtype)

    items = [(sc, p) for sc in range(n_sub) for p in range(n_pairs)]
    pre = state_free(items)
    for sc in range(n_sub):
        state_chain(sc, pre, [sc * n_pairs + p for p in range(n_pairs)])


def _cast_slab(rows, n_steps):
    share = 1
    while rows * share % (n_steps * BF16_ROWS):
        share *= 2
    return rows * share // n_steps, share


def _wkv(r, lw, k, v, asig, g, s0_bd, kkw, kaw, rk, lng, lnb, cast=()):
    B, T, D = r.shape
    C = _tile(T, WKV_CHUNK)
    n_sub = _tile(T // C, WKV_CHUNKS_PER_STEP)
    n_pairs = D // PAIR
    steps_t = T // (n_sub * C)
    act = pl.BlockSpec((None, n_sub * C, D), lambda b, t: (b, t, 0))
    vec = pl.BlockSpec((1, D), lambda b, t: (0, 0))
    st = pl.BlockSpec((None, n_pairs, PAIR, PAIR), lambda b, t: (b, 0, 0, 0))
    cast_specs = []
    for w in cast:
        slab, share = _cast_slab(w.shape[0], B * steps_t)
        cast_specs.append(pl.BlockSpec(
            (slab, w.shape[1]), lambda b, t, share=share: ((b * steps_t + t) // share, 0)))
    outs = pl.pallas_call(
        functools.partial(_wkv_kernel, chunk=C, n_sub=n_sub, n_pairs=n_pairs,
                          n_cast=len(cast)),
        grid=(B, steps_t),
        in_specs=[act] * 6 + [st] + [vec] * 5 + cast_specs,
        out_specs=[act, st] + cast_specs,
        out_shape=[jax.ShapeDtypeStruct((B, T, D), BF16),
                   jax.ShapeDtypeStruct((B, n_pairs, PAIR, PAIR), F32)]
        + [jax.ShapeDtypeStruct(w.shape, BF16) for w in cast],
        scratch_shapes=[pltpu.VMEM((n_sub * C, D), F32)],
        compiler_params=_params(("arbitrary", "arbitrary")),
        name="wkv",
    )(r, lw, k, v, asig, g, s0_bd, kkw.reshape(1, D), kaw.reshape(1, D), rk.reshape(1, D),
      lng.reshape(1, D), lnb.reshape(1, D), *cast)
    return outs[0], outs[1], outs[2:]


def _state_to_pairs(s):
    B, H, n, _ = s.shape
    s = s.reshape(B, H // 2, 2, n, n)
    z = jnp.zeros((B, H // 2, n, n), s.dtype)
    top = jnp.concatenate([s[:, :, 0], z], axis=-1)
    bot = jnp.concatenate([z, s[:, :, 1]], axis=-1)
    return jnp.concatenate([top, bot], axis=-2)


def _pairs_to_state(sp):
    B, P, _, _ = sp.shape
    n = RW_HEAD
    h0 = sp[:, :, :n, :n]
    h1 = sp[:, :, n:, n:]
    return jnp.stack([h0, h1], axis=2).reshape(B, 2 * P, n, n)


def _ffn_kernel(x_ref, a_ref, wo_ref, conv0_ref, ng_ref, wg_ref, wv_ref, cw_ref, cb_ref,
                wd_ref, og_ref, o_ref, conv_out, xn_s, acc_s, carry_s,
                *, final_norm, nseq, seq_rows):
    t = pl.program_id(1)
    f = pl.program_id(2)
    nf = pl.num_programs(2)
    keep = CONV_W - 1

    @pl.when(f == 0)
    def _():
        x1 = x_ref[...] + jnp.dot(a_ref[...], wo_ref[...], preferred_element_type=F32)
        acc_s[...] = x1
        xn_s[...] = (_rms_scale(x1) * ng_ref[...]).astype(BF16)

    @pl.when(t == 0)
    def _():
        carry_s[f] = conv0_ref[0]

    xn = xn_s[...]
    gate = jnp.dot(xn, wg_ref[...], preferred_element_type=F32)
    val = jnp.dot(xn, wv_ref[...], preferred_element_type=F32)
    prev = carry_s[f]
    rows = _row_ids(gate.shape)
    g1 = jnp.where(rows == 0, prev[1:2, :], pltpu.roll(gate, 1, axis=0))
    g2 = jnp.where(rows == 0, prev[0:1, :],
                   jnp.where(rows == 1, prev[1:2, :], pltpu.roll(gate, 2, axis=0)))
    for s in range(1, nseq):
        first = s * seq_rows
        prev_s = conv0_ref[s]
        g1 = jnp.where(rows == first, prev_s[1:2, :], g1)
        g2 = jnp.where(rows == first, prev_s[0:1, :],
                       jnp.where(rows == first + 1, prev_s[1:2, :], g2))
    c = cb_ref[...] + cw_ref[0:1, :] * g2
    c = c + cw_ref[1:2, :] * g1
    c = c + cw_ref[2:3, :] * gate
    y = (c * jax.nn.sigmoid(c)) * val
    acc_s[...] += jnp.dot(y.astype(BF16), wd_ref[...], preferred_element_type=F32)
    tf = gate.shape[1]
    cols = pl.ds(pl.multiple_of(f * tf, tf), tf)
    for s in range(nseq):
        conv_out[s, :, cols] = gate[(s + 1) * seq_rows - keep:(s + 1) * seq_rows, :]
    carry_s[f] = gate[nseq * seq_rows - keep:nseq * seq_rows, :]

    @pl.when(f == nf - 1)
    def _():
        out = acc_s[...]
        if final_norm:
            out = _rms_scale(out) * og_ref[...]
        o_ref[...] = out


def _ffn(x, a, wo, conv0, ng, wup, cw, cb, wdown, og, *, layer, final_norm):
    B, T, D = x.shape
    F = wdown.shape[1]
    Ka = a.shape[-1]
    G, nseq, tm = _row_tiling(B, T)
    R = B * T // G
    tf = _tile(F, 512)
    nf = F // tf
    keep = CONV_W - 1
    row = lambda b, t, f: (0, 0)
    out, conv = pl.pallas_call(
        functools.partial(_ffn_kernel, final_norm=final_norm, nseq=nseq, seq_rows=tm // nseq),
        grid=(G, R // tm, nf),
        in_specs=[
            pl.BlockSpec((None, tm, D), lambda b, t, f: (b, t, 0)),
            pl.BlockSpec((None, tm, Ka), lambda b, t, f: (b, t, 0)),
            pl.BlockSpec((Ka, D), row, pipeline_mode=pl.Buffered(1)),
            pl.BlockSpec((nseq, keep, tf), lambda b, t, f: (b, 0, f)),
            pl.BlockSpec((1, D), row),
            pl.BlockSpec((None, D, tf), lambda b, t, f: (layer, 0, f)),
            pl.BlockSpec((None, D, tf), lambda b, t, f: (layer, 0, f + nf)),
            pl.BlockSpec((CONV_W, tf), lambda b, t, f: (0, f)),
            pl.BlockSpec((1, tf), lambda b, t, f: (0, f)),
            pl.BlockSpec((None, tf, D), lambda b, t, f: (layer, f, 0)),
            pl.BlockSpec((1, D), row),
        ],
        out_specs=[pl.BlockSpec((None, tm, D), lambda b, t, f: (b, t, 0)),
                   pl.BlockSpec((nseq, keep, F), lambda b, t, f: (b, 0, 0))],
        out_shape=[jax.ShapeDtypeStruct((G, R, D), F32),
                   jax.ShapeDtypeStruct((B, keep, F), F32)],
        scratch_shapes=[pltpu.VMEM((tm, D), BF16), pltpu.VMEM((tm, D), F32),
                        pltpu.VMEM((nf, keep, tf), F32)],
        compiler_params=_params(("arbitrary", "arbitrary", "arbitrary")),
        name="conv_ffn",
    )(x.reshape(G, R, D), a.reshape(G, R, Ka), wo, conv0, ng.reshape(1, D), wup, wup, cw,
      cb.reshape(1, F), wdown, og.reshape(1, D))
    return out.reshape(B, T, D), conv


def _kvq_kernel(x_ref, gkv_ref, gq_ref, wk_ref, wv_ref, wq_ref,
                k_out, v_out, kb_out, vb_out, q_out, xkv_s, xq_s, *, q_scale):
    @pl.when(pl.program_id(1) == 0)
    def _():
        xh = _rms_scale(x_ref[...])
        xkv_s[...] = (xh * gkv_ref[...]).astype(BF16)
        xq_s[...] = (xh * gq_ref[...]).astype(BF16)

    xkv = xkv_s[...]
    k = jnp.dot(xkv, wk_ref[...], preferred_element_type=F32)
    v = jnp.dot(xkv, wv_ref[...], preferred_element_type=F32)
    k_out[...] = k
    v_out[...] = v
    kb_out[...] = k.astype(BF16)
    vb_out[...] = v.astype(BF16)
    q = jnp.dot(xq_s[...], wq_ref[...], preferred_element_type=F32)
    q_out[...] = (q * q_scale).astype(BF16)


def _kvq(x, gkv, gq, wkv, wq):
    M, D = x.shape
    N = wq.shape[1]
    tm = _tile(M, ROW_TILE)
    tn = _tile(N, 512)
    nn = N // tn
    row = lambda i, j: (0, 0)
    blk = pl.BlockSpec((tm, tn), lambda i, j: (i, j))
    return pl.pallas_call(
        functools.partial(_kvq_kernel, q_scale=float(SB_HEAD_DIM) ** -0.5 * LOG2_E),
        grid=(M // tm, nn),
        in_specs=[pl.BlockSpec((tm, D), lambda i, j: (i, 0)),
                  pl.BlockSpec((1, D), row), pl.BlockSpec((1, D), row),
                  pl.BlockSpec((D, tn), lambda i, j: (0, j)),
                  pl.BlockSpec((D, tn), lambda i, j: (0, j + nn)),
                  pl.BlockSpec((D, tn), lambda i, j: (0, j))],
        out_specs=[blk] * 5,
        out_shape=[jax.ShapeDtypeStruct((M, N), F32), jax.ShapeDtypeStruct((M, N), F32),
                   jax.ShapeDtypeStruct((M, N), BF16), jax.ShapeDtypeStruct((M, N), BF16),
                   jax.ShapeDtypeStruct((M, N), BF16)],
        scratch_shapes=[pltpu.VMEM((tm, D), BF16), pltpu.VMEM((tm, D), BF16)],
        compiler_params=_params(("parallel", "arbitrary")),
        name="kvq_proj",
    )(x, gkv.reshape(1, D), gq.reshape(1, D), wkv, wkv, wq)


SB_DEAD = 152.0


def _sb_block(q, kb, vb, tri, carry, valid, on=None):
    z = _dot_nt(q, kb)
    sp = jnp.maximum(z, 0.0) + jnp.log2(1.0 + jnp.exp2(jnp.minimum(z, -z)))
    if valid is not None:
        sp = jnp.where(valid, sp, 0.0)
    if on is not None:
        sp = jnp.where(on, sp, 0.0)
    later = jnp.dot(sp.astype(BF16), tri, preferred_element_type=F32)
    logw = z - sp - later
    if carry is not None:
        logw = logw - carry
    w = jnp.exp2(logw)
    if valid is not None:
        w = jnp.where(valid, w, 0.0)
    if on is not None:
        w = jnp.where(on, w, 0.0)
    out = jnp.dot(w.astype(BF16), vb.astype(BF16), preferred_element_type=F32)
    return out, later[:, 0:1] + sp[:, 0:1]


def _sb_prompt_kernel(q_ref, k_ref, v_ref, tri_ref, o_ref, acc_s, carry_s, *, nsub, tb):
    qi = pl.program_id(2)
    blk0 = qi * nsub
    tri = tri_ref[...]
    valid = _col_ids((tb, tb)) < _row_ids((tb, tb))
    rows = lambda i: slice(i * tb, (i + 1) * tb)

    def kv_block(j):
        k0 = pl.multiple_of(j * tb, tb)
        return k_ref[pl.ds(k0, tb), :], v_ref[pl.ds(k0, tb), :]

    kvs = [kv_block(blk0 + i) for i in range(nsub)]
    acc, car = [], []
    for i in range(nsub):
        out, tot = _sb_block(q_ref[rows(i), :], kvs[i][0], kvs[i][1], tri, None, valid)
        acc.append(out)
        car.append(tot)
    for i in range(nsub):
        if i == 0:
            kb, vb = kv_block(jnp.maximum(blk0 - 1, 0))
            out, tot = _sb_block(q_ref[rows(0), :], kb, vb, tri, car[0], None, on=blk0 > 0)
        else:
            out, tot = _sb_block(q_ref[rows(i), :], kvs[i - 1][0], kvs[i - 1][1], tri,
                                 car[i], None)
        acc_s[i] = acc[i] + out
        carry_s[i] = car[i] + tot

    def live_min(t):
        m = jnp.float32(jnp.inf)
        for i in range(nsub):
            m = jnp.where(blk0 + i - 2 - t >= 0, jnp.minimum(m, jnp.min(carry_s[i])), m)
        return m

    def cond(st):
        t, m = st
        return jnp.logical_and(blk0 + nsub - 3 - t >= 0, m < SB_DEAD)

    def body(st):
        t, _ = st
        for i in range(nsub):
            j = blk0 + i - 2 - t
            kb, vb = kv_block(jnp.maximum(j, 0))
            out, tot = _sb_block(q_ref[rows(i), :], kb, vb, tri, carry_s[i], None, on=j >= 0)
            acc_s[i] += out
            carry_s[i] += tot
        return t + 1, live_min(t + 1)

    lax.while_loop(cond, body, (jnp.int32(0), live_min(0)))
    for i in range(nsub):
        o_ref[rows(i), :] = acc_s[i].astype(o_ref.dtype)


def _sb_decode_kernel(q_ref, k_ref, v_ref, kc_hbm, vc_hbm, trid_ref, trip_ref, o_ref,
                      acc_s, carry_s, near_k, near_v, near_sem, old_k, old_v, old_sem,
                      *, tq, tk, n_heads, n_old):
    b = pl.program_id(0)
    slot = b % 2
    hd = SB_HEAD_DIM
    newest = n_old * tk
    heads = range(n_heads)
    lanes = lambda h: slice(h * hd, (h + 1) * hd)

    def block_copies(row, k0, k_dst, v_dst, sem):
        cps = []
        for h in heads:
            cps.append(pltpu.make_async_copy(kc_hbm.at[row, pl.ds(k0, tk), h, :], k_dst.at[h],
                                             sem.at[0]))
            cps.append(pltpu.make_async_copy(vc_hbm.at[row, pl.ds(k0, tk), h, :], v_dst.at[h],
                                             sem.at[1]))
        return cps

    @pl.when(b == 0)
    def _():
        for c in block_copies(0, newest, near_k.at[0], near_v.at[0], near_sem.at[0]):
            c.start()

    @pl.when(b + 1 < pl.num_programs(0))
    def _():
        for c in block_copies(b + 1, newest, near_k.at[1 - slot], near_v.at[1 - slot],
                              near_sem.at[1 - slot]):
            c.start()

    valid = _col_ids((tq, tq)) < _row_ids((tq, tq))
    trid = trid_ref[...]
    trip = trip_ref[...]
    q = [q_ref[:, lanes(h)] for h in heads]
    own = [_sb_block(q[h], k_ref[:, lanes(h)], v_ref[:, lanes(h)], trid, None, valid)
           for h in heads]
    for c in block_copies(b, newest, near_k.at[slot], near_v.at[slot], near_sem.at[slot]):
        c.wait()
    for h in heads:
        acc, carry = own[h]
        out, tot = _sb_block(q[h], near_k[slot, h], near_v[slot, h], trip, carry, None)
        acc_s[h] = acc + out
        carry_s[h] = carry + tot

    def cond(st):
        t, m = st
        return jnp.logical_and(t < n_old, m < SB_DEAD)

    def body(st):
        t, _ = st
        k0 = pl.multiple_of((n_old - 1 - t) * tk, tk)
        copies = block_copies(b, k0, old_k, old_v, old_sem)
        for c in copies:
            c.start()
        for c in copies:
            c.wait()
        for h in heads:
            out, tot = _sb_block(q_ref[:, lanes(h)], old_k[h], old_v[h], trip, carry_s[h], None)
            acc_s[h] += out
            carry_s[h] += tot
        return t + 1, jnp.min(carry_s[...])

    lax.while_loop(cond, body, (jnp.int32(0), jnp.min(carry_s[...])))
    for h in heads:
        o_ref[:, lanes(h)] = acc_s[h].astype(o_ref.dtype)


def _suffix_ones(n):
    return (_row_ids((n, n)) > _col_ids((n, n))).astype(BF16)


SB_BLOCK = 256
SB_QSUB = 4


def _sb_attention(q, k, v, k_past=None, v_past=None):
    B, T, D = q.shape
    H = D // SB_HEAD_DIM
    hd = SB_HEAD_DIM
    full = lambda n: pl.BlockSpec((n, n), lambda b, h, i: (0, 0))
    seq = pl.BlockSpec((None, T, hd), lambda b, h, i: (b, 0, h))
    if k_past is None:
        tb = _tile(T, SB_BLOCK)
        nsub = _tile(T // tb, SB_QSUB)
        tq = nsub * tb
        qspec = pl.BlockSpec((None, tq, hd), lambda b, h, i: (b, i, h))
        return pl.pallas_call(
            functools.partial(_sb_prompt_kernel, nsub=nsub, tb=tb),
            grid=(B, H, T // tq),
            in_specs=[qspec, seq, seq, full(tb)],
            out_specs=qspec,
            out_shape=jax.ShapeDtypeStruct((B, T, D), BF16),
            scratch_shapes=[pltpu.VMEM((nsub, tb, hd), F32), pltpu.VMEM((nsub, tb, 1), F32)],
            compiler_params=_params(("parallel", "parallel", "arbitrary")),
            name="sb_attention",
        )(q, k, v, _suffix_ones(tb))
    P = k_past.shape[1]
    tk = _tile(P, SB_BLOCK)
    row = pl.BlockSpec((None, T, D), lambda b: (b, 0, 0))
    tri = lambda n: pl.BlockSpec((n, n), lambda b: (0, 0))
    hbm = pl.BlockSpec(memory_space=pl.ANY)
    return pl.pallas_call(
        functools.partial(_sb_decode_kernel, tq=T, tk=tk, n_heads=H, n_old=P // tk - 1),
        grid=(B,),
        in_specs=[row, row, row, hbm, hbm, tri(T), tri(tk)],
        out_specs=row,
        out_shape=jax.ShapeDtypeStruct((B, T, D), BF16),
        scratch_shapes=[pltpu.VMEM((H, T, hd), F32), pltpu.VMEM((H, T, 1), F32),
                        pltpu.VMEM((2, H, tk, hd), F32), pltpu.VMEM((2, H, tk, hd), F32),
                        pltpu.SemaphoreType.DMA((2, 2)),
                        pltpu.VMEM((H, tk, hd), F32), pltpu.VMEM((H, tk, hd), F32),
                        pltpu.SemaphoreType.DMA((2,))],
        compiler_params=_params(("arbitrary",)),
        name="sb_attention_decode",
    )(q, k, v, k_past, v_past, _suffix_ones(T), _suffix_ones(tk))


LATE_WEIGHTS = ('rw_wo', 'f_wup', 'f_wdown', 'w_kv', 'sb_wq', 'sb_wo')


def _trunk(x, shift0, wkv0, conv0, k_past, v_past, p, late_f32=None):
    B, T, D = x.shape
    M = B * T
    r, k, v, lw, asig, g, shift = _rw_proj(
        x, shift0[0], p['a_norm_g'][0], p['rw_mu'][0], p['rw_w0'][0], p['rw_a0'][0],
        p['rw_wr'], p['rw_wk'], p['rw_wv'], p['rw_w1'], p['rw_w2'], p['rw_a1'], p['rw_a2'],
        p['rw_g1'], p['rw_g2'])
    cast = () if late_f32 is None else tuple(
        late_f32[n].reshape(-1, late_f32[n].shape[-1]) for n in LATE_WEIGHTS)
    o, s_bd, casted = _wkv(r, lw, k, v, asig, g, _state_to_pairs(wkv0[0]),
                           p['rw_kk'][0], p['rw_ka'][0], p['rw_rk'][0].reshape(-1),
                           p['rw_lnx_g'][0], p['rw_lnx_b'][0], cast=cast)
    if late_f32 is not None:
        for n, w in zip(LATE_WEIGHTS, casted):
            p[n] = w.reshape(late_f32[n].shape)
    x, conv_a = _ffn(x, o, p['rw_wo'], conv0[0], p['f_norm_g'][0], p['f_wup'],
                     p['f_conv_w'][0], p['f_conv_b'][0], p['f_wdown'], p['out_norm_g'],
                     layer=0, final_norm=False)
    k_sh, v_sh, kb, vb, q = _kvq(x.reshape(M, D), p['kv_norm_g'], p['b_norm_g'][0],
                                 p['w_kv'], p['sb_wq'])
    d_att = q.shape[1]
    att = _sb_attention(q.reshape(B, T, d_att), kb.reshape(B, T, d_att), vb.reshape(B, T, d_att),
                        k_past, v_past)
    y, conv_b = _ffn(x, att, p['sb_wo'], conv0[1], p['f_norm_g'][1], p['f_wup'],
                     p['f_conv_w'][1], p['f_conv_b'][1], p['f_wdown'], p['out_norm_g'],
                     layer=1, final_norm=True)
    H = d_att // SB_HEAD_DIM
    return (y, _pairs_to_state(s_bd)[None], shift.reshape(1, B, D), jnp.stack([conv_a, conv_b]),
            k_sh.reshape(B, T, H, SB_HEAD_DIM), v_sh.reshape(B, T, H, SB_HEAD_DIM))


def kernel(x_prompt, x_sample, cache_k, cache_v, state_wkv, state_shift, state_conv, a_norm_g, rw_mu, rw_w0, rw_w1, rw_w2, rw_a0, rw_a1, rw_a2, rw_g1, rw_g2, rw_kk, rw_ka, rw_rk, rw_wr, rw_wk, rw_wv, rw_wo, rw_lnx_g, rw_lnx_b, kv_norm_g, w_kv, b_norm_g, sb_wq, sb_wo, f_norm_g, f_wup, f_conv_w, f_conv_b, f_wdown, out_norm_g):
    bf = lambda w: w.astype(BF16)
    p = dict(a_norm_g=a_norm_g, rw_mu=rw_mu, rw_w0=rw_w0, rw_a0=rw_a0,
             rw_w1=bf(rw_w1[0]), rw_w2=bf(rw_w2[0]), rw_a1=bf(rw_a1[0]), rw_a2=bf(rw_a2[0]),
             rw_g1=bf(rw_g1[0]), rw_g2=bf(rw_g2[0]),
             rw_kk=rw_kk, rw_ka=rw_ka, rw_rk=rw_rk,
             rw_wr=bf(rw_wr[0]), rw_wk=bf(rw_wk[0]), rw_wv=bf(rw_wv[0]),
             rw_lnx_g=rw_lnx_g, rw_lnx_b=rw_lnx_b, kv_norm_g=kv_norm_g,
             b_norm_g=b_norm_g, f_norm_g=f_norm_g,
             f_conv_w=f_conv_w, f_conv_b=f_conv_b, out_norm_g=out_norm_g)
    late_f32 = dict(rw_wo=rw_wo[0], f_wup=f_wup, f_wdown=f_wdown, w_kv=w_kv,
                    sb_wq=sb_wq[0], sb_wo=sb_wo[0])
    B, _, D = x_prompt.shape
    n_a = state_shift.shape[0]
    depth = state_conv.shape[0]
    F = state_conv.shape[-1]
    H = state_wkv.shape[2]
    shift0 = jnp.zeros((n_a, B, D), x_prompt.dtype)
    wkv0 = jnp.zeros((n_a, B, H, RW_HEAD, RW_HEAD), F32)
    conv0 = jnp.zeros((depth, B, CONV_W - 1, F), x_prompt.dtype)
    y_p, wkv_p, shift_p, conv_p, k_p, v_p = _trunk(x_prompt, shift0, wkv0, conv0, None, None, p,
                                                   late_f32=late_f32)
    y_s, wkv_s, shift_s, conv_s, k_s, v_s = _trunk(
        x_sample, state_shift, state_wkv.astype(F32), state_conv, cache_k, cache_v, p)
    return (y_p, y_s, wkv_p.astype(state_wkv.dtype), shift_p, conv_p, k_p, v_p,
            wkv_s.astype(state_wkv.dtype), shift_s, conv_s, k_s, v_s)
```

```python
import functools

import jax
import jax.numpy as jnp
from jax import lax
from jax.experimental import pallas as pl
from jax.experimental.pallas import tpu as pltpu

F32 = jnp.float32
BF16 = jnp.bfloat16

RW_HEAD = 64
PAIR = 2 * RW_HEAD
SB_HEAD_DIM = 128
GN_EPS = 64e-5
NORM_EPS = 1e-6
CONV_W = 3
RW_PROJ_DTYPES = (BF16, BF16, BF16, F32, BF16, BF16)
BF16_ROWS = 16
WKV_CHUNK = 64
NEUMANN_BLOCK = 16
WKV_CHUNKS_PER_STEP = 2
LOG2_E = 1.4426950408889634
ROW_TILE = 512
VMEM_LIMIT = 56 * 1024 * 1024


def _tile(n, pref):
    if n <= pref:
        return n
    t = pref
    while n % t:
        t //= 2
    return t


def _row_tiling(B, T):
    if T >= ROW_TILE:
        return B, 1, _tile(T, ROW_TILE)
    nseq = _tile(B, max(ROW_TILE // T, 1))
    return B // nseq, nseq, nseq * T


def _params(sem):
    return pltpu.CompilerParams(dimension_semantics=sem, vmem_limit_bytes=VMEM_LIMIT)


def _dot(a, b):
    return jnp.dot(a.astype(BF16), b.astype(BF16), preferred_element_type=F32)


def _dot_nt(a, b):
    return lax.dot_general(a.astype(BF16), b.astype(BF16), (((1,), (1,)), ((), ())),
                           preferred_element_type=F32)


def _dot_tn(a, b):
    return lax.dot_general(a.astype(BF16), b.astype(BF16), (((0,), (0,)), ((), ())),
                           preferred_element_type=F32)


def _rms_scale(x):
    return x * lax.rsqrt(jnp.mean(x * x, axis=-1, keepdims=True) + NORM_EPS)


def _softplus(u):
    return jnp.maximum(u, 0.0) + jnp.log1p(jnp.exp(-jnp.abs(u)))


def _row_ids(shape):
    return lax.broadcasted_iota(jnp.int32, shape, 0)


def _col_ids(shape):
    return lax.broadcasted_iota(jnp.int32, shape, 1)


def _rw_proj_kernel(x_ref, shift0_ref, ng_ref, mu_ref, w0_ref, a0_ref,
                    wr_ref, wk_ref, wv_ref, w1_ref, w2_ref, a1_ref, a2_ref, g1_ref, g2_ref,
                    r_out, k_out, v_out, lw_out, as_out, g_out, shift_out,
                    xr_s, xk_s, xv_s, hw_s, ha_s, hg_s, carry_s, *, nseq, seq_rows):
    t = pl.program_id(1)
    j = pl.program_id(2)

    @pl.when(j == 0)
    def _():
        @pl.when(t == 0)
        def _():
            carry_s[...] = shift0_ref[0]

        xn = _rms_scale(x_ref[...]) * ng_ref[...]
        rows = _row_ids(xn.shape)
        x_prev = jnp.where(rows == 0, carry_s[...], pltpu.roll(xn, 1, axis=0))
        for s in range(1, nseq):
            x_prev = jnp.where(rows == s * seq_rows, shift0_ref[s], x_prev)
        for s in range(nseq):
            shift_out[s] = xn[(s + 1) * seq_rows - 1:(s + 1) * seq_rows, :]
        carry_s[...] = xn[nseq * seq_rows - 1:nseq * seq_rows, :]
        xx = x_prev - xn
        mix = lambda i: (xn + xx * mu_ref[i:i + 1, :]).astype(BF16)
        xr_s[...] = mix(0)
        xk_s[...] = mix(2)
        xv_s[...] = mix(3)
        hw_s[...] = jnp.tanh(_dot(mix(1), w1_ref[...])).astype(BF16)
        ha_s[...] = _dot(mix(4), a1_ref[...]).astype(BF16)
        hg_s[...] = jax.nn.sigmoid(_dot(mix(5), g1_ref[...])).astype(BF16)

    r_out[...] = _dot(xr_s[...], wr_ref[...]).astype(r_out.dtype)
    k_out[...] = _dot(xk_s[...], wk_ref[...]).astype(k_out.dtype)
    v_out[...] = _dot(xv_s[...], wv_ref[...]).astype(v_out.dtype)
    w_log = -_softplus(-(w0_ref[...] + _dot(hw_s[...], w2_ref[...]))) - 0.5
    lw_out[...] = -jnp.exp(w_log)
    as_out[...] = jax.nn.sigmoid(a0_ref[...] + _dot(ha_s[...], a2_ref[...])).astype(as_out.dtype)
    g_out[...] = _dot(hg_s[...], g2_ref[...]).astype(g_out.dtype)


def _rw_proj(x, shift0, ng, mu, w0, a0, wr, wk, wv, w1, w2, a1, a2, g1, g2):
    B, T, D = x.shape
    G, nseq, tm = _row_tiling(B, T)
    R = B * T // G
    tn = _tile(D, 512)
    lw_dim, la_dim, lg_dim = w1.shape[1], a1.shape[1], g1.shape[1]
    row = lambda b, t, j: (0, 0)
    colv = pl.BlockSpec((1, tn), lambda b, t, j: (0, j))
    act = pl.BlockSpec((None, tm, tn), lambda b, t, j: (b, t, j))
    outs = pl.pallas_call(
        functools.partial(_rw_proj_kernel, nseq=nseq, seq_rows=tm // nseq),
        grid=(G, R // tm, D // tn),
        in_specs=[
            pl.BlockSpec((None, tm, D), lambda b, t, j: (b, t, 0)),
            pl.BlockSpec((nseq, 1, D), lambda b, t, j: (b, 0, 0)),
            pl.BlockSpec((1, D), row),
            pl.BlockSpec((6, D), row),
            colv, colv,
            pl.BlockSpec((D, tn), lambda b, t, j: (0, j)),
            pl.BlockSpec((D, tn), lambda b, t, j: (0, j)),
            pl.BlockSpec((D, tn), lambda b, t, j: (0, j)),
            pl.BlockSpec((D, lw_dim), row),
            pl.BlockSpec((lw_dim, tn), lambda b, t, j: (0, j)),
            pl.BlockSpec((D, la_dim), row),
            pl.BlockSpec((la_dim, tn), lambda b, t, j: (0, j)),
            pl.BlockSpec((D, lg_dim), row),
            pl.BlockSpec((lg_dim, tn), lambda b, t, j: (0, j)),
        ],
        out_specs=[act] * 6 + [pl.BlockSpec((nseq, 1, D), lambda b, t, j: (b, 0, 0))],
        out_shape=[jax.ShapeDtypeStruct((G, R, D), dt) for dt in RW_PROJ_DTYPES]
        + [jax.ShapeDtypeStruct((B, 1, D), F32)],
        scratch_shapes=[
            pltpu.VMEM((tm, D), BF16), pltpu.VMEM((tm, D), BF16), pltpu.VMEM((tm, D), BF16),
            pltpu.VMEM((tm, lw_dim), BF16), pltpu.VMEM((tm, la_dim), BF16),
            pltpu.VMEM((tm, lg_dim), BF16), pltpu.VMEM((1, D), F32),
        ],
        compiler_params=_params(("arbitrary", "arbitrary", "arbitrary")),
        name="rw_proj",
    )(x.reshape(G, R, D), shift0.reshape(B, 1, D), ng.reshape(1, D), mu, w0.reshape(1, D),
      a0.reshape(1, D), wr, wk, wv, w1, w2, a1, a2, g1, g2)
    return [o.reshape(B, T, D) for o in outs[:6]] + [outs[6]]


def _wkv_kernel(r_ref, lw_ref, k_ref, v_ref, as_ref, g_ref, s0_ref,
                kkw_ref, kaw_ref, rk_ref, lng_ref, lnb_ref, *rest,
                chunk, n_sub, n_pairs, n_cast):
    cast_in = rest[:n_cast]
    o_ref, s_ref = rest[n_cast:n_cast + 2]
    cast_out = rest[n_cast + 2:2 * n_cast + 2]
    c_s = rest[-1]
    C = chunk
    C2 = 2 * C
    trow = lambda sc: slice(sc * C, (sc + 1) * C)

    for w_in, w_out in zip(cast_in, cast_out):
        w_out[...] = w_in[...].astype(w_out.dtype)

    @pl.when(pl.program_id(1) == 0)
    def _():
        s_ref[...] = s0_ref[...]

    tri = (_col_ids((C, C)) <= _row_ids((C, C))).astype(BF16)
    for sc in range(n_sub):
        lw_all = lw_ref[trow(sc), :]
        lw_hi = lw_all.astype(BF16)
        lw_r = lw_all - lw_hi.astype(F32)
        lw_mid = lw_r.astype(BF16)
        lw_lo = (lw_r - lw_mid.astype(F32)).astype(BF16)
        c_s[trow(sc), :] = (jnp.dot(tri, lw_hi, preferred_element_type=F32)
                            + jnp.dot(tri, lw_mid, preferred_element_type=F32)
                            + jnp.dot(tri, lw_lo, preferred_element_type=F32))

    head0 = _col_ids((C, PAIR)) < RW_HEAD
    ones_bd = ((_row_ids((PAIR, PAIR)) // RW_HEAD) == (_col_ids((PAIR, PAIR)) // RW_HEAD))
    ones_bd_bf = ones_bd.astype(BF16)

    rr = _row_ids((C2, C2))
    cc = _col_ids((C2, C2))
    same_head = (rr // C) == (cc // C)
    nb = NEUMANN_BLOCK
    same16 = (rr // nb) == (cc // nb)
    same32 = (rr // (2 * nb)) == (cc // (2 * nb))
    m_e1 = same32 & (~same16) if C >= 2 * nb else None
    m_e2 = same_head & (~same32) if C >= 4 * nb else None
    eye22 = (rr == cc).astype(F32)
    r12 = _row_ids((C, C2))
    c12 = _col_ids((C, C2))
    strict12 = (c12 % C) < r12
    incl12 = (c12 % C) <= r12
    left12 = c12 < C
    same16_c = ((c12 % C) // nb) == (r12 // nb)
    c16 = _col_ids((nb, C2))
    blk_of_lane = (c16 % C) // nb
    eye16 = ((c16 % nb) == _row_ids((nb, C2))).astype(F32)

    def expand(x):
        return jnp.where(same_head, jnp.concatenate([x, x], axis=0), 0.0)

    def expand16(x):
        return jnp.where(same16, jnp.concatenate([x] * (C2 // nb), axis=0), 0.0)

    def group_sum(x):
        return jnp.dot(x.astype(BF16), ones_bd_bf, preferred_element_type=F32)

    cat0 = lambda xs: jnp.concatenate(xs, axis=0)
    zero_h1 = lambda x: jnp.where(head0, x, 0.0)
    zero_h0 = lambda x: jnp.where(head0, 0.0, x)
    inv_n = 1.0 / RW_HEAD

    def state_free(items):
        n = range(len(items))
        tr = [trow(sc) for sc, _ in items]
        sl = [slice(p * PAIR, (p + 1) * PAIR) for _, p in items]
        r = [r_ref[tr[i], sl[i]].astype(F32) for i in n]
        lw = [lw_ref[tr[i], sl[i]] for i in n]
        k = [k_ref[tr[i], sl[i]].astype(F32) for i in n]
        v = [v_ref[tr[i], sl[i]].astype(F32) for i in n]
        asig = [as_ref[tr[i], sl[i]].astype(F32) for i in n]
        c = [c_s[tr[i], sl[i]] for i in n]
        c_end = [x[C - 1:C, :] for x in c]

        kkv = [k[i] * kkw_ref[:, sl[i]] for i in n]
        ss = [group_sum(x * x) for x in kkv]
        kk = [kkv[i] / jnp.maximum(jnp.sqrt(ss[i]), 1e-12) for i in n]
        b_in = [kk[i] * asig[i] for i in n]
        k_in = [k[i] * (1.0 + (asig[i] - 1.0) * kaw_ref[:, sl[i]]) for i in n]
        bonus_s = [group_sum(r[i] * k_in[i] * rk_ref[:, sl[i]]) for i in n]

        e_neg = [jnp.exp(-x) for x in c]
        a_t = [(-kk[i]) * jnp.exp(c[i] - lw[i]) for i in n]
        r_t = [r[i] * jnp.exp(c[i]) for i in n]
        b_h = [b_in[i] * e_neg[i] for i in n]
        k_h = [k_in[i] * e_neg[i] for i in n]
        e_end = [jnp.exp(c_end[i] - c[i]) for i in n]
        RE = [cat0([b_in[i] * e_end[i], k_in[i] * e_end[i]]) for i in n]

        L = [cat0([a_t[i], r_t[i]]) for i in n]
        G = [_dot_nt(L[i], cat0([zero_h1(b_h[i]), zero_h1(k_h[i]),
                                 zero_h0(k_h[i]), zero_h0(b_h[i])])) for i in n]
        G0 = [x[:, :C2] for x in G]
        G1 = [x[:, C2:] for x in G]
        g0t = [x[:C] for x in G0]
        g1t = [x[:C] for x in G1]

        n_c = [jnp.where(strict12, jnp.where(left12, g0t[i], g1t[i]), 0.0) for i in n]
        n_bd = [expand(x) for x in n_c]
        d1 = [sum(jnp.where(blk_of_lane == b, x[b * nb:(b + 1) * nb], 0.0)
                  for b in range(C // nb)) for x in n_c]
        d1_bd = [expand16(x) for x in d1]
        d2 = [_dot(d1[i], d1_bd[i]) for i in n]
        d2_bd = [expand16(x) for x in d2]
        d4 = [_dot(d2[i], d2_bd[i]) for i in n]
        d4_bd = [expand16(x) for x in d4]
        d8 = [_dot(d4[i], d4_bd[i]) for i in n]
        pa = [_dot(eye16 + d1[i], eye22 + d2_bd[i]) for i in n]
        pb = [_dot(eye16 + d4[i], eye22 + expand16(d8[i])) for i in n]
        t16 = [_dot(pa[i], expand16(pb[i])) for i in n]
        tcat = [jnp.where(same16_c, cat0([x] * (C // nb)), 0.0) for x in t16]
        for m_e in (m_e1, m_e2):
            if m_e is not None:
                t_bd = [expand(x) for x in tcat]
                x = [_dot(tcat[i], jnp.where(m_e, n_bd[i], 0.0)) for i in n]
                tcat = [tcat[i] + _dot(x[i], t_bd[i]) for i in n]

        v0 = [jnp.where(head0, x, 0.0) for x in v]
        v1 = [jnp.where(head0, 0.0, x) for x in v]
        av = [_dot(jnp.where(strict12, jnp.where(left12, g1t[i], g0t[i]), 0.0),
                   cat0([v1[i], v0[i]])) for i in n]
        bot = [jnp.concatenate([jnp.where(incl12, G0[i][C:], 0.0),
                                jnp.where(incl12, G1[i][C:], 0.0)], axis=1) for i in n]
        return dict(L=L, tcat=tcat, av=av, bot=bot, RE=RE, v=v, v0=v0, v1=v1,
                    decay=[jnp.exp(x) for x in c_end], bonus=bonus_s)

    def state_chain(sc, pre, sel):
        n = range(len(sel))
        at = lambda name: [pre[name][j] for j in sel]
        L, tcat, av, bot, RE, v, v0, v1, decay, bonus_s = (
            at(x) for x in ("L", "tcat", "av", "bot", "RE", "v", "v0", "v1", "decay", "bonus"))
        sl = [slice(p * PAIR, (p + 1) * PAIR) for p in range(n_pairs)]
        S = [s_ref[p] for p in range(n_pairs)]
        P = [_dot_nt(L[i], S[i]) for i in n]
        W = [P[i][:C] + av[i] for i in n]
        U = [_dot(tcat[i], cat0([zero_h1(W[i]), zero_h0(W[i])])) for i in n]
        Y = [P[i][C:] + _dot(bot[i], cat0([zero_h1(U[i]), v0[i], v1[i], zero_h0(U[i])]))
             for i in n]
        for i in n:
            upd = _dot_tn(cat0([U[i], v[i]]), RE[i])
            s_ref[i] = S[i] * decay[i] + jnp.where(ones_bd, upd, 0.0)

        mean = [group_sum(x) * inv_n for x in Y]
        dlt = [Y[i] - mean[i] for i in n]
        var = [group_sum(x * x) * inv_n for x in dlt]
        for i in n:
            yn = dlt[i] * lax.rsqrt(var[i] + GN_EPS) * lng_ref[:, sl[i]] + lnb_ref[:, sl[i]]
            out = (yn + bonus_s[i] * v[i]) * g_ref[trow(sc), sl[i]].astype(F32)
            o_ref[trow(sc), sl[i]] = out.astype(o_ref.dtype)

    items = [(sc, p) for sc in range(n_sub) for p in range(n_pairs)]
    pre = state_free(items)
    for sc in range(n_sub):
        state_chain(sc, pre, [sc * n_pairs + p for p in range(n_pairs)])


def _cast_slab(rows, n_steps):
    share = 1
    while rows * share % (n_steps * BF16_ROWS):
        share *= 2
    return rows * share // n_steps, share


def _wkv(r, lw, k, v, asig, g, s0_bd, kkw, kaw, rk, lng, lnb, cast=()):
    B, T, D = r.shape
    C = _tile(T, WKV_CHUNK)
    n_sub = _tile(T // C, WKV_CHUNKS_PER_STEP)
    n_pairs = D // PAIR
    steps_t = T // (n_sub * C)
    act = pl.BlockSpec((None, n_sub * C, D), lambda b, t: (b, t, 0))
    vec = pl.BlockSpec((1, D), lambda b, t: (0, 0))
    st = pl.BlockSpec((None, n_pairs, PAIR, PAIR), lambda b, t: (b, 0, 0, 0))
    cast_specs = []
    for w in cast:
        slab, share = _cast_slab(w.shape[0], B * steps_t)
        cast_specs.append(pl.BlockSpec(
            (slab, w.shape[1]), lambda b, t, share=share: ((b * steps_t + t) // share, 0)))
    outs = pl.pallas_call(
        functools.partial(_wkv_kernel, chunk=C, n_sub=n_sub, n_pairs=n_pairs,
                          n_cast=len(cast)),
        grid=(B, steps_t),
        in_specs=[act] * 6 + [st] + [vec] * 5 + cast_specs,
        out_specs=[act, st] + cast_specs,
        out_shape=[jax.ShapeDtypeStruct((B, T, D), BF16),
                   jax.ShapeDtypeStruct((B, n_pairs, PAIR, PAIR), F32)]
        + [jax.ShapeDtypeStruct(w.shape, BF16) for w in cast],
        scratch_shapes=[pltpu.VMEM((n_sub * C, D), F32)],
        compiler_params=_params(("arbitrary", "arbitrary")),
        name="wkv",
    )(r, lw, k, v, asig, g, s0_bd, kkw.reshape(1, D), kaw.reshape(1, D), rk.reshape(1, D),
      lng.reshape(1, D), lnb.reshape(1, D), *cast)
    return outs[0], outs[1], outs[2:]


def _state_to_pairs(s):
    B, H, n, _ = s.shape
    s = s.reshape(B, H // 2, 2, n, n)
    z = jnp.zeros((B, H // 2, n, n), s.dtype)
    top = jnp.concatenate([s[:, :, 0], z], axis=-1)
    bot = jnp.concatenate([z, s[:, :, 1]], axis=-1)
    return jnp.concatenate([top, bot], axis=-2)


def _pairs_to_state(sp):
    B, P, _, _ = sp.shape
    n = RW_HEAD
    h0 = sp[:, :, :n, :n]
    h1 = sp[:, :, n:, n:]
    return jnp.stack([h0, h1], axis=2).reshape(B, 2 * P, n, n)


def _ffn_kernel(x_ref, a_ref, wo_ref, conv0_ref, ng_ref, wg_ref, wv_ref, cw_ref, cb_ref,
                wd_ref, og_ref, o_ref, conv_out, xn_s, acc_s, carry_s,
                *, final_norm, nseq, seq_rows):
    t = pl.program_id(1)
    f = pl.program_id(2)
    nf = pl.num_programs(2)
    keep = CONV_W - 1

    @pl.when(f == 0)
    def _():
        x1 = x_ref[...] + jnp.dot(a_ref[...], wo_ref[...], preferred_element_type=F32)
        acc_s[...] = x1
        xn_s[...] = (_rms_scale(x1) * ng_ref[...]).astype(BF16)

    @pl.when(t == 0)
    def _():
        carry_s[f] = conv0_ref[0]

    xn = xn_s[...]
    gate = jnp.dot(xn, wg_ref[...], preferred_element_type=F32)
    val = jnp.dot(xn, wv_ref[...], preferred_element_type=F32)
    prev = carry_s[f]
    rows = _row_ids(gate.shape)
    g1 = jnp.where(rows == 0, prev[1:2, :], pltpu.roll(gate, 1, axis=0))
    g2 = jnp.where(rows == 0, prev[0:1, :],
                   jnp.where(rows == 1, prev[1:2, :], pltpu.roll(gate, 2, axis=0)))
    for s in range(1, nseq):
        first = s * seq_rows
        prev_s = conv0_ref[s]
        g1 = jnp.where(rows == first, prev_s[1:2, :], g1)
        g2 = jnp.where(rows == first, prev_s[0:1, :],
                       jnp.where(rows == first + 1, prev_s[1:2, :], g2))
    c = cb_ref[...] + cw_ref[0:1, :] * g2
    c = c + cw_ref[1:2, :] * g1
    c = c + cw_ref[2:3, :] * gate
    y = (c * jax.nn.sigmoid(c)) * val
    acc_s[...] += jnp.dot(y.astype(BF16), wd_ref[...], preferred_element_type=F32)
    tf = gate.shape[1]
    cols = pl.ds(pl.multiple_of(f * tf, tf), tf)
    for s in range(nseq):
        conv_out[s, :, cols] = gate[(s + 1) * seq_rows - keep:(s + 1) * seq_rows, :]
    carry_s[f] = gate[nseq * seq_rows - keep:nseq * seq_rows, :]

    @pl.when(f == nf - 1)
    def _():
        out = acc_s[...]
        if final_norm:
            out = _rms_scale(out) * og_ref[...]
        o_ref[...] = out


def _ffn(x, a, wo, conv0, ng, wup, cw, cb, wdown, og, *, layer, final_norm):
    B, T, D = x.shape
    F = wdown.shape[1]
    Ka = a.shape[-1]
    G, nseq, tm = _row_tiling(B, T)
    R = B * T // G
    tf = _tile(F, 512)
    nf = F // tf
    keep = CONV_W - 1
    row = lambda b, t, f: (0, 0)
    out, conv = pl.pallas_call(
        functools.partial(_ffn_kernel, final_norm=final_norm, nseq=nseq, seq_rows=tm // nseq),
        grid=(G, R // tm, nf),
        in_specs=[
            pl.BlockSpec((None, tm, D), lambda b, t, f: (b, t, 0)),
            pl.BlockSpec((None, tm, Ka), lambda b, t, f: (b, t, 0)),
            pl.BlockSpec((Ka, D), row, pipeline_mode=pl.Buffered(1)),
            pl.BlockSpec((nseq, keep, tf), lambda b, t, f: (b, 0, f)),
            pl.BlockSpec((1, D), row),
            pl.BlockSpec((None, D, tf), lambda b, t, f: (layer, 0, f)),
            pl.BlockSpec((None, D, tf), lambda b, t, f: (layer, 0, f + nf)),
            pl.BlockSpec((CONV_W, tf), lambda b, t, f: (0, f)),
            pl.BlockSpec((1, tf), lambda b, t, f: (0, f)),
            pl.BlockSpec((None, tf, D), lambda b, t, f: (layer, f, 0)),
            pl.BlockSpec((1, D), row),
        ],
        out_specs=[pl.BlockSpec((None, tm, D), lambda b, t, f: (b, t, 0)),
                   pl.BlockSpec((nseq, keep, F), lambda b, t, f: (b, 0, 0))],
        out_shape=[jax.ShapeDtypeStruct((G, R, D), F32),
                   jax.ShapeDtypeStruct((B, keep, F), F32)],
        scratch_shapes=[pltpu.VMEM((tm, D), BF16), pltpu.VMEM((tm, D), F32),
                        pltpu.VMEM((nf, keep, tf), F32)],
        compiler_params=_params(("arbitrary", "arbitrary", "arbitrary")),
        name="conv_ffn",
    )(x.reshape(G, R, D), a.reshape(G, R, Ka), wo, conv0, ng.reshape(1, D), wup, wup, cw,
      cb.reshape(1, F), wdown, og.reshape(1, D))
    return out.reshape(B, T, D), conv


def _kvq_kernel(x_ref, gkv_ref, gq_ref, wk_ref, wv_ref, wq_ref,
                k_out, v_out, q_out, xkv_s, xq_s, *, q_scale):
    @pl.when(pl.program_id(1) == 0)
    def _():
        xh = _rms_scale(x_ref[...])
        xkv_s[...] = (xh * gkv_ref[...]).astype(BF16)
        xq_s[...] = (xh * gq_ref[...]).astype(BF16)

    xkv = xkv_s[...]
    k_out[...] = jnp.dot(xkv, wk_ref[...], preferred_element_type=F32)
    v_out[...] = jnp.dot(xkv, wv_ref[...], preferred_element_type=F32)
    q = jnp.dot(xq_s[...], wq_ref[...], preferred_element_type=F32)
    q_out[...] = (q * q_scale).astype(BF16)


def _kvq(x, gkv, gq, wkv, wq):
    M, D = x.shape
    N = wq.shape[1]
    tm = _tile(M, ROW_TILE)
    tn = _tile(N, 512)
    nn = N // tn
    row = lambda i, j: (0, 0)
    blk = pl.BlockSpec((tm, tn), lambda i, j: (i, j))
    return pl.pallas_call(
        functools.partial(_kvq_kernel, q_scale=float(SB_HEAD_DIM) ** -0.5 * LOG2_E),
        grid=(M // tm, nn),
        in_specs=[pl.BlockSpec((tm, D), lambda i, j: (i, 0)),
                  pl.BlockSpec((1, D), row), pl.BlockSpec((1, D), row),
                  pl.BlockSpec((D, tn), lambda i, j: (0, j)),
                  pl.BlockSpec((D, tn), lambda i, j: (0, j + nn)),
                  pl.BlockSpec((D, tn), lambda i, j: (0, j))],
        out_specs=[blk] * 3,
        out_shape=[jax.ShapeDtypeStruct((M, N), F32), jax.ShapeDtypeStruct((M, N), F32),
                   jax.ShapeDtypeStruct((M, N), BF16)],
        scratch_shapes=[pltpu.VMEM((tm, D), BF16), pltpu.VMEM((tm, D), BF16)],
        compiler_params=_params(("parallel", "arbitrary")),
        name="kvq_proj",
    )(x, gkv.reshape(1, D), gq.reshape(1, D), wkv, wkv, wq)


SB_DEAD = 152.0


def _sb_block(q, kb, vb, tri, carry, valid, on=None):
    z = _dot_nt(q, kb)
    sp = jnp.maximum(z, 0.0) + jnp.log2(1.0 + jnp.exp2(jnp.minimum(z, -z)))
    if valid is not None:
        sp = jnp.where(valid, sp, 0.0)
    if on is not None:
        sp = jnp.where(on, sp, 0.0)
    later = jnp.dot(sp.astype(BF16), tri, preferred_element_type=F32)
    logw = z - sp - later
    if carry is not None:
        logw = logw - carry
    w = jnp.exp2(logw)
    if valid is not None:
        w = jnp.where(valid, w, 0.0)
    if on is not None:
        w = jnp.where(on, w, 0.0)
    out = jnp.dot(w.astype(BF16), vb.astype(BF16), preferred_element_type=F32)
    return out, later[:, 0:1] + sp[:, 0:1]


def _sb_prompt_kernel(q_ref, k_ref, v_ref, tri_ref, o_ref, acc_s, carry_s, *, nsub, tb):
    qi = pl.program_id(2)
    blk0 = qi * nsub
    tri = tri_ref[...]
    valid = _col_ids((tb, tb)) < _row_ids((tb, tb))
    rows = lambda i: slice(i * tb, (i + 1) * tb)

    def kv_block(j):
        k0 = pl.multiple_of(j * tb, tb)
        return k_ref[pl.ds(k0, tb), :], v_ref[pl.ds(k0, tb), :]

    kvs = [kv_block(blk0 + i) for i in range(nsub)]
    acc, car = [], []
    for i in range(nsub):
        out, tot = _sb_block(q_ref[rows(i), :], kvs[i][0], kvs[i][1], tri, None, valid)
        acc.append(out)
        car.append(tot)
    for i in range(nsub):
        if i == 0:
            kb, vb = kv_block(jnp.maximum(blk0 - 1, 0))
            out, tot = _sb_block(q_ref[rows(0), :], kb, vb, tri, car[0], None, on=blk0 > 0)
        else:
            out, tot = _sb_block(q_ref[rows(i), :], kvs[i - 1][0], kvs[i - 1][1], tri,
                                 car[i], None)
        acc_s[i] = acc[i] + out
        carry_s[i] = car[i] + tot

    def live_min(t):
        m = jnp.float32(jnp.inf)
        for i in range(nsub):
            m = jnp.where(blk0 + i - 2 - t >= 0, jnp.minimum(m, jnp.min(carry_s[i])), m)
        return m

    def cond(st):
        t, m = st
        return jnp.logical_and(blk0 + nsub - 3 - t >= 0, m < SB_DEAD)

    def body(st):
        t, _ = st
        for i in range(nsub):
            j = blk0 + i - 2 - t
            kb, vb = kv_block(jnp.maximum(j, 0))
            out, tot = _sb_block(q_ref[rows(i), :], kb, vb, tri, carry_s[i], None, on=j >= 0)
            acc_s[i] += out
            carry_s[i] += tot
        return t + 1, live_min(t + 1)

    lax.while_loop(cond, body, (jnp.int32(0), live_min(0)))
    for i in range(nsub):
        o_ref[rows(i), :] = acc_s[i].astype(o_ref.dtype)


def _sb_decode_kernel(q_ref, k_ref, v_ref, kc_hbm, vc_hbm, trid_ref, trip_ref, o_ref,
                      acc_s, carry_s, near_k, near_v, near_sem, old_k, old_v, old_sem,
                      *, tq, tk, n_heads, n_old):
    b = pl.program_id(0)
    slot = b % 2
    hd = SB_HEAD_DIM
    newest = n_old * tk
    heads = range(n_heads)
    lanes = lambda h: slice(h * hd, (h + 1) * hd)

    def block_copies(row, k0, k_dst, v_dst, sem):
        cps = []
        for h in heads:
            cps.append(pltpu.make_async_copy(kc_hbm.at[row, pl.ds(k0, tk), h, :], k_dst.at[h],
                                             sem.at[0]))
            cps.append(pltpu.make_async_copy(vc_hbm.at[row, pl.ds(k0, tk), h, :], v_dst.at[h],
                                             sem.at[1]))
        return cps

    @pl.when(b == 0)
    def _():
        for c in block_copies(0, newest, near_k.at[0], near_v.at[0], near_sem.at[0]):
            c.start()

    @pl.when(b + 1 < pl.num_programs(0))
    def _():
        for c in block_copies(b + 1, newest, near_k.at[1 - slot], near_v.at[1 - slot],
                              near_sem.at[1 - slot]):
            c.start()

    valid = _col_ids((tq, tq)) < _row_ids((tq, tq))
    trid = trid_ref[...]
    trip = trip_ref[...]
    q = [q_ref[:, lanes(h)] for h in heads]
    own = [_sb_block(q[h], k_ref[:, lanes(h)], v_ref[:, lanes(h)], trid, None, valid)
           for h in heads]
    for c in block_copies(b, newest, near_k.at[slot], near_v.at[slot], near_sem.at[slot]):
        c.wait()
    for h in heads:
        acc, carry = own[h]
        out, tot = _sb_block(q[h], near_k[slot, h], near_v[slot, h], trip, carry, None)
        acc_s[h] = acc + out
        carry_s[h] = carry + tot

    def cond(st):
        t, m = st
        return jnp.logical_and(t < n_old, m < SB_DEAD)

    def body(st):
        t, _ = st
        k0 = pl.multiple_of((n_old - 1 - t) * tk, tk)
        copies = block_copies(b, k0, old_k, old_v, old_sem)
        for c in copies:
            c.start()
        for c in copies:
            c.wait()
        for h in heads:
            out, tot = _sb_block(q_ref[:, lanes(h)], old_k[h], old_v[h], trip, carry_s[h], None)
            acc_s[h] += out
            carry_s[h] += tot
        return t + 1, jnp.min(carry_s[...])

    lax.while_loop(cond, body, (jnp.int32(0), jnp.min(carry_s[...])))
    for h in heads:
        o_ref[:, lanes(h)] = acc_s[h].astype(o_ref.dtype)


def _suffix_ones(n):
    return (_row_ids((n, n)) > _col_ids((n, n))).astype(BF16)


SB_BLOCK = 256
SB_QSUB = 4


def _sb_attention(q, k, v, k_past=None, v_past=None):
    B, T, D = q.shape
    H = D // SB_HEAD_DIM
    hd = SB_HEAD_DIM
    full = lambda n: pl.BlockSpec((n, n), lambda b, h, i: (0, 0))
    seq = pl.BlockSpec((None, T, hd), lambda b, h, i: (b, 0, h))
    if k_past is None:
        tb = _tile(T, SB_BLOCK)
        nsub = _tile(T // tb, SB_QSUB)
        tq = nsub * tb
        qspec = pl.BlockSpec((None, tq, hd), lambda b, h, i: (b, i, h))
        return pl.pallas_call(
            functools.partial(_sb_prompt_kernel, nsub=nsub, tb=tb),
            grid=(B, H, T // tq),
            in_specs=[qspec, seq, seq, full(tb)],
            out_specs=qspec,
            out_shape=jax.ShapeDtypeStruct((B, T, D), BF16),
            scratch_shapes=[pltpu.VMEM((nsub, tb, hd), F32), pltpu.VMEM((nsub, tb, 1), F32)],
            compiler_params=_params(("parallel", "parallel", "arbitrary")),
            name="sb_attention",
        )(q, k, v, _suffix_ones(tb))
    P = k_past.shape[1]
    tk = _tile(P, SB_BLOCK)
    row = pl.BlockSpec((None, T, D), lambda b: (b, 0, 0))
    tri = lambda n: pl.BlockSpec((n, n), lambda b: (0, 0))
    hbm = pl.BlockSpec(memory_space=pl.ANY)
    return pl.pallas_call(
        functools.partial(_sb_decode_kernel, tq=T, tk=tk, n_heads=H, n_old=P // tk - 1),
        grid=(B,),
        in_specs=[row, row, row, hbm, hbm, tri(T), tri(tk)],
        out_specs=row,
        out_shape=jax.ShapeDtypeStruct((B, T, D), BF16),
        scratch_shapes=[pltpu.VMEM((H, T, hd), F32), pltpu.VMEM((H, T, 1), F32),
                        pltpu.VMEM((2, H, tk, hd), F32), pltpu.VMEM((2, H, tk, hd), F32),
                        pltpu.SemaphoreType.DMA((2, 2)),
                        pltpu.VMEM((H, tk, hd), F32), pltpu.VMEM((H, tk, hd), F32),
                        pltpu.SemaphoreType.DMA((2,))],
        compiler_params=_params(("arbitrary",)),
        name="sb_attention_decode",
    )(q, k, v, k_past, v_past, _suffix_ones(T), _suffix_ones(tk))


LATE_WEIGHTS = ('rw_wo', 'f_wup', 'f_wdown', 'w_kv', 'sb_wq', 'sb_wo')


def _trunk(x, shift0, wkv0, conv0, k_past, v_past, p, late_f32=None):
    B, T, D = x.shape
    M = B * T
    r, k, v, lw, asig, g, shift = _rw_proj(
        x, shift0[0], p['a_norm_g'][0], p['rw_mu'][0], p['rw_w0'][0], p['rw_a0'][0],
        p['rw_wr'], p['rw_wk'], p['rw_wv'], p['rw_w1'], p['rw_w2'], p['rw_a1'], p['rw_a2'],
        p['rw_g1'], p['rw_g2'])
    cast = () if late_f32 is None else tuple(
        late_f32[n].reshape(-1, late_f32[n].shape[-1]) for n in LATE_WEIGHTS)
    o, s_bd, casted = _wkv(r, lw, k, v, asig, g, _state_to_pairs(wkv0[0]),
                           p['rw_kk'][0], p['rw_ka'][0], p['rw_rk'][0].reshape(-1),
                           p['rw_lnx_g'][0], p['rw_lnx_b'][0], cast=cast)
    if late_f32 is not None:
        for n, w in zip(LATE_WEIGHTS, casted):
            p[n] = w.reshape(late_f32[n].shape)
    x, conv_a = _ffn(x, o, p['rw_wo'], conv0[0], p['f_norm_g'][0], p['f_wup'],
                     p['f_conv_w'][0], p['f_conv_b'][0], p['f_wdown'], p['out_norm_g'],
                     layer=0, final_norm=False)
    k_sh, v_sh, q = _kvq(x.reshape(M, D), p['kv_norm_g'], p['b_norm_g'][0],
                         p['w_kv'], p['sb_wq'])
    d_att = q.shape[1]
    att = _sb_attention(q.reshape(B, T, d_att), k_sh.reshape(B, T, d_att),
                        v_sh.reshape(B, T, d_att), k_past, v_past)
    y, conv_b = _ffn(x, att, p['sb_wo'], conv0[1], p['f_norm_g'][1], p['f_wup'],
                     p['f_conv_w'][1], p['f_conv_b'][1], p['f_wdown'], p['out_norm_g'],
                     layer=1, final_norm=True)
    H = d_att // SB_HEAD_DIM
    return (y, _pairs_to_state(s_bd)[None], shift.reshape(1, B, D), jnp.stack([conv_a, conv_b]),
            k_sh.reshape(B, T, H, SB_HEAD_DIM), v_sh.reshape(B, T, H, SB_HEAD_DIM))


def kernel(x_prompt, x_sample, cache_k, cache_v, state_wkv, state_shift, state_conv, a_norm_g, rw_mu, rw_w0, rw_w1, rw_w2, rw_a0, rw_a1, rw_a2, rw_g1, rw_g2, rw_kk, rw_ka, rw_rk, rw_wr, rw_wk, rw_wv, rw_wo, rw_lnx_g, rw_lnx_b, kv_norm_g, w_kv, b_norm_g, sb_wq, sb_wo, f_norm_g, f_wup, f_conv_w, f_conv_b, f_wdown, out_norm_g):
    bf = lambda w: w.astype(BF16)
    p = dict(a_norm_g=a_norm_g, rw_mu=rw_mu, rw_w0=rw_w0, rw_a0=rw_a0,
             rw_w1=bf(rw_w1[0]), rw_w2=bf(rw_w2[0]), rw_a1=bf(rw_a1[0]), rw_a2=bf(rw_a2[0]),
             rw_g1=bf(rw_g1[0]), rw_g2=bf(rw_g2[0]),
             rw_kk=rw_kk, rw_ka=rw_ka, rw_rk=rw_rk,
             rw_wr=bf(rw_wr[0]), rw_wk=bf(rw_wk[0]), rw_wv=bf(rw_wv[0]),
             rw_lnx_g=rw_lnx_g, rw_lnx_b=rw_lnx_b, kv_norm_g=kv_norm_g,
             b_norm_g=b_norm_g, f_norm_g=f_norm_g,
             f_conv_w=f_conv_w, f_conv_b=f_conv_b, out_norm_g=out_norm_g)
    late_f32 = dict(rw_wo=rw_wo[0], f_wup=f_wup, f_wdown=f_wdown, w_kv=w_kv,
                    sb_wq=sb_wq[0], sb_wo=sb_wo[0])
    B, _, D = x_prompt.shape
    n_a = state_shift.shape[0]
    depth = state_conv.shape[0]
    F = state_conv.shape[-1]
    H = state_wkv.shape[2]
    shift0 = jnp.zeros((n_a, B, D), x_prompt.dtype)
    wkv0 = jnp.zeros((n_a, B, H, RW_HEAD, RW_HEAD), F32)
    conv0 = jnp.zeros((depth, B, CONV_W - 1, F), x_prompt.dtype)
    y_p, wkv_p, shift_p, conv_p, k_p, v_p = _trunk(x_prompt, shift0, wkv0, conv0, None, None, p,
                                                   late_f32=late_f32)
    y_s, wkv_s, shift_s, conv_s, k_s, v_s = _trunk(
        x_sample, state_shift, state_wkv.astype(F32), state_conv, cache_k, cache_v, p)
    return (y_p, y_s, wkv_p.astype(state_wkv.dtype), shift_p, conv_p, k_p, v_p,
            wkv_s.astype(state_wkv.dtype), shift_s, conv_s, k_s, v_s)
```

```python
import functools

import jax
import jax.numpy as jnp
from jax import lax
from jax.experimental import pallas as pl
from jax.experimental.pallas import tpu as pltpu

F32 = jnp.float32
BF16 = jnp.bfloat16

RW_HEAD = 64
PAIR = 2 * RW_HEAD
SB_HEAD_DIM = 128
GN_EPS = 64e-5
NORM_EPS = 1e-6
CONV_W = 3
RW_PROJ_DTYPES = (BF16, BF16, BF16, F32, BF16, BF16)
BF16_ROWS = 16
WKV_CHUNK = 64
NEUMANN_BLOCK = 16
WKV_CHUNKS_PER_STEP = 2
LOG2_E = 1.4426950408889634
ROW_TILE = 512
VMEM_LIMIT = 56 * 1024 * 1024


def _tile(n, pref):
    if n <= pref:
        return n
    t = pref
    while n % t:
        t //= 2
    return t


def _row_tiling(B, T):
    if T >= ROW_TILE:
        return B, 1, _tile(T, ROW_TILE)
    nseq = _tile(B, max(ROW_TILE // T, 1))
    return B // nseq, nseq, nseq * T


def _params(sem):
    return pltpu.CompilerParams(dimension_semantics=sem, vmem_limit_bytes=VMEM_LIMIT)


def _dot(a, b):
    return jnp.dot(a.astype(BF16), b.astype(BF16), preferred_element_type=F32)


def _dot_nt(a, b):
    return lax.dot_general(a.astype(BF16), b.astype(BF16), (((1,), (1,)), ((), ())),
                           preferred_element_type=F32)


def _dot_tn(a, b):
    return lax.dot_general(a.astype(BF16), b.astype(BF16), (((0,), (0,)), ((), ())),
                           preferred_element_type=F32)


def _rms_scale(x):
    return x * lax.rsqrt(jnp.mean(x * x, axis=-1, keepdims=True) + NORM_EPS)


def _softplus(u):
    return jnp.maximum(u, 0.0) + jnp.log1p(jnp.exp(-jnp.abs(u)))


def _row_ids(shape):
    return lax.broadcasted_iota(jnp.int32, shape, 0)


def _col_ids(shape):
    return lax.broadcasted_iota(jnp.int32, shape, 1)


def _rw_proj_kernel(x_ref, shift0_ref, ng_ref, mu_ref, w0_ref, a0_ref,
                    wr_ref, wk_ref, wv_ref, w1_ref, w2_ref, a1_ref, a2_ref, g1_ref, g2_ref,
                    r_out, k_out, v_out, lw_out, as_out, g_out, shift_out,
                    xr_s, xk_s, xv_s, hw_s, ha_s, hg_s, carry_s, *, nseq, seq_rows):
    t = pl.program_id(1)
    j = pl.program_id(2)

    @pl.when(j == 0)
    def _():
        @pl.when(t == 0)
        def _():
            carry_s[...] = shift0_ref[0]

        xn = _rms_scale(x_ref[...]) * ng_ref[...]
        rows = _row_ids(xn.shape)
        x_prev = jnp.where(rows == 0, carry_s[...], pltpu.roll(xn, 1, axis=0))
        for s in range(1, nseq):
            x_prev = jnp.where(rows == s * seq_rows, shift0_ref[s], x_prev)
        for s in range(nseq):
            shift_out[s] = xn[(s + 1) * seq_rows - 1:(s + 1) * seq_rows, :]
        carry_s[...] = xn[nseq * seq_rows - 1:nseq * seq_rows, :]
        xx = x_prev - xn
        mix = lambda i: (xn + xx * mu_ref[i:i + 1, :]).astype(BF16)
        xr_s[...] = mix(0)
        xk_s[...] = mix(2)
        xv_s[...] = mix(3)
        hw_s[...] = jnp.tanh(_dot(mix(1), w1_ref[...])).astype(BF16)
        ha_s[...] = _dot(mix(4), a1_ref[...]).astype(BF16)
        hg_s[...] = jax.nn.sigmoid(_dot(mix(5), g1_ref[...])).astype(BF16)

    r_out[...] = _dot(xr_s[...], wr_ref[...]).astype(r_out.dtype)
    k_out[...] = _dot(xk_s[...], wk_ref[...]).astype(k_out.dtype)
    v_out[...] = _dot(xv_s[...], wv_ref[...]).astype(v_out.dtype)
    w_log = -_softplus(-(w0_ref[...] + _dot(hw_s[...], w2_ref[...]))) - 0.5
    lw_out[...] = -jnp.exp(w_log)
    as_out[...] = jax.nn.sigmoid(a0_ref[...] + _dot(ha_s[...], a2_ref[...])).astype(as_out.dtype)
    g_out[...] = _dot(hg_s[...], g2_ref[...]).astype(g_out.dtype)


def _rw_proj(x, shift0, ng, mu, w0, a0, wr, wk, wv, w1, w2, a1, a2, g1, g2):
    B, T, D = x.shape
    G, nseq, tm = _row_tiling(B, T)
    R = B * T // G
    tn = _tile(D, 512)
    lw_dim, la_dim, lg_dim = w1.shape[1], a1.shape[1], g1.shape[1]
    row = lambda b, t, j: (0, 0)
    colv = pl.BlockSpec((1, tn), lambda b, t, j: (0, j))
    act = pl.BlockSpec((None, tm, tn), lambda b, t, j: (b, t, j))
    outs = pl.pallas_call(
        functools.partial(_rw_proj_kernel, nseq=nseq, seq_rows=tm // nseq),
        grid=(G, R // tm, D // tn),
        in_specs=[
            pl.BlockSpec((None, tm, D), lambda b, t, j: (b, t, 0)),
            pl.BlockSpec((nseq, 1, D), lambda b, t, j: (b, 0, 0)),
            pl.BlockSpec((1, D), row),
            pl.BlockSpec((6, D), row),
            colv, colv,
            pl.BlockSpec((D, tn), lambda b, t, j: (0, j)),
            pl.BlockSpec((D, tn), lambda b, t, j: (0, j)),
            pl.BlockSpec((D, tn), lambda b, t, j: (0, j)),
            pl.BlockSpec((D, lw_dim), row),
            pl.BlockSpec((lw_dim, tn), lambda b, t, j: (0, j)),
            pl.BlockSpec((D, la_dim), row),
            pl.BlockSpec((la_dim, tn), lambda b, t, j: (0, j)),
            pl.BlockSpec((D, lg_dim), row),
            pl.BlockSpec((lg_dim, tn), lambda b, t, j: (0, j)),
        ],
        out_specs=[act] * 6 + [pl.BlockSpec((nseq, 1, D), lambda b, t, j: (b, 0, 0))],
        out_shape=[jax.ShapeDtypeStruct((G, R, D), dt) for dt in RW_PROJ_DTYPES]
        + [jax.ShapeDtypeStruct((B, 1, D), F32)],
        scratch_shapes=[
            pltpu.VMEM((tm, D), BF16), pltpu.VMEM((tm, D), BF16), pltpu.VMEM((tm, D), BF16),
            pltpu.VMEM((tm, lw_dim), BF16), pltpu.VMEM((tm, la_dim), BF16),
            pltpu.VMEM((tm, lg_dim), BF16), pltpu.VMEM((1, D), F32),
        ],
        compiler_params=_params(("arbitrary", "arbitrary", "arbitrary")),
        name="rw_proj",
    )(x.reshape(G, R, D), shift0.reshape(B, 1, D), ng.reshape(1, D), mu, w0.reshape(1, D),
      a0.reshape(1, D), wr, wk, wv, w1, w2, a1, a2, g1, g2)
    return [o.reshape(B, T, D) for o in outs[:6]] + [outs[6]]


def _wkv_kernel(r_ref, lw_ref, k_ref, v_ref, as_ref, g_ref, s0_ref,
                kkw_ref, kaw_ref, rk_ref, lng_ref, lnb_ref, *rest,
                chunk, n_sub, n_pairs, n_cast):
    cast_in = rest[:n_cast]
    o_ref, s_ref = rest[n_cast:n_cast + 2]
    cast_out = rest[n_cast + 2:2 * n_cast + 2]
    c_s = rest[-1]
    C = chunk
    C2 = 2 * C
    trow = lambda sc: slice(sc * C, (sc + 1) * C)

    for w_in, w_out in zip(cast_in, cast_out):
        w_out[...] = w_in[...].astype(w_out.dtype)

    @pl.when(pl.program_id(1) == 0)
    def _():
        s_ref[...] = s0_ref[...]

    tri = (_col_ids((C, C)) <= _row_ids((C, C))).astype(BF16)
    for sc in range(n_sub):
        lw_all = lw_ref[trow(sc), :]
        lw_hi = lw_all.astype(BF16)
        lw_r = lw_all - lw_hi.astype(F32)
        lw_mid = lw_r.astype(BF16)
        lw_lo = (lw_r - lw_mid.astype(F32)).astype(BF16)
        c_s[trow(sc), :] = (jnp.dot(tri, lw_hi, preferred_element_type=F32)
                            + jnp.dot(tri, lw_mid, preferred_element_type=F32)
                            + jnp.dot(tri, lw_lo, preferred_element_type=F32))

    head0 = _col_ids((C, PAIR)) < RW_HEAD
    ones_bd = ((_row_ids((PAIR, PAIR)) // RW_HEAD) == (_col_ids((PAIR, PAIR)) // RW_HEAD))
    ones_bd_bf = ones_bd.astype(BF16)

    rr = _row_ids((C2, C2))
    cc = _col_ids((C2, C2))
    same_head = (rr // C) == (cc // C)
    nb = NEUMANN_BLOCK
    same16 = (rr // nb) == (cc // nb)
    same32 = (rr // (2 * nb)) == (cc // (2 * nb))
    m_e1 = same32 & (~same16) if C >= 2 * nb else None
    m_e2 = same_head & (~same32) if C >= 4 * nb else None
    eye22 = (rr == cc).astype(F32)
    r12 = _row_ids((C, C2))
    c12 = _col_ids((C, C2))
    strict12 = (c12 % C) < r12
    incl12 = (c12 % C) <= r12
    left12 = c12 < C
    same16_c = ((c12 % C) // nb) == (r12 // nb)
    c16 = _col_ids((nb, C2))
    blk_of_lane = (c16 % C) // nb
    eye16 = ((c16 % nb) == _row_ids((nb, C2))).astype(F32)

    def expand(x):
        return jnp.where(same_head, jnp.concatenate([x, x], axis=0), 0.0)

    def expand16(x):
        return jnp.where(same16, jnp.concatenate([x] * (C2 // nb), axis=0), 0.0)

    def group_sum(x):
        return jnp.dot(x.astype(BF16), ones_bd_bf, preferred_element_type=F32)

    cat0 = lambda xs: jnp.concatenate(xs, axis=0)
    zero_h1 = lambda x: jnp.where(head0, x, 0.0)
    zero_h0 = lambda x: jnp.where(head0, 0.0, x)
    inv_n = 1.0 / RW_HEAD

    def state_free(items):
        n = range(len(items))
        tr = [trow(sc) for sc, _ in items]
        sl = [slice(p * PAIR, (p + 1) * PAIR) for _, p in items]
        r = [r_ref[tr[i], sl[i]].astype(F32) for i in n]
        lw = [lw_ref[tr[i], sl[i]] for i in n]
        k = [k_ref[tr[i], sl[i]].astype(F32) for i in n]
        v = [v_ref[tr[i], sl[i]].astype(F32) for i in n]
        asig = [as_ref[tr[i], sl[i]].astype(F32) for i in n]
        c = [c_s[tr[i], sl[i]] for i in n]
        c_end = [x[C - 1:C, :] for x in c]

        kkv = [k[i] * kkw_ref[:, sl[i]] for i in n]
        ss = [group_sum(x * x) for x in kkv]
        kk = [kkv[i] / jnp.maximum(jnp.sqrt(ss[i]), 1e-12) for i in n]
        b_in = [kk[i] * asig[i] for i in n]
        k_in = [k[i] * (1.0 + (asig[i] - 1.0) * kaw_ref[:, sl[i]]) for i in n]
        bonus_s = [group_sum(r[i] * k_in[i] * rk_ref[:, sl[i]]) for i in n]

        e_neg = [jnp.exp(-x) for x in c]
        a_t = [(-kk[i]) * jnp.exp(c[i] - lw[i]) for i in n]
        r_t = [r[i] * jnp.exp(c[i]) for i in n]
        b_h = [b_in[i] * e_neg[i] for i in n]
        k_h = [k_in[i] * e_neg[i] for i in n]
        e_end = [jnp.exp(c_end[i] - c[i]) for i in n]
        RE = [cat0([b_in[i] * e_end[i], k_in[i] * e_end[i]]) for i in n]

        L = [cat0([a_t[i], r_t[i]]) for i in n]
        G = [_dot_nt(L[i], cat0([zero_h1(b_h[i]), zero_h1(k_h[i]),
                                 zero_h0(k_h[i]), zero_h0(b_h[i])])) for i in n]
        G0 = [x[:, :C2] for x in G]
        G1 = [x[:, C2:] for x in G]
        g0t = [x[:C] for x in G0]
        g1t = [x[:C] for x in G1]

        n_c = [jnp.where(strict12, jnp.where(left12, g0t[i], g1t[i]), 0.0) for i in n]
        n_bd = [expand(x) for x in n_c]
        d1 = [sum(jnp.where(blk_of_lane == b, x[b * nb:(b + 1) * nb], 0.0)
                  for b in range(C // nb)) for x in n_c]
        d1_bd = [expand16(x) for x in d1]
        d2 = [_dot(d1[i], d1_bd[i]) for i in n]
        d2_bd = [expand16(x) for x in d2]
        d4 = [_dot(d2[i], d2_bd[i]) for i in n]
        d4_bd = [expand16(x) for x in d4]
        d8 = [_dot(d4[i], d4_bd[i]) for i in n]
        pa = [_dot(eye16 + d1[i], eye22 + d2_bd[i]) for i in n]
        pb = [_dot(eye16 + d4[i], eye22 + expand16(d8[i])) for i in n]
        t16 = [_dot(pa[i], expand16(pb[i])) for i in n]
        tcat = [jnp.where(same16_c, cat0([x] * (C // nb)), 0.0) for x in t16]
        for m_e in (m_e1, m_e2):
            if m_e is not None:
                t_bd = [expand(x) for x in tcat]
                x = [_dot(tcat[i], jnp.where(m_e, n_bd[i], 0.0)) for i in n]
                tcat = [tcat[i] + _dot(x[i], t_bd[i]) for i in n]

        v0 = [jnp.where(head0, x, 0.0) for x in v]
        v1 = [jnp.where(head0, 0.0, x) for x in v]
        av = [_dot(jnp.where(strict12, jnp.where(left12, g1t[i], g0t[i]), 0.0),
                   cat0([v1[i], v0[i]])) for i in n]
        bot = [jnp.concatenate([jnp.where(incl12, G0[i][C:], 0.0),
                                jnp.where(incl12, G1[i][C:], 0.0)], axis=1) for i in n]
        return dict(L=L, tcat=tcat, av=av, bot=bot, RE=RE, v=v, v0=v0, v1=v1,
                    decay=[jnp.exp(x) for x in c_end], bonus=bonus_s)

    def state_chain(sc, pre, sel):
        n = range(len(sel))
        at = lambda name: [pre[name][j] for j in sel]
        L, tcat, av, bot, RE, v, v0, v1, decay, bonus_s = (
            at(x) for x in ("L", "tcat", "av", "bot", "RE", "v", "v0", "v1", "decay", "bonus"))
        sl = [slice(p * PAIR, (p + 1) * PAIR) for p in range(n_pairs)]
        S = [s_ref[p] for p in range(n_pairs)]
        P = [_dot_nt(L[i], S[i]) for i in n]
        W = [P[i][:C] + av[i] for i in n]
        U = [_dot(tcat[i], cat0([zero_h1(W[i]), zero_h0(W[i])])) for i in n]
        Y = [P[i][C:] + _dot(bot[i], cat0([zero_h1(U[i]), v0[i], v1[i], zero_h0(U[i])]))
             for i in n]
        for i in n:
            upd = _dot_tn(cat0([U[i], v[i]]), RE[i])
            s_ref[i] = S[i] * decay[i] + jnp.where(ones_bd, upd, 0.0)

        mean = [group_sum(x) * inv_n for x in Y]
        dlt = [Y[i] - mean[i] for i in n]
        var = [group_sum(x * x) * inv_n for x in dlt]
        for i in n:
            yn = dlt[i] * lax.rsqrt(var[i] + GN_EPS) * lng_ref[:, sl[i]] + lnb_ref[:, sl[i]]
            out = (yn + bonus_s[i] * v[i]) * g_ref[trow(sc), sl[i]].astype(F32)
            o_ref[trow(sc), sl[i]] = out.astype(o_ref.dtype)

    items = [(sc, p) for sc in range(n_sub) for p in range(n_pairs)]
    pre = state_free(items)
    for sc in range(n_sub):
        state_chain(sc, pre, [sc * n_pairs + p for p in range(n_pairs)])


def _cast_slab(rows, n_steps):
    share = 1
    while rows * share % (n_steps * BF16_ROWS):
        share *= 2
    return rows * share // n_steps, share


def _wkv(r, lw, k, v, asig, g, s0_bd, kkw, kaw, rk, lng, lnb, cast=()):
    B, T, D = r.shape
    C = _tile(T, WKV_CHUNK)
    n_sub = _tile(T // C, WKV_CHUNKS_PER_STEP)
    n_pairs = D // PAIR
    steps_t = T // (n_sub * C)
    act = pl.BlockSpec((None, n_sub * C, D), lambda b, t: (b, t, 0))
    vec = pl.BlockSpec((1, D), lambda b, t: (0, 0))
    st = pl.BlockSpec((None, n_pairs, PAIR, PAIR), lambda b, t: (b, 0, 0, 0))
    cast_specs = []
    for w in cast:
        slab, share = _cast_slab(w.shape[0], B * steps_t)
        cast_specs.append(pl.BlockSpec(
            (slab, w.shape[1]), lambda b, t, share=share: ((b * steps_t + t) // share, 0)))
    outs = pl.pallas_call(
        functools.partial(_wkv_kernel, chunk=C, n_sub=n_sub, n_pairs=n_pairs,
                          n_cast=len(cast)),
        grid=(B, steps_t),
        in_specs=[act] * 6 + [st] + [vec] * 5 + cast_specs,
        out_specs=[act, st] + cast_specs,
        out_shape=[jax.ShapeDtypeStruct((B, T, D), BF16),
                   jax.ShapeDtypeStruct((B, n_pairs, PAIR, PAIR), F32)]
        + [jax.ShapeDtypeStruct(w.shape, BF16) for w in cast],
        scratch_shapes=[pltpu.VMEM((n_sub * C, D), F32)],
        compiler_params=_params(("arbitrary", "arbitrary")),
        name="wkv",
    )(r, lw, k, v, asig, g, s0_bd, kkw.reshape(1, D), kaw.reshape(1, D), rk.reshape(1, D),
      lng.reshape(1, D), lnb.reshape(1, D), *cast)
    return outs[0], outs[1], outs[2:]


def _state_to_pairs(s):
    B, H, n, _ = s.shape
    s = s.reshape(B, H // 2, 2, n, n)
    z = jnp.zeros((B, H // 2, n, n), s.dtype)
    top = jnp.concatenate([s[:, :, 0], z], axis=-1)
    bot = jnp.concatenate([z, s[:, :, 1]], axis=-1)
    return jnp.concatenate([top, bot], axis=-2)


def _pairs_to_state(sp):
    B, P, _, _ = sp.shape
    n = RW_HEAD
    h0 = sp[:, :, :n, :n]
    h1 = sp[:, :, n:, n:]
    return jnp.stack([h0, h1], axis=2).reshape(B, 2 * P, n, n)


def _ffn_kernel(x_ref, a_ref, wo_ref, conv0_ref, ng_ref, wg_ref, wv_ref, cw_ref, cb_ref,
                wd_ref, og_ref, o_ref, conv_out, xn_s, acc_s, carry_s,
                *, final_norm, nseq, seq_rows):
    t = pl.program_id(1)
    f = pl.program_id(2)
    nf = pl.num_programs(2)
    keep = CONV_W - 1

    @pl.when(f == 0)
    def _():
        x1 = x_ref[...] + jnp.dot(a_ref[...], wo_ref[...], preferred_element_type=F32)
        acc_s[...] = x1
        xn_s[...] = (_rms_scale(x1) * ng_ref[...]).astype(BF16)

    @pl.when(t == 0)
    def _():
        carry_s[f] = conv0_ref[0]

    xn = xn_s[...]
    gate = jnp.dot(xn, wg_ref[...], preferred_element_type=F32)
    val = jnp.dot(xn, wv_ref[...], preferred_element_type=F32)
    prev = carry_s[f]
    rows = _row_ids(gate.shape)
    g1 = jnp.where(rows == 0, prev[1:2, :], pltpu.roll(gate, 1, axis=0))
    g2 = jnp.where(rows == 0, prev[0:1, :],
                   jnp.where(rows == 1, prev[1:2, :], pltpu.roll(gate, 2, axis=0)))
    for s in range(1, nseq):
        first = s * seq_rows
        prev_s = conv0_ref[s]
        g1 = jnp.where(rows == first, prev_s[1:2, :], g1)
        g2 = jnp.where(rows == first, prev_s[0:1, :],
                       jnp.where(rows == first + 1, prev_s[1:2, :], g2))
    c = cb_ref[...] + cw_ref[0:1, :] * g2
    c = c + cw_ref[1:2, :] * g1
    c = c + cw_ref[2:3, :] * gate
    y = (c * jax.nn.sigmoid(c)) * val
    acc_s[...] += jnp.dot(y.astype(BF16), wd_ref[...], preferred_element_type=F32)
    tf = gate.shape[1]
    cols = pl.ds(pl.multiple_of(f * tf, tf), tf)
    for s in range(nseq):
        conv_out[s, :, cols] = gate[(s + 1) * seq_rows - keep:(s + 1) * seq_rows, :]
    carry_s[f] = gate[nseq * seq_rows - keep:nseq * seq_rows, :]

    @pl.when(f == nf - 1)
    def _():
        out = acc_s[...]
        if final_norm:
            out = _rms_scale(out) * og_ref[...]
        o_ref[...] = out


def _ffn(x, a, wo, conv0, ng, wup, cw, cb, wdown, og, *, layer, final_norm):
    B, T, D = x.shape
    F = wdown.shape[1]
    Ka = a.shape[-1]
    G, nseq, tm = _row_tiling(B, T)
    R = B * T // G
    tf = _tile(F, 512)
    nf = F // tf
    keep = CONV_W - 1
    row = lambda b, t, f: (0, 0)
    out, conv = pl.pallas_call(
        functools.partial(_ffn_kernel, final_norm=final_norm, nseq=nseq, seq_rows=tm // nseq),
        grid=(G, R // tm, nf),
        in_specs=[
            pl.BlockSpec((None, tm, D), lambda b, t, f: (b, t, 0)),
            pl.BlockSpec((None, tm, Ka), lambda b, t, f: (b, t, 0)),
            pl.BlockSpec((Ka, D), row, pipeline_mode=pl.Buffered(1)),
            pl.BlockSpec((nseq, keep, tf), lambda b, t, f: (b, 0, f)),
            pl.BlockSpec((1, D), row),
            pl.BlockSpec((None, D, tf), lambda b, t, f: (layer, 0, f)),
            pl.BlockSpec((None, D, tf), lambda b, t, f: (layer, 0, f + nf)),
            pl.BlockSpec((CONV_W, tf), lambda b, t, f: (0, f)),
            pl.BlockSpec((1, tf), lambda b, t, f: (0, f)),
            pl.BlockSpec((None, tf, D), lambda b, t, f: (layer, f, 0)),
            pl.BlockSpec((1, D), row),
        ],
        out_specs=[pl.BlockSpec((None, tm, D), lambda b, t, f: (b, t, 0)),
                   pl.BlockSpec((nseq, keep, F), lambda b, t, f: (b, 0, 0))],
        out_shape=[jax.ShapeDtypeStruct((G, R, D), F32),
                   jax.ShapeDtypeStruct((B, keep, F), F32)],
        scratch_shapes=[pltpu.VMEM((tm, D), BF16), pltpu.VMEM((tm, D), F32),
                        pltpu.VMEM((nf, keep, tf), F32)],
        compiler_params=_params(("arbitrary", "arbitrary", "arbitrary")),
        name="conv_ffn",
    )(x.reshape(G, R, D), a.reshape(G, R, Ka), wo, conv0, ng.reshape(1, D), wup, wup, cw,
      cb.reshape(1, F), wdown, og.reshape(1, D))
    return out.reshape(B, T, D), conv


def _kvq_kernel(x_ref, gkv_ref, gq_ref, wk_ref, wv_ref, wq_ref,
                k_out, v_out, kb_out, vb_out, q_out, xkv_s, xq_s, *, q_scale):
    @pl.when(pl.program_id(1) == 0)
    def _():
        xh = _rms_scale(x_ref[...])
        xkv_s[...] = (xh * gkv_ref[...]).astype(BF16)
        xq_s[...] = (xh * gq_ref[...]).astype(BF16)

    xkv = xkv_s[...]
    k = jnp.dot(xkv, wk_ref[...], preferred_element_type=F32)
    v = jnp.dot(xkv, wv_ref[...], preferred_element_type=F32)
    k_out[...] = k
    v_out[...] = v
    kb_out[...] = k.astype(BF16)
    vb_out[...] = v.astype(BF16)
    q = jnp.dot(xq_s[...], wq_ref[...], preferred_element_type=F32)
    q_out[...] = (q * q_scale).astype(BF16)


def _kvq(x, gkv, gq, wkv, wq):
    M, D = x.shape
    N = wq.shape[1]
    tm = _tile(M, ROW_TILE)
    tn = _tile(N, 512)
    nn = N // tn
    row = lambda i, j: (0, 0)
    blk = pl.BlockSpec((tm, tn), lambda i, j: (i, j))
    return pl.pallas_call(
        functools.partial(_kvq_kernel, q_scale=float(SB_HEAD_DIM) ** -0.5 * LOG2_E),
        grid=(M // tm, nn),
        in_specs=[pl.BlockSpec((tm, D), lambda i, j: (i, 0)),
                  pl.BlockSpec((1, D), row), pl.BlockSpec((1, D), row),
                  pl.BlockSpec((D, tn), lambda i, j: (0, j)),
                  pl.BlockSpec((D, tn), lambda i, j: (0, j + nn)),
                  pl.BlockSpec((D, tn), lambda i, j: (0, j))],
        out_specs=[blk] * 5,
        out_shape=[jax.ShapeDtypeStruct((M, N), F32), jax.ShapeDtypeStruct((M, N), F32),
                   jax.ShapeDtypeStruct((M, N), BF16), jax.ShapeDtypeStruct((M, N), BF16),
                   jax.ShapeDtypeStruct((M, N), BF16)],
        scratch_shapes=[pltpu.VMEM((tm, D), BF16), pltpu.VMEM((tm, D), BF16)],
        compiler_params=_params(("parallel", "arbitrary")),
        name="kvq_proj",
    )(x, gkv.reshape(1, D), gq.reshape(1, D), wkv, wkv, wq)


SB_DEAD = 152.0


def _sb_block(q, kb, vb, tri, carry, valid, on=None):
    z = _dot_nt(q, kb)
    sp = jnp.maximum(z, 0.0) + jnp.log2(1.0 + jnp.exp2(jnp.minimum(z, -z)))
    if valid is not None:
        sp = jnp.where(valid, sp, 0.0)
    if on is not None:
        sp = jnp.where(on, sp, 0.0)
    later = jnp.dot(sp.astype(BF16), tri, preferred_element_type=F32)
    logw = z - sp - later
    if carry is not None:
        logw = logw - carry
    w = jnp.exp2(logw)
    if valid is not None:
        w = jnp.where(valid, w, 0.0)
    if on is not None:
        w = jnp.where(on, w, 0.0)
    out = jnp.dot(w.astype(BF16), vb.astype(BF16), preferred_element_type=F32)
    return out, later[:, 0:1] + sp[:, 0:1]


def _sb_prompt_kernel(q_ref, k_ref, v_ref, tri_ref, o_ref, acc_s, carry_s, *, nsub, tb):
    qi = pl.program_id(2)
    blk0 = qi * nsub
    tri = tri_ref[...]
    valid = _col_ids((tb, tb)) < _row_ids((tb, tb))
    rows = lambda i: slice(i * tb, (i + 1) * tb)

    def kv_block(j):
        k0 = pl.multiple_of(j * tb, tb)
        return k_ref[pl.ds(k0, tb), :], v_ref[pl.ds(k0, tb), :]

    kvs = [kv_block(blk0 + i) for i in range(nsub)]
    acc, car = [], []
    for i in range(nsub):
        out, tot = _sb_block(q_ref[rows(i), :], kvs[i][0], kvs[i][1], tri, None, valid)
        acc.append(out)
        car.append(tot)
    for i in range(nsub):
        if i == 0:
            kb, vb = kv_block(jnp.maximum(blk0 - 1, 0))
            out, tot = _sb_block(q_ref[rows(0), :], kb, vb, tri, car[0], None, on=blk0 > 0)
        else:
            out, tot = _sb_block(q_ref[rows(i), :], kvs[i - 1][0], kvs[i - 1][1], tri,
                                 car[i], None)
        acc_s[i] = acc[i] + out
        carry_s[i] = car[i] + tot

    def live_min(t):
        m = jnp.float32(jnp.inf)
        for i in range(nsub):
            m = jnp.where(blk0 + i - 2 - t >= 0, jnp.minimum(m, jnp.min(carry_s[i])), m)
        return m

    def cond(st):
        t, m = st
        return jnp.logical_and(blk0 + nsub - 3 - t >= 0, m < SB_DEAD)

    def body(st):
        t, _ = st
        for i in range(nsub):
            j = blk0 + i - 2 - t
            kb, vb = kv_block(jnp.maximum(j, 0))
            out, tot = _sb_block(q_ref[rows(i), :], kb, vb, tri, carry_s[i], None, on=j >= 0)
            acc_s[i] += out
            carry_s[i] += tot
        return t + 1, live_min(t + 1)

    lax.while_loop(cond, body, (jnp.int32(0), live_min(0)))
    for i in range(nsub):
        o_ref[rows(i), :] = acc_s[i].astype(o_ref.dtype)


def _sb_decode_kernel(q_ref, k_ref, v_ref, kc_hbm, vc_hbm, trid_ref, trip_ref, o_ref,
                      acc_s, carry_s, near_k, near_v, near_sem, old_k, old_v, old_sem,
                      *, tq, tk, n_heads, n_old):
    b = pl.program_id(0)
    slot = b % 2
    hd = SB_HEAD_DIM
    newest = n_old * tk
    heads = range(n_heads)
    lanes = lambda h: slice(h * hd, (h + 1) * hd)

    def block_copies(row, k0, k_dst, v_dst, sem, first):
        cps = []
        for h in heads:
            cps.append(pltpu.make_async_copy(kc_hbm.at[row, pl.ds(k0, tk), h, :], k_dst.at[h],
                                             sem.at[first]))
            cps.append(pltpu.make_async_copy(vc_hbm.at[row, pl.ds(k0, tk), h, :], v_dst.at[h],
                                             sem.at[first + 1]))
        return cps

    @pl.when(b == 0)
    def _():
        for c in block_copies(0, newest, near_k.at[0], near_v.at[0], near_sem, 0):
            c.start()

    @pl.when(b + 1 < pl.num_programs(0))
    def _():
        for c in block_copies(b + 1, newest, near_k.at[1 - slot], near_v.at[1 - slot],
                              near_sem, 2 * (1 - slot)):
            c.start()

    valid = _col_ids((tq, tq)) < _row_ids((tq, tq))
    trid = trid_ref[...]
    trip = trip_ref[...]
    q = [q_ref[:, lanes(h)] for h in heads]
    own = [_sb_block(q[h], k_ref[:, lanes(h)], v_ref[:, lanes(h)], trid, None, valid)
           for h in heads]
    for c in block_copies(b, newest, near_k.at[slot], near_v.at[slot], near_sem, 2 * slot):
        c.wait()
    for h in heads:
        acc, carry = own[h]
        out, tot = _sb_block(q[h], near_k[slot, h], near_v[slot, h], trip, carry, None)
        acc_s[h] = acc + out
        carry_s[h] = carry + tot

    def cond(st):
        t, m = st
        return jnp.logical_and(t < n_old, m < SB_DEAD)

    def body(st):
        t, _ = st
        k0 = pl.multiple_of((n_old - 1 - t) * tk, tk)
        copies = block_copies(b, k0, old_k, old_v, old_sem, 0)
        for c in copies:
            c.start()
        for c in copies:
            c.wait()
        for h in heads:
            out, tot = _sb_block(q_ref[:, lanes(h)], old_k[h], old_v[h], trip, carry_s[h], None)
            acc_s[h] += out
            carry_s[h] += tot
        return t + 1, jnp.min(carry_s[...])

    lax.while_loop(cond, body, (jnp.int32(0), jnp.min(carry_s[...])))
    for h in heads:
        o_ref[:, lanes(h)] = acc_s[h].astype(o_ref.dtype)


def _suffix_ones(n):
    return (_row_ids((n, n)) > _col_ids((n, n))).astype(BF16)


SB_BLOCK = 256
SB_QSUB = 4


def _sb_attention(q, k, v, k_past=None, v_past=None):
    B, T, D = q.shape
    H = D // SB_HEAD_DIM
    hd = SB_HEAD_DIM
    full = lambda n: pl.BlockSpec((n, n), lambda b, h, i: (0, 0))
    seq = pl.BlockSpec((None, T, hd), lambda b, h, i: (b, 0, h))
    if k_past is None:
        tb = _tile(T, SB_BLOCK)
        nsub = _tile(T // tb, SB_QSUB)
        tq = nsub * tb
        qspec = pl.BlockSpec((None, tq, hd), lambda b, h, i: (b, i, h))
        return pl.pallas_call(
            functools.partial(_sb_prompt_kernel, nsub=nsub, tb=tb),
            grid=(B, H, T // tq),
            in_specs=[qspec, seq, seq, full(tb)],
            out_specs=qspec,
            out_shape=jax.ShapeDtypeStruct((B, T, D), BF16),
            scratch_shapes=[pltpu.VMEM((nsub, tb, hd), F32), pltpu.VMEM((nsub, tb, 1), F32)],
            compiler_params=_params(("parallel", "parallel", "arbitrary")),
            name="sb_attention",
        )(q, k, v, _suffix_ones(tb))
    P = k_past.shape[1]
    tk = _tile(P, SB_BLOCK)
    row = pl.BlockSpec((None, T, D), lambda b: (b, 0, 0))
    tri = lambda n: pl.BlockSpec((n, n), lambda b: (0, 0))
    hbm = pl.BlockSpec(memory_space=pl.ANY)
    return pl.pallas_call(
        functools.partial(_sb_decode_kernel, tq=T, tk=tk, n_heads=H, n_old=P // tk - 1),
        grid=(B,),
        in_specs=[row, row, row, hbm, hbm, tri(T), tri(tk)],
        out_specs=row,
        out_shape=jax.ShapeDtypeStruct((B, T, D), BF16),
        scratch_shapes=[pltpu.VMEM((H, T, hd), F32), pltpu.VMEM((H, T, 1), F32),
                        pltpu.VMEM((2, H, tk, hd), F32), pltpu.VMEM((2, H, tk, hd), F32),
                        pltpu.SemaphoreType.DMA((4,)),
                        pltpu.VMEM((H, tk, hd), F32), pltpu.VMEM((H, tk, hd), F32),
                        pltpu.SemaphoreType.DMA((2,))],
        compiler_params=_params(("arbitrary",)),
        name="sb_attention_decode",
    )(q, k, v, k_past, v_past, _suffix_ones(T), _suffix_ones(tk))


LATE_WEIGHTS = ('rw_wo', 'f_wup', 'f_wdown', 'w_kv', 'sb_wq', 'sb_wo')


def _trunk(x, shift0, wkv0, conv0, k_past, v_past, p, late_f32=None):
    B, T, D = x.shape
    M = B * T
    r, k, v, lw, asig, g, shift = _rw_proj(
        x, shift0[0], p['a_norm_g'][0], p['rw_mu'][0], p['rw_w0'][0], p['rw_a0'][0],
        p['rw_wr'], p['rw_wk'], p['rw_wv'], p['rw_w1'], p['rw_w2'], p['rw_a1'], p['rw_a2'],
        p['rw_g1'], p['rw_g2'])
    cast = () if late_f32 is None else tuple(
        late_f32[n].reshape(-1, late_f32[n].shape[-1]) for n in LATE_WEIGHTS)
    o, s_bd, casted = _wkv(r, lw, k, v, asig, g, _state_to_pairs(wkv0[0]),
                           p['rw_kk'][0], p['rw_ka'][0], p['rw_rk'][0].reshape(-1),
                           p['rw_lnx_g'][0], p['rw_lnx_b'][0], cast=cast)
    if late_f32 is not None:
        for n, w in zip(LATE_WEIGHTS, casted):
            p[n] = w.reshape(late_f32[n].shape)
    x, conv_a = _ffn(x, o, p['rw_wo'], conv0[0], p['f_norm_g'][0], p['f_wup'],
                     p['f_conv_w'][0], p['f_conv_b'][0], p['f_wdown'], p['out_norm_g'],
                     layer=0, final_norm=False)
    k_sh, v_sh, kb, vb, q = _kvq(x.reshape(M, D), p['kv_norm_g'], p['b_norm_g'][0],
                                 p['w_kv'], p['sb_wq'])
    d_att = q.shape[1]
    att = _sb_attention(q.reshape(B, T, d_att), kb.reshape(B, T, d_att), vb.reshape(B, T, d_att),
                        k_past, v_past)
    y, conv_b = _ffn(x, att, p['sb_wo'], conv0[1], p['f_norm_g'][1], p['f_wup'],
                     p['f_conv_w'][1], p['f_conv_b'][1], p['f_wdown'], p['out_norm_g'],
                     layer=1, final_norm=True)
    H = d_att // SB_HEAD_DIM
    return (y, _pairs_to_state(s_bd)[None], shift.reshape(1, B, D), jnp.stack([conv_a, conv_b]),
            k_sh.reshape(B, T, H, SB_HEAD_DIM), v_sh.reshape(B, T, H, SB_HEAD_DIM))


def kernel(x_prompt, x_sample, cache_k, cache_v, state_wkv, state_shift, state_conv, a_norm_g, rw_mu, rw_w0, rw_w1, rw_w2, rw_a0, rw_a1, rw_a2, rw_g1, rw_g2, rw_kk, rw_ka, rw_rk, rw_wr, rw_wk, rw_wv, rw_wo, rw_lnx_g, rw_lnx_b, kv_norm_g, w_kv, b_norm_g, sb_wq, sb_wo, f_norm_g, f_wup, f_conv_w, f_conv_b, f_wdown, out_norm_g):
    bf = lambda w: w.astype(BF16)
    p = dict(a_norm_g=a_norm_g, rw_mu=rw_mu, rw_w0=rw_w0, rw_a0=rw_a0,
             rw_w1=bf(rw_w1[0]), rw_w2=bf(rw_w2[0]), rw_a1=bf(rw_a1[0]), rw_a2=bf(rw_a2[0]),
             rw_g1=bf(rw_g1[0]), rw_g2=bf(rw_g2[0]),
             rw_kk=rw_kk, rw_ka=rw_ka, rw_rk=rw_rk,
             rw_wr=bf(rw_wr[0]), rw_wk=bf(rw_wk[0]), rw_wv=bf(rw_wv[0]),
             rw_lnx_g=rw_lnx_g, rw_lnx_b=rw_lnx_b, kv_norm_g=kv_norm_g,
             b_norm_g=b_norm_g, f_norm_g=f_norm_g,
             f_conv_w=f_conv_w, f_conv_b=f_conv_b, out_norm_g=out_norm_g)
    late_f32 = dict(rw_wo=rw_wo[0], f_wup=f_wup, f_wdown=f_wdown, w_kv=w_kv,
                    sb_wq=sb_wq[0], sb_wo=sb_wo[0])
    B, _, D = x_prompt.shape
    n_a = state_shift.shape[0]
    depth = state_conv.shape[0]
    F = state_conv.shape[-1]
    H = state_wkv.shape[2]
    shift0 = jnp.zeros((n_a, B, D), x_prompt.dtype)
    wkv0 = jnp.zeros((n_a, B, H, RW_HEAD, RW_HEAD), F32)
    conv0 = jnp.zeros((depth, B, CONV_W - 1, F), x_prompt.dtype)
    y_p, wkv_p, shift_p, conv_p, k_p, v_p = _trunk(x_prompt, shift0, wkv0, conv0, None, None, p,
                                                   late_f32=late_f32)
    y_s, wkv_s, shift_s, conv_s, k_s, v_s = _trunk(
        x_sample, state_shift, state_wkv.astype(F32), state_conv, cache_k, cache_v, p)
    return (y_p, y_s, wkv_p.astype(state_wkv.dtype), shift_p, conv_p, k_p, v_p,
            wkv_s.astype(state_wkv.dtype), shift_s, conv_s, k_s, v_s)
```

```python
import functools

import jax
import jax.numpy as jnp
from jax import lax
from jax.experimental import pallas as pl
from jax.experimental.pallas import tpu as pltpu

F32 = jnp.float32
BF16 = jnp.bfloat16

RW_HEAD = 64
PAIR = 2 * RW_HEAD
SB_HEAD_DIM = 128
GN_EPS = 64e-5
NORM_EPS = 1e-6
CONV_W = 3
RW_PROJ_DTYPES = (BF16, BF16, BF16, F32, BF16, BF16)
BF16_ROWS = 16
WKV_CHUNK = 64
NEUMANN_BLOCK = 16
WKV_CHUNKS_PER_STEP = 2
LOG2_E = 1.4426950408889634
ROW_TILE = 512
COL_TILE = 512
VMEM_LIMIT = 56 * 1024 * 1024


def _tile(n, pref):
    if n <= pref:
        return n
    t = pref
    while n % t:
        t //= 2
    return t


def _row_tiling(B, T):
    if T >= ROW_TILE:
        return B, 1, _tile(T, ROW_TILE)
    nseq = _tile(B, max(ROW_TILE // T, 1))
    return B // nseq, nseq, nseq * T


def _params(sem):
    return pltpu.CompilerParams(dimension_semantics=sem, vmem_limit_bytes=VMEM_LIMIT)


def _dot(a, b):
    return jnp.dot(a.astype(BF16), b.astype(BF16), preferred_element_type=F32)


def _dot_nt(a, b):
    return lax.dot_general(a.astype(BF16), b.astype(BF16), (((1,), (1,)), ((), ())),
                           preferred_element_type=F32)


def _dot_tn(a, b):
    return lax.dot_general(a.astype(BF16), b.astype(BF16), (((0,), (0,)), ((), ())),
                           preferred_element_type=F32)


def _rms_scale(x):
    return x * lax.rsqrt(jnp.mean(x * x, axis=-1, keepdims=True) + NORM_EPS)


def _softplus(u):
    return jnp.maximum(u, 0.0) + jnp.log1p(jnp.exp(-jnp.abs(u)))


def _row_ids(shape):
    return lax.broadcasted_iota(jnp.int32, shape, 0)


def _col_ids(shape):
    return lax.broadcasted_iota(jnp.int32, shape, 1)


def _rw_proj_kernel(x_ref, shift0_ref, ng_ref, mu_ref, w0_ref, a0_ref,
                    wr_ref, wk_ref, wv_ref, w1_ref, w2_ref, a1_ref, a2_ref, g1_ref, g2_ref,
                    r_out, k_out, v_out, lw_out, as_out, g_out, shift_out,
                    xr_s, xk_s, xv_s, hw_s, ha_s, hg_s, carry_s, *, nseq, seq_rows):
    t = pl.program_id(1)
    j = pl.program_id(2)

    @pl.when(j == 0)
    def _():
        @pl.when(t == 0)
        def _():
            carry_s[...] = shift0_ref[0]

        xn = _rms_scale(x_ref[...]) * ng_ref[...]
        rows = _row_ids(xn.shape)
        x_prev = jnp.where(rows == 0, carry_s[...], pltpu.roll(xn, 1, axis=0))
        for s in range(1, nseq):
            x_prev = jnp.where(rows == s * seq_rows, shift0_ref[s], x_prev)
        for s in range(nseq):
            shift_out[s] = xn[(s + 1) * seq_rows - 1:(s + 1) * seq_rows, :]
        carry_s[...] = xn[nseq * seq_rows - 1:nseq * seq_rows, :]
        xx = x_prev - xn
        mix = lambda i: (xn + xx * mu_ref[i:i + 1, :]).astype(BF16)
        xr_s[...] = mix(0)
        xk_s[...] = mix(2)
        xv_s[...] = mix(3)
        hw_s[...] = jnp.tanh(_dot(mix(1), w1_ref[...])).astype(BF16)
        ha_s[...] = _dot(mix(4), a1_ref[...]).astype(BF16)
        hg_s[...] = jax.nn.sigmoid(_dot(mix(5), g1_ref[...])).astype(BF16)

    r_out[...] = _dot(xr_s[...], wr_ref[...]).astype(r_out.dtype)
    k_out[...] = _dot(xk_s[...], wk_ref[...]).astype(k_out.dtype)
    v_out[...] = _dot(xv_s[...], wv_ref[...]).astype(v_out.dtype)
    w_log = -_softplus(-(w0_ref[...] + _dot(hw_s[...], w2_ref[...]))) - 0.5
    lw_out[...] = -jnp.exp(w_log)
    as_out[...] = jax.nn.sigmoid(a0_ref[...] + _dot(ha_s[...], a2_ref[...])).astype(as_out.dtype)
    g_out[...] = _dot(hg_s[...], g2_ref[...]).astype(g_out.dtype)


def _rw_proj(x, shift0, ng, mu, w0, a0, wr, wk, wv, w1, w2, a1, a2, g1, g2):
    B, T, D = x.shape
    G, nseq, tm = _row_tiling(B, T)
    R = B * T // G
    tn = _tile(D, COL_TILE)
    lw_dim, la_dim, lg_dim = w1.shape[1], a1.shape[1], g1.shape[1]
    row = lambda b, t, j: (0, 0)
    colv = pl.BlockSpec((1, tn), lambda b, t, j: (0, j))
    act = pl.BlockSpec((None, tm, tn), lambda b, t, j: (b, t, j))
    outs = pl.pallas_call(
        functools.partial(_rw_proj_kernel, nseq=nseq, seq_rows=tm // nseq),
        grid=(G, R // tm, D // tn),
        in_specs=[
            pl.BlockSpec((None, tm, D), lambda b, t, j: (b, t, 0)),
            pl.BlockSpec((nseq, 1, D), lambda b, t, j: (b, 0, 0)),
            pl.BlockSpec((1, D), row),
            pl.BlockSpec((6, D), row),
            colv, colv,
            pl.BlockSpec((D, tn), lambda b, t, j: (0, j)),
            pl.BlockSpec((D, tn), lambda b, t, j: (0, j)),
            pl.BlockSpec((D, tn), lambda b, t, j: (0, j)),
            pl.BlockSpec((D, lw_dim), row),
            pl.BlockSpec((lw_dim, tn), lambda b, t, j: (0, j)),
            pl.BlockSpec((D, la_dim), row),
            pl.BlockSpec((la_dim, tn), lambda b, t, j: (0, j)),
            pl.BlockSpec((D, lg_dim), row),
            pl.BlockSpec((lg_dim, tn), lambda b, t, j: (0, j)),
        ],
        out_specs=[act] * 6 + [pl.BlockSpec((nseq, 1, D), lambda b, t, j: (b, 0, 0))],
        out_shape=[jax.ShapeDtypeStruct((G, R, D), dt) for dt in RW_PROJ_DTYPES]
        + [jax.ShapeDtypeStruct((B, 1, D), F32)],
        scratch_shapes=[
            pltpu.VMEM((tm, D), BF16), pltpu.VMEM((tm, D), BF16), pltpu.VMEM((tm, D), BF16),
            pltpu.VMEM((tm, lw_dim), BF16), pltpu.VMEM((tm, la_dim), BF16),
            pltpu.VMEM((tm, lg_dim), BF16), pltpu.VMEM((1, D), F32),
        ],
        compiler_params=_params(("arbitrary", "arbitrary", "arbitrary")),
        name="rw_proj",
    )(x.reshape(G, R, D), shift0.reshape(B, 1, D), ng.reshape(1, D), mu, w0.reshape(1, D),
      a0.reshape(1, D), wr, wk, wv, w1, w2, a1, a2, g1, g2)
    return [o.reshape(B, T, D) for o in outs[:6]] + [outs[6]]


def _wkv_kernel(r_ref, lw_ref, k_ref, v_ref, as_ref, g_ref, s0_ref,
                kkw_ref, kaw_ref, rk_ref, lng_ref, lnb_ref, *rest,
                chunk, n_sub, n_pairs, n_cast):
    cast_in = rest[:n_cast]
    o_ref, s_ref = rest[n_cast:n_cast + 2]
    cast_out = rest[n_cast + 2:2 * n_cast + 2]
    c_s = rest[-1]
    C = chunk
    C2 = 2 * C
    trow = lambda sc: slice(sc * C, (sc + 1) * C)

    for w_in, w_out in zip(cast_in, cast_out):
        w_out[...] = w_in[...].astype(w_out.dtype)

    @pl.when(pl.program_id(1) == 0)
    def _():
        s_ref[...] = s0_ref[...]

    tri = (_col_ids((C, C)) <= _row_ids((C, C))).astype(BF16)
    for sc in range(n_sub):
        lw_all = lw_ref[trow(sc), :]
        lw_hi = lw_all.astype(BF16)
        lw_r = lw_all - lw_hi.astype(F32)
        lw_mid = lw_r.astype(BF16)
        lw_lo = (lw_r - lw_mid.astype(F32)).astype(BF16)
        c_s[trow(sc), :] = (jnp.dot(tri, lw_hi, preferred_element_type=F32)
                            + jnp.dot(tri, lw_mid, preferred_element_type=F32)
                            + jnp.dot(tri, lw_lo, preferred_element_type=F32))

    head0 = _col_ids((C, PAIR)) < RW_HEAD
    ones_bd = ((_row_ids((PAIR, PAIR)) // RW_HEAD) == (_col_ids((PAIR, PAIR)) // RW_HEAD))
    ones_bd_bf = ones_bd.astype(BF16)

    rr = _row_ids((C2, C2))
    cc = _col_ids((C2, C2))
    same_head = (rr // C) == (cc // C)
    nb = NEUMANN_BLOCK
    same16 = (rr // nb) == (cc // nb)
    same32 = (rr // (2 * nb)) == (cc // (2 * nb))
    m_e1 = same32 & (~same16) if C >= 2 * nb else None
    m_e2 = same_head & (~same32) if C >= 4 * nb else None
    eye22 = (rr == cc).astype(F32)
    r12 = _row_ids((C, C2))
    c12 = _col_ids((C, C2))
    strict12 = (c12 % C) < r12
    incl12 = (c12 % C) <= r12
    left12 = c12 < C
    same16_c = ((c12 % C) // nb) == (r12 // nb)
    c16 = _col_ids((nb, C2))
    blk_of_lane = (c16 % C) // nb
    eye16 = ((c16 % nb) == _row_ids((nb, C2))).astype(F32)

    def expand(x):
        return jnp.where(same_head, jnp.concatenate([x, x], axis=0), 0.0)

    def expand16(x):
        return jnp.where(same16, jnp.concatenate([x] * (C2 // nb), axis=0), 0.0)

    def group_sum(x):
        return jnp.dot(x.astype(BF16), ones_bd_bf, preferred_element_type=F32)

    cat0 = lambda xs: jnp.concatenate(xs, axis=0)
    zero_h1 = lambda x: jnp.where(head0, x, 0.0)
    zero_h0 = lambda x: jnp.where(head0, 0.0, x)
    inv_n = 1.0 / RW_HEAD

    def state_free(items):
        n = range(len(items))
        tr = [trow(sc) for sc, _ in items]
        sl = [slice(p * PAIR, (p + 1) * PAIR) for _, p in items]
        r = [r_ref[tr[i], sl[i]].astype(F32) for i in n]
        lw = [lw_ref[tr[i], sl[i]] for i in n]
        k = [k_ref[tr[i], sl[i]].astype(F32) for i in n]
        v = [v_ref[tr[i], sl[i]].astype(F32) for i in n]
        asig = [as_ref[tr[i], sl[i]].astype(F32) for i in n]
        c = [c_s[tr[i], sl[i]] for i in n]
        c_end = [x[C - 1:C, :] for x in c]

        kkv = [k[i] * kkw_ref[:, sl[i]] for i in n]
        ss = [group_sum(x * x) for x in kkv]
        kk = [kkv[i] / jnp.maximum(jnp.sqrt(ss[i]), 1e-12) for i in n]
        b_in = [kk[i] * asig[i] for i in n]
        k_in = [k[i] * (1.0 + (asig[i] - 1.0) * kaw_ref[:, sl[i]]) for i in n]
        bonus_s = [group_sum(r[i] * k_in[i] * rk_ref[:, sl[i]]) for i in n]

        e_neg = [jnp.exp(-x) for x in c]
        a_t = [(-kk[i]) * jnp.exp(c[i] - lw[i]) for i in n]
        r_t = [r[i] * jnp.exp(c[i]) for i in n]
        b_h = [b_in[i] * e_neg[i] for i in n]
        k_h = [k_in[i] * e_neg[i] for i in n]
        e_end = [jnp.exp(c_end[i] - c[i]) for i in n]
        RE = [cat0([b_in[i] * e_end[i], k_in[i] * e_end[i]]) for i in n]

        L = [cat0([a_t[i], r_t[i]]) for i in n]
        G = [_dot_nt(L[i], cat0([zero_h1(b_h[i]), zero_h1(k_h[i]),
                                 zero_h0(k_h[i]), zero_h0(b_h[i])])) for i in n]
        G0 = [x[:, :C2] for x in G]
        G1 = [x[:, C2:] for x in G]
        g0t = [x[:C] for x in G0]
        g1t = [x[:C] for x in G1]

        n_c = [jnp.where(strict12, jnp.where(left12, g0t[i], g1t[i]), 0.0) for i in n]
        n_bd = [expand(x) for x in n_c]
        d1 = [sum(jnp.where(blk_of_lane == b, x[b * nb:(b + 1) * nb], 0.0)
                  for b in range(C // nb)) for x in n_c]
        d1_bd = [expand16(x) for x in d1]
        d2 = [_dot(d1[i], d1_bd[i]) for i in n]
        d2_bd = [expand16(x) for x in d2]
        d4 = [_dot(d2[i], d2_bd[i]) for i in n]
        d4_bd = [expand16(x) for x in d4]
        d8 = [_dot(d4[i], d4_bd[i]) for i in n]
        pa = [_dot(eye16 + d1[i], eye22 + d2_bd[i]) for i in n]
        pb = [_dot(eye16 + d4[i], eye22 + expand16(d8[i])) for i in n]
        t16 = [_dot(pa[i], expand16(pb[i])) for i in n]
        tcat = [jnp.where(same16_c, cat0([x] * (C // nb)), 0.0) for x in t16]
        for m_e in (m_e1, m_e2):
            if m_e is not None:
                t_bd = [expand(x) for x in tcat]
                x = [_dot(tcat[i], jnp.where(m_e, n_bd[i], 0.0)) for i in n]
                tcat = [tcat[i] + _dot(x[i], t_bd[i]) for i in n]

        v0 = [jnp.where(head0, x, 0.0) for x in v]
        v1 = [jnp.where(head0, 0.0, x) for x in v]
        av = [_dot(jnp.where(strict12, jnp.where(left12, g1t[i], g0t[i]), 0.0),
                   cat0([v1[i], v0[i]])) for i in n]
        bot = [jnp.concatenate([jnp.where(incl12, G0[i][C:], 0.0),
                                jnp.where(incl12, G1[i][C:], 0.0)], axis=1) for i in n]
        return dict(L=L, tcat=tcat, av=av, bot=bot, RE=RE, v=v, v0=v0, v1=v1,
                    decay=[jnp.exp(x) for x in c_end], bonus=bonus_s)

    def state_chain(sc, pre, sel):
        n = range(len(sel))
        at = lambda name: [pre[name][j] for j in sel]
        L, tcat, av, bot, RE, v, v0, v1, decay, bonus_s = (
            at(x) for x in ("L", "tcat", "av", "bot", "RE", "v", "v0", "v1", "decay", "bonus"))
        sl = [slice(p * PAIR, (p + 1) * PAIR) for p in range(n_pairs)]
        S = [s_ref[p] for p in range(n_pairs)]
        P = [_dot_nt(L[i], S[i]) for i in n]
        W = [P[i][:C] + av[i] for i in n]
        U = [_dot(tcat[i], cat0([zero_h1(W[i]), zero_h0(W[i])])) for i in n]
        Y = [P[i][C:] + _dot(bot[i], cat0([zero_h1(U[i]), v0[i], v1[i], zero_h0(U[i])]))
             for i in n]
        for i in n:
            upd = _dot_tn(cat0([U[i], v[i]]), RE[i])
            s_ref[i] = S[i] * decay[i] + jnp.where(ones_bd, upd, 0.0)

        mean = [group_sum(x) * inv_n for x in Y]
        dlt = [Y[i] - mean[i] for i in n]
        var = [group_sum(x * x) * inv_n for x in dlt]
        for i in n:
            yn = dlt[i] * lax.rsqrt(var[i] + GN_EPS) * lng_ref[:, sl[i]] + lnb_ref[:, sl[i]]
            out = (yn + bonus_s[i] * v[i]) * g_ref[trow(sc), sl[i]].astype(F32)
            o_ref[trow(sc), sl[i]] = out.astype(o_ref.dtype)

    items = [(sc, p) for sc in range(n_sub) for p in range(n_pairs)]
    pre = state_free(items)
    for sc in range(n_sub):
        state_chain(sc, pre, [sc * n_pairs + p for p in range(n_pairs)])


def _cast_slab(rows, n_steps):
    share = 1
    while rows * share % (n_steps * BF16_ROWS):
        share *= 2
    return rows * share // n_steps, share


def _wkv(r, lw, k, v, asig, g, s0_bd, kkw, kaw, rk, lng, lnb, cast=()):
    B, T, D = r.shape
    C = _tile(T, WKV_CHUNK)
    n_sub = _tile(T // C, WKV_CHUNKS_PER_STEP)
    n_pairs = D // PAIR
    steps_t = T // (n_sub * C)
    act = pl.BlockSpec((None, n_sub * C, D), lambda b, t: (b, t, 0))
    vec = pl.BlockSpec((1, D), lambda b, t: (0, 0))
    st = pl.BlockSpec((None, n_pairs, PAIR, PAIR), lambda b, t: (b, 0, 0, 0))
    cast_specs = []
    for w in cast:
        slab, share = _cast_slab(w.shape[0], B * steps_t)
        cast_specs.append(pl.BlockSpec(
            (slab, w.shape[1]), lambda b, t, share=share: ((b * steps_t + t) // share, 0)))
    outs = pl.pallas_call(
        functools.partial(_wkv_kernel, chunk=C, n_sub=n_sub, n_pairs=n_pairs,
                          n_cast=len(cast)),
        grid=(B, steps_t),
        in_specs=[act] * 6 + [st] + [vec] * 5 + cast_specs,
        out_specs=[act, st] + cast_specs,
        out_shape=[jax.ShapeDtypeStruct((B, T, D), BF16),
                   jax.ShapeDtypeStruct((B, n_pairs, PAIR, PAIR), F32)]
        + [jax.ShapeDtypeStruct(w.shape, BF16) for w in cast],
        scratch_shapes=[pltpu.VMEM((n_sub * C, D), F32)],
        compiler_params=_params(("arbitrary", "arbitrary")),
        name="wkv",
    )(r, lw, k, v, asig, g, s0_bd, kkw.reshape(1, D), kaw.reshape(1, D), rk.reshape(1, D),
      lng.reshape(1, D), lnb.reshape(1, D), *cast)
    return outs[0], outs[1], outs[2:]


def _state_to_pairs(s):
    B, H, n, _ = s.shape
    s = s.reshape(B, H // 2, 2, n, n)
    z = jnp.zeros((B, H // 2, n, n), s.dtype)
    top = jnp.concatenate([s[:, :, 0], z], axis=-1)
    bot = jnp.concatenate([z, s[:, :, 1]], axis=-1)
    return jnp.concatenate([top, bot], axis=-2)


def _pairs_to_state(sp):
    B, P, _, _ = sp.shape
    n = RW_HEAD
    h0 = sp[:, :, :n, :n]
    h1 = sp[:, :, n:, n:]
    return jnp.stack([h0, h1], axis=2).reshape(B, 2 * P, n, n)


def _ffn_kernel(x_ref, a_ref, wo_ref, conv0_ref, ng_ref, wg_ref, wv_ref, cw_ref, cb_ref,
                wd_ref, og_ref, o_ref, conv_out, xn_s, acc_s, carry_s,
                *, final_norm, nseq, seq_rows):
    t = pl.program_id(1)
    f = pl.program_id(2)
    nf = pl.num_programs(2)
    keep = CONV_W - 1

    @pl.when(f == 0)
    def _():
        x1 = x_ref[...] + jnp.dot(a_ref[...], wo_ref[...], preferred_element_type=F32)
        acc_s[...] = x1
        xn_s[...] = (_rms_scale(x1) * ng_ref[...]).astype(BF16)

    @pl.when(t == 0)
    def _():
        carry_s[f] = conv0_ref[0]

    xn = xn_s[...]
    gate = jnp.dot(xn, wg_ref[...], preferred_element_type=F32)
    val = jnp.dot(xn, wv_ref[...], preferred_element_type=F32)
    prev = carry_s[f]
    rows = _row_ids(gate.shape)
    g1 = jnp.where(rows == 0, prev[1:2, :], pltpu.roll(gate, 1, axis=0))
    g2 = jnp.where(rows == 0, prev[0:1, :],
                   jnp.where(rows == 1, prev[1:2, :], pltpu.roll(gate, 2, axis=0)))
    for s in range(1, nseq):
        first = s * seq_rows
        prev_s = conv0_ref[s]
        g1 = jnp.where(rows == first, prev_s[1:2, :], g1)
        g2 = jnp.where(rows == first, prev_s[0:1, :],
                       jnp.where(rows == first + 1, prev_s[1:2, :], g2))
    c = cb_ref[...] + cw_ref[0:1, :] * g2
    c = c + cw_ref[1:2, :] * g1
    c = c + cw_ref[2:3, :] * gate
    y = (c * jax.nn.sigmoid(c)) * val
    acc_s[...] += jnp.dot(y.astype(BF16), wd_ref[...], preferred_element_type=F32)
    tf = gate.shape[1]
    cols = pl.ds(pl.multiple_of(f * tf, tf), tf)
    for s in range(nseq):
        conv_out[s, :, cols] = gate[(s + 1) * seq_rows - keep:(s + 1) * seq_rows, :]
    carry_s[f] = gate[nseq * seq_rows - keep:nseq * seq_rows, :]

    @pl.when(f == nf - 1)
    def _():
        out = acc_s[...]
        if final_norm:
            out = _rms_scale(out) * og_ref[...]
        o_ref[...] = out


def _ffn(x, a, wo, conv0, ng, wup, cw, cb, wdown, og, *, layer, final_norm):
    B, T, D = x.shape
    F = wdown.shape[1]
    Ka = a.shape[-1]
    G, nseq, tm = _row_tiling(B, T)
    R = B * T // G
    tf = _tile(F, COL_TILE)
    nf = F // tf
    keep = CONV_W - 1
    row = lambda b, t, f: (0, 0)
    out, conv = pl.pallas_call(
        functools.partial(_ffn_kernel, final_norm=final_norm, nseq=nseq, seq_rows=tm // nseq),
        grid=(G, R // tm, nf),
        in_specs=[
            pl.BlockSpec((None, tm, D), lambda b, t, f: (b, t, 0)),
            pl.BlockSpec((None, tm, Ka), lambda b, t, f: (b, t, 0)),
            pl.BlockSpec((Ka, D), row, pipeline_mode=pl.Buffered(1)),
            pl.BlockSpec((nseq, keep, tf), lambda b, t, f: (b, 0, f)),
            pl.BlockSpec((1, D), row),
            pl.BlockSpec((None, D, tf), lambda b, t, f: (layer, 0, f)),
            pl.BlockSpec((None, D, tf), lambda b, t, f: (layer, 0, f + nf)),
            pl.BlockSpec((CONV_W, tf), lambda b, t, f: (0, f)),
            pl.BlockSpec((1, tf), lambda b, t, f: (0, f)),
            pl.BlockSpec((None, tf, D), lambda b, t, f: (layer, f, 0)),
            pl.BlockSpec((1, D), row),
        ],
        out_specs=[pl.BlockSpec((None, tm, D), lambda b, t, f: (b, t, 0)),
                   pl.BlockSpec((nseq, keep, F), lambda b, t, f: (b, 0, 0))],
        out_shape=[jax.ShapeDtypeStruct((G, R, D), F32),
                   jax.ShapeDtypeStruct((B, keep, F), F32)],
        scratch_shapes=[pltpu.VMEM((tm, D), BF16), pltpu.VMEM((tm, D), F32),
                        pltpu.VMEM((nf, keep, tf), F32)],
        compiler_params=_params(("arbitrary", "arbitrary", "arbitrary")),
        name="conv_ffn",
    )(x.reshape(G, R, D), a.reshape(G, R, Ka), wo, conv0, ng.reshape(1, D), wup, wup, cw,
      cb.reshape(1, F), wdown, og.reshape(1, D))
    return out.reshape(B, T, D), conv


def _kvq_kernel(x_ref, gkv_ref, gq_ref, wk_ref, wv_ref, wq_ref,
                k_out, v_out, kb_out, vb_out, q_out, xkv_s, xq_s, *, q_scale):
    @pl.when(pl.program_id(1) == 0)
    def _():
        xh = _rms_scale(x_ref[...])
        xkv_s[...] = (xh * gkv_ref[...]).astype(BF16)
        xq_s[...] = (xh * gq_ref[...]).astype(BF16)

    xkv = xkv_s[...]
    k = jnp.dot(xkv, wk_ref[...], preferred_element_type=F32)
    v = jnp.dot(xkv, wv_ref[...], preferred_element_type=F32)
    k_out[...] = k
    v_out[...] = v
    kb_out[...] = k.astype(BF16)
    vb_out[...] = v.astype(BF16)
    q = jnp.dot(xq_s[...], wq_ref[...], preferred_element_type=F32)
    q_out[...] = (q * q_scale).astype(BF16)


def _kvq(x, gkv, gq, wkv, wq):
    M, D = x.shape
    N = wq.shape[1]
    tm = _tile(M, ROW_TILE)
    tn = _tile(N, COL_TILE)
    nn = N // tn
    row = lambda i, j: (0, 0)
    blk = pl.BlockSpec((tm, tn), lambda i, j: (i, j))
    return pl.pallas_call(
        functools.partial(_kvq_kernel, q_scale=float(SB_HEAD_DIM) ** -0.5 * LOG2_E),
        grid=(M // tm, nn),
        in_specs=[pl.BlockSpec((tm, D), lambda i, j: (i, 0)),
                  pl.BlockSpec((1, D), row), pl.BlockSpec((1, D), row),
                  pl.BlockSpec((D, tn), lambda i, j: (0, j)),
                  pl.BlockSpec((D, tn), lambda i, j: (0, j + nn)),
                  pl.BlockSpec((D, tn), lambda i, j: (0, j))],
        out_specs=[blk] * 5,
        out_shape=[jax.ShapeDtypeStruct((M, N), F32), jax.ShapeDtypeStruct((M, N), F32),
                   jax.ShapeDtypeStruct((M, N), BF16), jax.ShapeDtypeStruct((M, N), BF16),
                   jax.ShapeDtypeStruct((M, N), BF16)],
        scratch_shapes=[pltpu.VMEM((tm, D), BF16), pltpu.VMEM((tm, D), BF16)],
        compiler_params=_params(("parallel", "arbitrary")),
        name="kvq_proj",
    )(x, gkv.reshape(1, D), gq.reshape(1, D), wkv, wkv, wq)


SB_DEAD = 152.0


def _sb_block(q, kb, vb, tri, carry, valid, on=None):
    z = _dot_nt(q, kb)
    sp = jnp.maximum(z, 0.0) + jnp.log2(1.0 + jnp.exp2(jnp.minimum(z, -z)))
    if valid is not None:
        sp = jnp.where(valid, sp, 0.0)
    if on is not None:
        sp = jnp.where(on, sp, 0.0)
    later = jnp.dot(sp.astype(BF16), tri, preferred_element_type=F32)
    logw = z - sp - later
    if carry is not None:
        logw = logw - carry
    w = jnp.exp2(logw)
    if valid is not None:
        w = jnp.where(valid, w, 0.0)
    if on is not None:
        w = jnp.where(on, w, 0.0)
    out = jnp.dot(w.astype(BF16), vb.astype(BF16), preferred_element_type=F32)
    return out, later[:, 0:1] + sp[:, 0:1]


def _sb_prompt_kernel(q_ref, k_ref, v_ref, tri_ref, o_ref, acc_s, carry_s, *, nsub, tb):
    qi = pl.program_id(2)
    blk0 = qi * nsub
    tri = tri_ref[...]
    valid = _col_ids((tb, tb)) < _row_ids((tb, tb))
    rows = lambda i: slice(i * tb, (i + 1) * tb)

    def kv_block(j):
        k0 = pl.multiple_of(j * tb, tb)
        return k_ref[pl.ds(k0, tb), :], v_ref[pl.ds(k0, tb), :]

    kvs = [kv_block(blk0 + i) for i in range(nsub)]
    acc, car = [], []
    for i in range(nsub):
        out, tot = _sb_block(q_ref[rows(i), :], kvs[i][0], kvs[i][1], tri, None, valid)
        acc.append(out)
        car.append(tot)
    for i in range(nsub):
        if i == 0:
            kb, vb = kv_block(jnp.maximum(blk0 - 1, 0))
            out, tot = _sb_block(q_ref[rows(0), :], kb, vb, tri, car[0], None, on=blk0 > 0)
        else:
            out, tot = _sb_block(q_ref[rows(i), :], kvs[i - 1][0], kvs[i - 1][1], tri,
                                 car[i], None)
        acc_s[i] = acc[i] + out
        carry_s[i] = car[i] + tot

    def live_min(t):
        m = jnp.float32(jnp.inf)
        for i in range(nsub):
            m = jnp.where(blk0 + i - 2 - t >= 0, jnp.minimum(m, jnp.min(carry_s[i])), m)
        return m

    def cond(st):
        t, m = st
        return jnp.logical_and(blk0 + nsub - 3 - t >= 0, m < SB_DEAD)

    def body(st):
        t, _ = st
        for i in range(nsub):
            j = blk0 + i - 2 - t
            kb, vb = kv_block(jnp.maximum(j, 0))
            out, tot = _sb_block(q_ref[rows(i), :], kb, vb, tri, carry_s[i], None, on=j >= 0)
            acc_s[i] += out
            carry_s[i] += tot
        return t + 1, live_min(t + 1)

    lax.while_loop(cond, body, (jnp.int32(0), live_min(0)))
    for i in range(nsub):
        o_ref[rows(i), :] = acc_s[i].astype(o_ref.dtype)


def _sb_decode_kernel(q_ref, k_ref, v_ref, kc_hbm, vc_hbm, trid_ref, trip_ref, o_ref,
                      acc_s, carry_s, near_k, near_v, near_sem, old_k, old_v, old_sem,
                      *, tq, tk, n_heads, n_old):
    b = pl.program_id(0)
    slot = b % 2
    hd = SB_HEAD_DIM
    newest = n_old * tk
    heads = range(n_heads)
    lanes = lambda h: slice(h * hd, (h + 1) * hd)

    def block_copies(row, k0, k_dst, v_dst, sem, first):
        cps = []
        for h in heads:
            cps.append(pltpu.make_async_copy(kc_hbm.at[row, pl.ds(k0, tk), h, :], k_dst.at[h],
                                             sem.at[first]))
            cps.append(pltpu.make_async_copy(vc_hbm.at[row, pl.ds(k0, tk), h, :], v_dst.at[h],
                                             sem.at[first + 1]))
        return cps

    @pl.when(b == 0)
    def _():
        for c in block_copies(0, newest, near_k.at[0], near_v.at[0], near_sem, 0):
            c.start()

    @pl.when(b + 1 < pl.num_programs(0))
    def _():
        for c in block_copies(b + 1, newest, near_k.at[1 - slot], near_v.at[1 - slot],
                              near_sem, 2 * (1 - slot)):
            c.start()

    valid = _col_ids((tq, tq)) < _row_ids((tq, tq))
    trid = trid_ref[...]
    trip = trip_ref[...]
    q = [q_ref[:, lanes(h)] for h in heads]
    own = [_sb_block(q[h], k_ref[:, lanes(h)], v_ref[:, lanes(h)], trid, None, valid)
           for h in heads]
    for c in block_copies(b, newest, near_k.at[slot], near_v.at[slot], near_sem, 2 * slot):
        c.wait()
    for h in heads:
        acc, carry = own[h]
        out, tot = _sb_block(q[h], near_k[slot, h], near_v[slot, h], trip, carry, None)
        acc_s[h] = acc + out
        carry_s[h] = carry + tot

    def cond(st):
        t, m = st
        return jnp.logical_and(t < n_old, m < SB_DEAD)

    def body(st):
        t, _ = st
        k0 = pl.multiple_of((n_old - 1 - t) * tk, tk)
        copies = block_copies(b, k0, old_k, old_v, old_sem, 0)
        for c in copies:
            c.start()
        for c in copies:
            c.wait()
        for h in heads:
            out, tot = _sb_block(q_ref[:, lanes(h)], old_k[h], old_v[h], trip, carry_s[h], None)
            acc_s[h] += out
            carry_s[h] += tot
        return t + 1, jnp.min(carry_s[...])

    lax.while_loop(cond, body, (jnp.int32(0), jnp.min(carry_s[...])))
    for h in heads:
        o_ref[:, lanes(h)] = acc_s[h].astype(o_ref.dtype)


def _suffix_ones(n):
    return (_row_ids((n, n)) > _col_ids((n, n))).astype(BF16)


SB_BLOCK = 256
SB_QSUB = 4


def _sb_attention(q, k, v, k_past=None, v_past=None):
    B, T, D = q.shape
    H = D // SB_HEAD_DIM
    hd = SB_HEAD_DIM
    full = lambda n: pl.BlockSpec((n, n), lambda b, h, i: (0, 0))
    seq = pl.BlockSpec((None, T, hd), lambda b, h, i: (b, 0, h))
    if k_past is None:
        tb = _tile(T, SB_BLOCK)
        nsub = _tile(T // tb, SB_QSUB)
        tq = nsub * tb
        qspec = pl.BlockSpec((None, tq, hd), lambda b, h, i: (b, i, h))
        return pl.pallas_call(
            functools.partial(_sb_prompt_kernel, nsub=nsub, tb=tb),
            grid=(B, H, T // tq),
            in_specs=[qspec, seq, seq, full(tb)],
            out_specs=qspec,
            out_shape=jax.ShapeDtypeStruct((B, T, D), BF16),
            scratch_shapes=[pltpu.VMEM((nsub, tb, hd), F32), pltpu.VMEM((nsub, tb, 1), F32)],
            compiler_params=_params(("parallel", "parallel", "arbitrary")),
            name="sb_attention",
        )(q, k, v, _suffix_ones(tb))
    P = k_past.shape[1]
    tk = _tile(P, SB_BLOCK)
    row = pl.BlockSpec((None, T, D), lambda b: (b, 0, 0))
    tri = lambda n: pl.BlockSpec((n, n), lambda b: (0, 0))
    hbm = pl.BlockSpec(memory_space=pl.ANY)
    return pl.pallas_call(
        functools.partial(_sb_decode_kernel, tq=T, tk=tk, n_heads=H, n_old=P // tk - 1),
        grid=(B,),
        in_specs=[row, row, row, hbm, hbm, tri(T), tri(tk)],
        out_specs=row,
        out_shape=jax.ShapeDtypeStruct((B, T, D), BF16),
        scratch_shapes=[pltpu.VMEM((H, T, hd), F32), pltpu.VMEM((H, T, 1), F32),
                        pltpu.VMEM((2, H, tk, hd), F32), pltpu.VMEM((2, H, tk, hd), F32),
                        pltpu.SemaphoreType.DMA((4,)),
                        pltpu.VMEM((H, tk, hd), F32), pltpu.VMEM((H, tk, hd), F32),
                        pltpu.SemaphoreType.DMA((2,))],
        compiler_params=_params(("arbitrary",)),
        name="sb_attention_decode",
    )(q, k, v, k_past, v_past, _suffix_ones(T), _suffix_ones(tk))


LATE_WEIGHTS = ('rw_wo', 'f_wup', 'f_wdown', 'w_kv', 'sb_wq', 'sb_wo')


def _trunk(x, shift0, wkv0, conv0, k_past, v_past, p, late_f32=None):
    B, T, D = x.shape
    M = B * T
    r, k, v, lw, asig, g, shift = _rw_proj(
        x, shift0[0], p['a_norm_g'][0], p['rw_mu'][0], p['rw_w0'][0], p['rw_a0'][0],
        p['rw_wr'], p['rw_wk'], p['rw_wv'], p['rw_w1'], p['rw_w2'], p['rw_a1'], p['rw_a2'],
        p['rw_g1'], p['rw_g2'])
    cast = () if late_f32 is None else tuple(
        late_f32[n].reshape(-1, late_f32[n].shape[-1]) for n in LATE_WEIGHTS)
    o, s_bd, casted = _wkv(r, lw, k, v, asig, g, _state_to_pairs(wkv0[0]),
                           p['rw_kk'][0], p['rw_ka'][0], p['rw_rk'][0].reshape(-1),
                           p['rw_lnx_g'][0], p['rw_lnx_b'][0], cast=cast)
    if late_f32 is not None:
        for n, w in zip(LATE_WEIGHTS, casted):
            p[n] = w.reshape(late_f32[n].shape)
    x, conv_a = _ffn(x, o, p['rw_wo'], conv0[0], p['f_norm_g'][0], p['f_wup'],
                     p['f_conv_w'][0], p['f_conv_b'][0], p['f_wdown'], p['out_norm_g'],
                     layer=0, final_norm=False)
    k_sh, v_sh, kb, vb, q = _kvq(x.reshape(M, D), p['kv_norm_g'], p['b_norm_g'][0],
                                 p['w_kv'], p['sb_wq'])
    d_att = q.shape[1]
    att = _sb_attention(q.reshape(B, T, d_att), kb.reshape(B, T, d_att), vb.reshape(B, T, d_att),
                        k_past, v_past)
    y, conv_b = _ffn(x, att, p['sb_wo'], conv0[1], p['f_norm_g'][1], p['f_wup'],
                     p['f_conv_w'][1], p['f_conv_b'][1], p['f_wdown'], p['out_norm_g'],
                     layer=1, final_norm=True)
    H = d_att // SB_HEAD_DIM
    return (y, _pairs_to_state(s_bd)[None], shift.reshape(1, B, D), jnp.stack([conv_a, conv_b]),
            k_sh.reshape(B, T, H, SB_HEAD_DIM), v_sh.reshape(B, T, H, SB_HEAD_DIM))


def kernel(x_prompt, x_sample, cache_k, cache_v, state_wkv, state_shift, state_conv, a_norm_g, rw_mu, rw_w0, rw_w1, rw_w2, rw_a0, rw_a1, rw_a2, rw_g1, rw_g2, rw_kk, rw_ka, rw_rk, rw_wr, rw_wk, rw_wv, rw_wo, rw_lnx_g, rw_lnx_b, kv_norm_g, w_kv, b_norm_g, sb_wq, sb_wo, f_norm_g, f_wup, f_conv_w, f_conv_b, f_wdown, out_norm_g):
    bf = lambda w: w.astype(BF16)
    p = dict(a_norm_g=a_norm_g, rw_mu=rw_mu, rw_w0=rw_w0, rw_a0=rw_a0,
             rw_w1=bf(rw_w1[0]), rw_w2=bf(rw_w2[0]), rw_a1=bf(rw_a1[0]), rw_a2=bf(rw_a2[0]),
             rw_g1=bf(rw_g1[0]), rw_g2=bf(rw_g2[0]),
             rw_kk=rw_kk, rw_ka=rw_ka, rw_rk=rw_rk,
             rw_wr=bf(rw_wr[0]), rw_wk=bf(rw_wk[0]), rw_wv=bf(rw_wv[0]),
             rw_lnx_g=rw_lnx_g, rw_lnx_b=rw_lnx_b, kv_norm_g=kv_norm_g,
             b_norm_g=b_norm_g, f_norm_g=f_norm_g,
             f_conv_w=f_conv_w, f_conv_b=f_conv_b, out_norm_g=out_norm_g)
    late_f32 = dict(rw_wo=rw_wo[0], f_wup=f_wup, f_wdown=f_wdown, w_kv=w_kv,
                    sb_wq=sb_wq[0], sb_wo=sb_wo[0])
    B, _, D = x_prompt.shape
    n_a = state_shift.shape[0]
    depth = state_conv.shape[0]
    F = state_conv.shape[-1]
    H = state_wkv.shape[2]
    shift0 = jnp.zeros((n_a, B, D), x_prompt.dtype)
    wkv0 = jnp.zeros((n_a, B, H, RW_HEAD, RW_HEAD), F32)
    conv0 = jnp.zeros((depth, B, CONV_W - 1, F), x_prompt.dtype)
    y_p, wkv_p, shift_p, conv_p, k_p, v_p = _trunk(x_prompt, shift0, wkv0, conv0, None, None, p,
                                                   late_f32=late_f32)
    y_s, wkv_s, shift_s, conv_s, k_s, v_s = _trunk(
        x_sample, state_shift, state_wkv.astype(F32), state_conv, cache_k, cache_v, p)
    return (y_p, y_s, wkv_p.astype(state_wkv.dtype), shift_p, conv_p, k_p, v_p,
            wkv_s.astype(state_wkv.dtype), shift_s, conv_s, k_s, v_s)
```

```python
import functools

import jax
import jax.numpy as jnp
from jax import lax
from jax.experimental import pallas as pl
from jax.experimental.pallas import tpu as pltpu

F32 = jnp.float32
BF16 = jnp.bfloat16

RW_HEAD = 64
PAIR = 2 * RW_HEAD
SB_HEAD_DIM = 128
GN_EPS = 64e-5
NORM_EPS = 1e-6
CONV_W = 3
RW_ACT = ('r', 'k', 'v', 'rate', 'gate')
BF16_ROWS = 16
WKV_CHUNK = 64
NEUMANN_BLOCK = 16
WKV_CHUNKS_PER_STEP = 2
LOG2_E = 1.4426950408889634
ROW_TILE = 512
COL_TILE = 512
VMEM_LIMIT = 56 * 1024 * 1024


def _tile(n, pref):
    if n <= pref:
        return n
    t = pref
    while n % t:
        t //= 2
    return t


def _row_tiling(B, T):
    if T >= ROW_TILE:
        return B, 1, _tile(T, ROW_TILE)
    nseq = _tile(B, max(ROW_TILE // T, 1))
    return B // nseq, nseq, nseq * T


def _params(sem):
    return pltpu.CompilerParams(dimension_semantics=sem, vmem_limit_bytes=VMEM_LIMIT)


def _dot(a, b):
    return jnp.dot(a.astype(BF16), b.astype(BF16), preferred_element_type=F32)


def _dot_nt(a, b):
    return lax.dot_general(a.astype(BF16), b.astype(BF16), (((1,), (1,)), ((), ())),
                           preferred_element_type=F32)


def _dot_tn(a, b):
    return lax.dot_general(a.astype(BF16), b.astype(BF16), (((0,), (0,)), ((), ())),
                           preferred_element_type=F32)


def _rms_scale(x):
    return x * lax.rsqrt(jnp.mean(x * x, axis=-1, keepdims=True) + NORM_EPS)


def _softplus(u):
    return jnp.maximum(u, 0.0) + jnp.log1p(jnp.exp(-jnp.abs(u)))


def _row_ids(shape):
    return lax.broadcasted_iota(jnp.int32, shape, 0)


def _col_ids(shape):
    return lax.broadcasted_iota(jnp.int32, shape, 1)


def _rw_proj_kernel(x_ref, shift0_ref, ng_ref, mu_ref, bias_ref, w3_ref, w1_ref, a1_ref, g1_ref,
                    lora2_ref, act_out, lw_out, shift_out,
                    xr_s, xk_s, xv_s, hw_s, ha_s, hg_s, carry_s, *, nseq, seq_rows):
    lw_dim, la_dim = hw_s.shape[1], ha_s.shape[1]
    w2 = lora2_ref[0:lw_dim, :]
    a2 = lora2_ref[lw_dim:lw_dim + la_dim, :]
    g2 = lora2_ref[lw_dim + la_dim:, :]
    t = pl.program_id(1)
    j = pl.program_id(2)

    @pl.when(j == 0)
    def _():
        @pl.when(t == 0)
        def _():
            carry_s[...] = shift0_ref[0]

        xn = _rms_scale(x_ref[...]) * ng_ref[...]
        rows = _row_ids(xn.shape)
        x_prev = jnp.where(rows == 0, carry_s[...], pltpu.roll(xn, 1, axis=0))
        for s in range(1, nseq):
            x_prev = jnp.where(rows == s * seq_rows, shift0_ref[s], x_prev)
        for s in range(nseq):
            shift_out[s] = xn[(s + 1) * seq_rows - 1:(s + 1) * seq_rows, :]
        carry_s[...] = xn[nseq * seq_rows - 1:nseq * seq_rows, :]
        xx = x_prev - xn
        mix = lambda i: (xn + xx * mu_ref[i:i + 1, :]).astype(BF16)
        xr_s[...] = mix(0)
        xk_s[...] = mix(2)
        xv_s[...] = mix(3)
        hw_s[...] = jnp.tanh(_dot(mix(1), w1_ref[...])).astype(BF16)
        ha_s[...] = _dot(mix(4), a1_ref[...]).astype(BF16)
        hg_s[...] = jax.nn.sigmoid(_dot(mix(5), g1_ref[...])).astype(BF16)

    bf = act_out.dtype
    act_out[0] = _dot(xr_s[...], w3_ref[0]).astype(bf)
    act_out[1] = _dot(xk_s[...], w3_ref[1]).astype(bf)
    act_out[2] = _dot(xv_s[...], w3_ref[2]).astype(bf)
    w_log = -_softplus(-(bias_ref[0:1, :] + _dot(hw_s[...], w2))) - 0.5
    lw_out[...] = -jnp.exp(w_log)
    act_out[3] = jax.nn.sigmoid(bias_ref[1:2, :] + _dot(ha_s[...], a2)).astype(bf)
    act_out[4] = _dot(hg_s[...], g2).astype(bf)


def _rw_proj(x, shift0, ng, mu, bias, w3, w1, a1, g1, lora2):
    B, T, D = x.shape
    G, nseq, tm = _row_tiling(B, T)
    R = B * T // G
    tn = _tile(D, COL_TILE)
    lw_dim, la_dim, lg_dim = w1.shape[1], a1.shape[1], g1.shape[1]
    n_act = len(RW_ACT)
    row = lambda b, t, j: (0, 0)
    act5, lw, shift = pl.pallas_call(
        functools.partial(_rw_proj_kernel, nseq=nseq, seq_rows=tm // nseq),
        grid=(G, R // tm, D // tn),
        in_specs=[
            pl.BlockSpec((None, tm, D), lambda b, t, j: (b, t, 0)),
            pl.BlockSpec((nseq, 1, D), lambda b, t, j: (b, 0, 0)),
            pl.BlockSpec((1, D), row),
            pl.BlockSpec((6, D), row),
            pl.BlockSpec((2, tn), lambda b, t, j: (0, j)),
            pl.BlockSpec((3, D, tn), lambda b, t, j: (0, 0, j)),
            pl.BlockSpec((D, lw_dim), row),
            pl.BlockSpec((D, la_dim), row),
            pl.BlockSpec((D, lg_dim), row),
            pl.BlockSpec((lora2.shape[0], tn), lambda b, t, j: (0, j)),
        ],
        out_specs=[pl.BlockSpec((n_act, None, tm, tn), lambda b, t, j: (0, b, t, j)),
                   pl.BlockSpec((None, tm, tn), lambda b, t, j: (b, t, j)),
                   pl.BlockSpec((nseq, 1, D), lambda b, t, j: (b, 0, 0))],
        out_shape=[jax.ShapeDtypeStruct((n_act, G, R, D), BF16),
                   jax.ShapeDtypeStruct((G, R, D), F32),
                   jax.ShapeDtypeStruct((B, 1, D), F32)],
        scratch_shapes=[
            pltpu.VMEM((tm, D), BF16), pltpu.VMEM((tm, D), BF16), pltpu.VMEM((tm, D), BF16),
            pltpu.VMEM((tm, lw_dim), BF16), pltpu.VMEM((tm, la_dim), BF16),
            pltpu.VMEM((tm, lg_dim), BF16), pltpu.VMEM((1, D), F32),
        ],
        compiler_params=_params(("arbitrary", "arbitrary", "arbitrary")),
        name="rw_proj",
    )(x.reshape(G, R, D), shift0.reshape(B, 1, D), ng.reshape(1, D), mu, bias, w3, w1, a1, g1,
      lora2)
    return act5.reshape(n_act, B, T, D), lw.reshape(B, T, D), shift


def _wkv_kernel(act_ref, lw_ref, s0_ref,
                kkw_ref, kaw_ref, rk_ref, lng_ref, lnb_ref, *rest,
                chunk, n_sub, n_pairs, n_cast):
    cast_in = rest[:n_cast]
    o_ref, s_ref = rest[n_cast:n_cast + 2]
    cast_out = rest[n_cast + 2:2 * n_cast + 2]
    c_s = rest[-1]
    C = chunk
    C2 = 2 * C
    trow = lambda sc: slice(sc * C, (sc + 1) * C)

    for w_in, w_out in zip(cast_in, cast_out):
        w_out[...] = w_in[...].astype(w_out.dtype)

    @pl.when(pl.program_id(1) == 0)
    def _():
        s_ref[...] = s0_ref[...]

    tri = (_col_ids((C, C)) <= _row_ids((C, C))).astype(BF16)
    for sc in range(n_sub):
        lw_all = lw_ref[trow(sc), :]
        lw_hi = lw_all.astype(BF16)
        lw_r = lw_all - lw_hi.astype(F32)
        lw_mid = lw_r.astype(BF16)
        lw_lo = (lw_r - lw_mid.astype(F32)).astype(BF16)
        c_s[trow(sc), :] = (jnp.dot(tri, lw_hi, preferred_element_type=F32)
                            + jnp.dot(tri, lw_mid, preferred_element_type=F32)
                            + jnp.dot(tri, lw_lo, preferred_element_type=F32))

    head0 = _col_ids((C, PAIR)) < RW_HEAD
    ones_bd = ((_row_ids((PAIR, PAIR)) // RW_HEAD) == (_col_ids((PAIR, PAIR)) // RW_HEAD))
    ones_bd_bf = ones_bd.astype(BF16)

    rr = _row_ids((C2, C2))
    cc = _col_ids((C2, C2))
    same_head = (rr // C) == (cc // C)
    nb = NEUMANN_BLOCK
    same16 = (rr // nb) == (cc // nb)
    same32 = (rr // (2 * nb)) == (cc // (2 * nb))
    m_e1 = same32 & (~same16) if C >= 2 * nb else None
    m_e2 = same_head & (~same32) if C >= 4 * nb else None
    eye22 = (rr == cc).astype(F32)
    r12 = _row_ids((C, C2))
    c12 = _col_ids((C, C2))
    strict12 = (c12 % C) < r12
    incl12 = (c12 % C) <= r12
    left12 = c12 < C
    same16_c = ((c12 % C) // nb) == (r12 // nb)
    c16 = _col_ids((nb, C2))
    blk_of_lane = (c16 % C) // nb
    eye16 = ((c16 % nb) == _row_ids((nb, C2))).astype(F32)

    def expand(x):
        return jnp.where(same_head, jnp.concatenate([x, x], axis=0), 0.0)

    def expand16(x):
        return jnp.where(same16, jnp.concatenate([x] * (C2 // nb), axis=0), 0.0)

    def group_sum(x):
        return jnp.dot(x.astype(BF16), ones_bd_bf, preferred_element_type=F32)

    cat0 = lambda xs: jnp.concatenate(xs, axis=0)
    zero_h1 = lambda x: jnp.where(head0, x, 0.0)
    zero_h0 = lambda x: jnp.where(head0, 0.0, x)
    inv_n = 1.0 / RW_HEAD

    def state_free(items):
        n = range(len(items))
        tr = [trow(sc) for sc, _ in items]
        sl = [slice(p * PAIR, (p + 1) * PAIR) for _, p in items]
        plane = lambda name: [act_ref[RW_ACT.index(name), tr[i], sl[i]].astype(F32) for i in n]
        r, k, v, asig = plane('r'), plane('k'), plane('v'), plane('rate')
        lw = [lw_ref[tr[i], sl[i]] for i in n]
        c = [c_s[tr[i], sl[i]] for i in n]
        c_end = [x[C - 1:C, :] for x in c]

        kkv = [k[i] * kkw_ref[:, sl[i]] for i in n]
        ss = [group_sum(x * x) for x in kkv]
        kk = [kkv[i] / jnp.maximum(jnp.sqrt(ss[i]), 1e-12) for i in n]
        b_in = [kk[i] * asig[i] for i in n]
        k_in = [k[i] * (1.0 + (asig[i] - 1.0) * kaw_ref[:, sl[i]]) for i in n]
        bonus_s = [group_sum(r[i] * k_in[i] * rk_ref[:, sl[i]]) for i in n]

        e_neg = [jnp.exp(-x) for x in c]
        a_t = [(-kk[i]) * jnp.exp(c[i] - lw[i]) for i in n]
        r_t = [r[i] * jnp.exp(c[i]) for i in n]
        b_h = [b_in[i] * e_neg[i] for i in n]
        k_h = [k_in[i] * e_neg[i] for i in n]
        e_end = [jnp.exp(c_end[i] - c[i]) for i in n]
        RE = [cat0([b_in[i] * e_end[i], k_in[i] * e_end[i]]) for i in n]

        L = [cat0([a_t[i], r_t[i]]) for i in n]
        G = [_dot_nt(L[i], cat0([zero_h1(b_h[i]), zero_h1(k_h[i]),
                                 zero_h0(k_h[i]), zero_h0(b_h[i])])) for i in n]
        G0 = [x[:, :C2] for x in G]
        G1 = [x[:, C2:] for x in G]
        g0t = [x[:C] for x in G0]
        g1t = [x[:C] for x in G1]

        n_c = [jnp.where(strict12, jnp.where(left12, g0t[i], g1t[i]), 0.0) for i in n]
        n_bd = [expand(x) for x in n_c]
        d1 = [sum(jnp.where(blk_of_lane == b, x[b * nb:(b + 1) * nb], 0.0)
                  for b in range(C // nb)) for x in n_c]
        d1_bd = [expand16(x) for x in d1]
        d2 = [_dot(d1[i], d1_bd[i]) for i in n]
        d2_bd = [expand16(x) for x in d2]
        d4 = [_dot(d2[i], d2_bd[i]) for i in n]
        d4_bd = [expand16(x) for x in d4]
        d8 = [_dot(d4[i], d4_bd[i]) for i in n]
        pa = [_dot(eye16 + d1[i], eye22 + d2_bd[i]) for i in n]
        pb = [_dot(eye16 + d4[i], eye22 + expand16(d8[i])) for i in n]
        t16 = [_dot(pa[i], expand16(pb[i])) for i in n]
        tcat = [jnp.where(same16_c, cat0([x] * (C // nb)), 0.0) for x in t16]
        for m_e in (m_e1, m_e2):
            if m_e is not None:
                t_bd = [expand(x) for x in tcat]
                x = [_dot(tcat[i], jnp.where(m_e, n_bd[i], 0.0)) for i in n]
                tcat = [tcat[i] + _dot(x[i], t_bd[i]) for i in n]

        v0 = [jnp.where(head0, x, 0.0) for x in v]
        v1 = [jnp.where(head0, 0.0, x) for x in v]
        av = [_dot(jnp.where(strict12, jnp.where(left12, g1t[i], g0t[i]), 0.0),
                   cat0([v1[i], v0[i]])) for i in n]
        bot = [jnp.concatenate([jnp.where(incl12, G0[i][C:], 0.0),
                                jnp.where(incl12, G1[i][C:], 0.0)], axis=1) for i in n]
        return dict(L=L, tcat=tcat, av=av, bot=bot, RE=RE, v=v, v0=v0, v1=v1,
                    decay=[jnp.exp(x) for x in c_end], bonus=bonus_s)

    def state_chain(sc, pre, sel):
        n = range(len(sel))
        at = lambda name: [pre[name][j] for j in sel]
        L, tcat, av, bot, RE, v, v0, v1, decay, bonus_s = (
            at(x) for x in ("L", "tcat", "av", "bot", "RE", "v", "v0", "v1", "decay", "bonus"))
        sl = [slice(p * PAIR, (p + 1) * PAIR) for p in range(n_pairs)]
        S = [s_ref[p] for p in range(n_pairs)]
        P = [_dot_nt(L[i], S[i]) for i in n]
        W = [P[i][:C] + av[i] for i in n]
        U = [_dot(tcat[i], cat0([zero_h1(W[i]), zero_h0(W[i])])) for i in n]
        Y = [P[i][C:] + _dot(bot[i], cat0([zero_h1(U[i]), v0[i], v1[i], zero_h0(U[i])]))
             for i in n]
        for i in n:
            upd = _dot_tn(cat0([U[i], v[i]]), RE[i])
            s_ref[i] = S[i] * decay[i] + jnp.where(ones_bd, upd, 0.0)

        mean = [group_sum(x) * inv_n for x in Y]
        dlt = [Y[i] - mean[i] for i in n]
        var = [group_sum(x * x) * inv_n for x in dlt]
        for i in n:
            yn = dlt[i] * lax.rsqrt(var[i] + GN_EPS) * lng_ref[:, sl[i]] + lnb_ref[:, sl[i]]
            gate = act_ref[RW_ACT.index('gate'), trow(sc), sl[i]].astype(F32)
            out = (yn + bonus_s[i] * v[i]) * gate
            o_ref[trow(sc), sl[i]] = out.astype(o_ref.dtype)

    items = [(sc, p) for sc in range(n_sub) for p in range(n_pairs)]
    pre = state_free(items)
    for sc in range(n_sub):
        state_chain(sc, pre, [sc * n_pairs + p for p in range(n_pairs)])


def _cast_slab(rows, n_steps):
    share = 1
    while rows * share % (n_steps * BF16_ROWS):
        share *= 2
    return rows * share // n_steps, share


def _wkv(act5, lw, s0_bd, kkw, kaw, rk, lng, lnb, cast=()):
    B, T, D = lw.shape
    C = _tile(T, WKV_CHUNK)
    n_sub = _tile(T // C, WKV_CHUNKS_PER_STEP)
    n_pairs = D // PAIR
    steps_t = T // (n_sub * C)
    act = pl.BlockSpec((None, n_sub * C, D), lambda b, t: (b, t, 0))
    vec = pl.BlockSpec((1, D), lambda b, t: (0, 0))
    st = pl.BlockSpec((None, n_pairs, PAIR, PAIR), lambda b, t: (b, 0, 0, 0))
    cast_specs = []
    for w in cast:
        slab, share = _cast_slab(w.shape[0], B * steps_t)
        cast_specs.append(pl.BlockSpec(
            (slab, w.shape[1]), lambda b, t, share=share: ((b * steps_t + t) // share, 0)))
    outs = pl.pallas_call(
        functools.partial(_wkv_kernel, chunk=C, n_sub=n_sub, n_pairs=n_pairs,
                          n_cast=len(cast)),
        grid=(B, steps_t),
        in_specs=[pl.BlockSpec((len(RW_ACT), None, n_sub * C, D), lambda b, t: (0, b, t, 0)),
                  act, st] + [vec] * 5 + cast_specs,
        out_specs=[act, st] + cast_specs,
        out_shape=[jax.ShapeDtypeStruct((B, T, D), BF16),
                   jax.ShapeDtypeStruct((B, n_pairs, PAIR, PAIR), F32)]
        + [jax.ShapeDtypeStruct(w.shape, BF16) for w in cast],
        scratch_shapes=[pltpu.VMEM((n_sub * C, D), F32)],
        compiler_params=_params(("arbitrary", "arbitrary")),
        name="wkv",
    )(act5, lw, s0_bd, kkw.reshape(1, D), kaw.reshape(1, D), rk.reshape(1, D),
      lng.reshape(1, D), lnb.reshape(1, D), *cast)
    return outs[0], outs[1], outs[2:]


def _state_to_pairs(s):
    B, H, n, _ = s.shape
    s = s.reshape(B, H // 2, 2, n, n)
    z = jnp.zeros((B, H // 2, n, n), s.dtype)
    top = jnp.concatenate([s[:, :, 0], z], axis=-1)
    bot = jnp.concatenate([z, s[:, :, 1]], axis=-1)
    return jnp.concatenate([top, bot], axis=-2)


def _pairs_to_state(sp):
    B, P, _, _ = sp.shape
    n = RW_HEAD
    h0 = sp[:, :, :n, :n]
    h1 = sp[:, :, n:, n:]
    return jnp.stack([h0, h1], axis=2).reshape(B, 2 * P, n, n)


def _ffn_kernel(x_ref, a_ref, wo_ref, conv0_ref, ng_ref, wg_ref, wv_ref, cw_ref, cb_ref,
                wd_ref, og_ref, o_ref, conv_out, xn_s, acc_s, carry_s,
                *, final_norm, nseq, seq_rows):
    t = pl.program_id(1)
    f = pl.program_id(2)
    nf = pl.num_programs(2)
    keep = CONV_W - 1

    @pl.when(f == 0)
    def _():
        x1 = x_ref[...] + jnp.dot(a_ref[...], wo_ref[...], preferred_element_type=F32)
        acc_s[...] = x1
        xn_s[...] = (_rms_scale(x1) * ng_ref[...]).astype(BF16)

    @pl.when(t == 0)
    def _():
        carry_s[f] = conv0_ref[0]

    xn = xn_s[...]
    gate = jnp.dot(xn, wg_ref[...], preferred_element_type=F32)
    val = jnp.dot(xn, wv_ref[...], preferred_element_type=F32)
    prev = carry_s[f]
    rows = _row_ids(gate.shape)
    g1 = jnp.where(rows == 0, prev[1:2, :], pltpu.roll(gate, 1, axis=0))
    g2 = jnp.where(rows == 0, prev[0:1, :],
                   jnp.where(rows == 1, prev[1:2, :], pltpu.roll(gate, 2, axis=0)))
    for s in range(1, nseq):
        first = s * seq_rows
        prev_s = conv0_ref[s]
        g1 = jnp.where(rows == first, prev_s[1:2, :], g1)
        g2 = jnp.where(rows == first, prev_s[0:1, :],
                       jnp.where(rows == first + 1, prev_s[1:2, :], g2))
    c = cb_ref[...] + cw_ref[0:1, :] * g2
    c = c + cw_ref[1:2, :] * g1
    c = c + cw_ref[2:3, :] * gate
    y = (c * jax.nn.sigmoid(c)) * val
    acc_s[...] += jnp.dot(y.astype(BF16), wd_ref[...], preferred_element_type=F32)
    tf = gate.shape[1]
    cols = pl.ds(pl.multiple_of(f * tf, tf), tf)
    for s in range(nseq):
        conv_out[s, :, cols] = gate[(s + 1) * seq_rows - keep:(s + 1) * seq_rows, :]
    carry_s[f] = gate[nseq * seq_rows - keep:nseq * seq_rows, :]

    @pl.when(f == nf - 1)
    def _():
        out = acc_s[...]
        if final_norm:
            out = _rms_scale(out) * og_ref[...]
        o_ref[...] = out


def _ffn(x, a, wo, conv0, ng, wup, cw, cb, wdown, og, *, layer, final_norm):
    B, T, D = x.shape
    F = wdown.shape[1]
    Ka = a.shape[-1]
    G, nseq, tm = _row_tiling(B, T)
    R = B * T // G
    tf = _tile(F, COL_TILE)
    nf = F // tf
    keep = CONV_W - 1
    row = lambda b, t, f: (0, 0)
    out, conv = pl.pallas_call(
        functools.partial(_ffn_kernel, final_norm=final_norm, nseq=nseq, seq_rows=tm // nseq),
        grid=(G, R // tm, nf),
        in_specs=[
            pl.BlockSpec((None, tm, D), lambda b, t, f: (b, t, 0)),
            pl.BlockSpec((None, tm, Ka), lambda b, t, f: (b, t, 0)),
            pl.BlockSpec((Ka, D), row, pipeline_mode=pl.Buffered(1)),
            pl.BlockSpec((nseq, keep, tf), lambda b, t, f: (b, 0, f)),
            pl.BlockSpec((1, D), row),
            pl.BlockSpec((None, D, tf), lambda b, t, f: (layer, 0, f)),
            pl.BlockSpec((None, D, tf), lambda b, t, f: (layer, 0, f + nf)),
            pl.BlockSpec((CONV_W, tf), lambda b, t, f: (0, f)),
            pl.BlockSpec((1, tf), lambda b, t, f: (0, f)),
            pl.BlockSpec((None, tf, D), lambda b, t, f: (layer, f, 0)),
            pl.BlockSpec((1, D), row),
        ],
        out_specs=[pl.BlockSpec((None, tm, D), lambda b, t, f: (b, t, 0)),
                   pl.BlockSpec((nseq, keep, F), lambda b, t, f: (b, 0, 0))],
        out_shape=[jax.ShapeDtypeStruct((G, R, D), F32),
                   jax.ShapeDtypeStruct((B, keep, F), F32)],
        scratch_shapes=[pltpu.VMEM((tm, D), BF16), pltpu.VMEM((tm, D), F32),
                        pltpu.VMEM((nf, keep, tf), F32)],
        compiler_params=_params(("arbitrary", "arbitrary", "arbitrary")),
        name="conv_ffn",
    )(x.reshape(G, R, D), a.reshape(G, R, Ka), wo, conv0, ng.reshape(1, D), wup, wup, cw,
      cb.reshape(1, F), wdown, og.reshape(1, D))
    return out.reshape(B, T, D), conv


def _kvq_kernel(x_ref, gkv_ref, gq_ref, wk_ref, wv_ref, wq_ref,
                k_out, v_out, kb_out, vb_out, q_out, xkv_s, xq_s, *, q_scale):
    @pl.when(pl.program_id(1) == 0)
    def _():
        xh = _rms_scale(x_ref[...])
        xkv_s[...] = (xh * gkv_ref[...]).astype(BF16)
        xq_s[...] = (xh * gq_ref[...]).astype(BF16)

    xkv = xkv_s[...]
    k = jnp.dot(xkv, wk_ref[...], preferred_element_type=F32)
    v = jnp.dot(xkv, wv_ref[...], preferred_element_type=F32)
    k_out[...] = k
    v_out[...] = v
    kb_out[...] = k.astype(BF16)
    vb_out[...] = v.astype(BF16)
    q = jnp.dot(xq_s[...], wq_ref[...], preferred_element_type=F32)
    q_out[...] = (q * q_scale).astype(BF16)


def _kvq(x, gkv, gq, wkv, wq):
    M, D = x.shape
    N = wq.shape[1]
    tm = _tile(M, ROW_TILE)
    tn = _tile(N, COL_TILE)
    nn = N // tn
    row = lambda i, j: (0, 0)
    blk = pl.BlockSpec((tm, tn), lambda i, j: (i, j))
    return pl.pallas_call(
        functools.partial(_kvq_kernel, q_scale=float(SB_HEAD_DIM) ** -0.5 * LOG2_E),
        grid=(M // tm, nn),
        in_specs=[pl.BlockSpec((tm, D), lambda i, j: (i, 0)),
                  pl.BlockSpec((1, D), row), pl.BlockSpec((1, D), row),
                  pl.BlockSpec((D, tn), lambda i, j: (0, j)),
                  pl.BlockSpec((D, tn), lambda i, j: (0, j + nn)),
                  pl.BlockSpec((D, tn), lambda i, j: (0, j))],
        out_specs=[blk] * 5,
        out_shape=[jax.ShapeDtypeStruct((M, N), F32), jax.ShapeDtypeStruct((M, N), F32),
                   jax.ShapeDtypeStruct((M, N), BF16), jax.ShapeDtypeStruct((M, N), BF16),
                   jax.ShapeDtypeStruct((M, N), BF16)],
        scratch_shapes=[pltpu.VMEM((tm, D), BF16), pltpu.VMEM((tm, D), BF16)],
        compiler_params=_params(("parallel", "arbitrary")),
        name="kvq_proj",
    )(x, gkv.reshape(1, D), gq.reshape(1, D), wkv, wkv, wq)


SB_DEAD = 152.0


def _sb_block(q, kb, vb, tri, carry, valid, on=None):
    z = _dot_nt(q, kb)
    sp = jnp.maximum(z, 0.0) + jnp.log2(1.0 + jnp.exp2(jnp.minimum(z, -z)))
    if valid is not None:
        sp = jnp.where(valid, sp, 0.0)
    if on is not None:
        sp = jnp.where(on, sp, 0.0)
    later = jnp.dot(sp.astype(BF16), tri, preferred_element_type=F32)
    logw = z - sp - later
    if carry is not None:
        logw = logw - carry
    w = jnp.exp2(logw)
    if valid is not None:
        w = jnp.where(valid, w, 0.0)
    if on is not None:
        w = jnp.where(on, w, 0.0)
    out = jnp.dot(w.astype(BF16), vb.astype(BF16), preferred_element_type=F32)
    return out, later[:, 0:1] + sp[:, 0:1]


def _sb_prompt_kernel(q_ref, k_ref, v_ref, tri_ref, o_ref, acc_s, carry_s, *, nsub, tb):
    qi = pl.program_id(2)
    blk0 = qi * nsub
    tri = tri_ref[...]
    valid = _col_ids((tb, tb)) < _row_ids((tb, tb))
    rows = lambda i: slice(i * tb, (i + 1) * tb)

    def kv_block(j):
        k0 = pl.multiple_of(j * tb, tb)
        return k_ref[pl.ds(k0, tb), :], v_ref[pl.ds(k0, tb), :]

    kvs = [kv_block(blk0 + i) for i in range(nsub)]
    acc, car = [], []
    for i in range(nsub):
        out, tot = _sb_block(q_ref[rows(i), :], kvs[i][0], kvs[i][1], tri, None, valid)
        acc.append(out)
        car.append(tot)
    for i in range(nsub):
        if i == 0:
            kb, vb = kv_block(jnp.maximum(blk0 - 1, 0))
            out, tot = _sb_block(q_ref[rows(0), :], kb, vb, tri, car[0], None, on=blk0 > 0)
        else:
            out, tot = _sb_block(q_ref[rows(i), :], kvs[i - 1][0], kvs[i - 1][1], tri,
                                 car[i], None)
        acc_s[i] = acc[i] + out
        carry_s[i] = car[i] + tot

    def live_min(t):
        m = jnp.float32(jnp.inf)
        for i in range(nsub):
            m = jnp.where(blk0 + i - 2 - t >= 0, jnp.minimum(m, jnp.min(carry_s[i])), m)
        return m

    def cond(st):
        t, m = st
        return jnp.logical_and(blk0 + nsub - 3 - t >= 0, m < SB_DEAD)

    def body(st):
        t, _ = st
        for i in range(nsub):
            j = blk0 + i - 2 - t
            kb, vb = kv_block(jnp.maximum(j, 0))
            out, tot = _sb_block(q_ref[rows(i), :], kb, vb, tri, carry_s[i], None, on=j >= 0)
            acc_s[i] += out
            carry_s[i] += tot
        return t + 1, live_min(t + 1)

    lax.while_loop(cond, body, (jnp.int32(0), live_min(0)))
    for i in range(nsub):
        o_ref[rows(i), :] = acc_s[i].astype(o_ref.dtype)


def _sb_decode_kernel(q_ref, k_ref, v_ref, kc_hbm, vc_hbm, trid_ref, trip_ref, o_ref,
                      acc_s, carry_s, near_k, near_v, near_sem, old_k, old_v, old_sem,
                      *, tq, tk, n_heads, n_old):
    b = pl.program_id(0)
    slot = b % 2
    hd = SB_HEAD_DIM
    newest = n_old * tk
    heads = range(n_heads)
    lanes = lambda h: slice(h * hd, (h + 1) * hd)

    def block_copies(row, k0, k_dst, v_dst, sem, first):
        cps = []
        for h in heads:
            cps.append(pltpu.make_async_copy(kc_hbm.at[row, pl.ds(k0, tk), h, :], k_dst.at[h],
                                             sem.at[first]))
            cps.append(pltpu.make_async_copy(vc_hbm.at[row, pl.ds(k0, tk), h, :], v_dst.at[h],
                                             sem.at[first + 1]))
        return cps

    @pl.when(b == 0)
    def _():
        for c in block_copies(0, newest, near_k.at[0], near_v.at[0], near_sem, 0):
            c.start()

    @pl.when(b + 1 < pl.num_programs(0))
    def _():
        for c in block_copies(b + 1, newest, near_k.at[1 - slot], near_v.at[1 - slot],
                              near_sem, 2 * (1 - slot)):
            c.start()

    valid = _col_ids((tq, tq)) < _row_ids((tq, tq))
    trid = trid_ref[...]
    trip = trip_ref[...]
    q = [q_ref[:, lanes(h)] for h in heads]
    own = [_sb_block(q[h], k_ref[:, lanes(h)], v_ref[:, lanes(h)], trid, None, valid)
           for h in heads]
    for c in block_copies(b, newest, near_k.at[slot], near_v.at[slot], near_sem, 2 * slot):
        c.wait()
    for h in heads:
        acc, carry = own[h]
        out, tot = _sb_block(q[h], near_k[slot, h], near_v[slot, h], trip, carry, None)
        acc_s[h] = acc + out
        carry_s[h] = carry + tot

    def cond(st):
        t, m = st
        return jnp.logical_and(t < n_old, m < SB_DEAD)

    def body(st):
        t, _ = st
        k0 = pl.multiple_of((n_old - 1 - t) * tk, tk)
        copies = block_copies(b, k0, old_k, old_v, old_sem, 0)
        for c in copies:
            c.start()
        for c in copies:
            c.wait()
        for h in heads:
            out, tot = _sb_block(q_ref[:, lanes(h)], old_k[h], old_v[h], trip, carry_s[h], None)
            acc_s[h] += out
            carry_s[h] += tot
        return t + 1, jnp.min(carry_s[...])

    lax.while_loop(cond, body, (jnp.int32(0), jnp.min(carry_s[...])))
    for h in heads:
        o_ref[:, lanes(h)] = acc_s[h].astype(o_ref.dtype)


def _suffix_ones(n):
    return (_row_ids((n, n)) > _col_ids((n, n))).astype(BF16)


SB_BLOCK = 256
SB_QSUB = 4


def _sb_attention(q, k, v, k_past=None, v_past=None):
    B, T, D = q.shape
    H = D // SB_HEAD_DIM
    hd = SB_HEAD_DIM
    full = lambda n: pl.BlockSpec((n, n), lambda b, h, i: (0, 0))
    seq = pl.BlockSpec((None, T, hd), lambda b, h, i: (b, 0, h))
    if k_past is None:
        tb = _tile(T, SB_BLOCK)
        nsub = _tile(T // tb, SB_QSUB)
        tq = nsub * tb
        qspec = pl.BlockSpec((None, tq, hd), lambda b, h, i: (b, i, h))
        return pl.pallas_call(
            functools.partial(_sb_prompt_kernel, nsub=nsub, tb=tb),
            grid=(B, H, T // tq),
            in_specs=[qspec, seq, seq, full(tb)],
            out_specs=qspec,
            out_shape=jax.ShapeDtypeStruct((B, T, D), BF16),
            scratch_shapes=[pltpu.VMEM((nsub, tb, hd), F32), pltpu.VMEM((nsub, tb, 1), F32)],
            compiler_params=_params(("parallel", "parallel", "arbitrary")),
            name="sb_attention",
        )(q, k, v, _suffix_ones(tb))
    P = k_past.shape[1]
    tk = _tile(P, SB_BLOCK)
    row = pl.BlockSpec((None, T, D), lambda b: (b, 0, 0))
    tri = lambda n: pl.BlockSpec((n, n), lambda b: (0, 0))
    hbm = pl.BlockSpec(memory_space=pl.ANY)
    return pl.pallas_call(
        functools.partial(_sb_decode_kernel, tq=T, tk=tk, n_heads=H, n_old=P // tk - 1),
        grid=(B,),
        in_specs=[row, row, row, hbm, hbm, tri(T), tri(tk)],
        out_specs=row,
        out_shape=jax.ShapeDtypeStruct((B, T, D), BF16),
        scratch_shapes=[pltpu.VMEM((H, T, hd), F32), pltpu.VMEM((H, T, 1), F32),
                        pltpu.VMEM((2, H, tk, hd), F32), pltpu.VMEM((2, H, tk, hd), F32),
                        pltpu.SemaphoreType.DMA((4,)),
                        pltpu.VMEM((H, tk, hd), F32), pltpu.VMEM((H, tk, hd), F32),
                        pltpu.SemaphoreType.DMA((2,))],
        compiler_params=_params(("arbitrary",)),
        name="sb_attention_decode",
    )(q, k, v, k_past, v_past, _suffix_ones(T), _suffix_ones(tk))


LATE_WEIGHTS = ('rw_wo', 'f_wup', 'f_wdown', 'w_kv', 'sb_wq', 'sb_wo')


def _trunk(x, shift0, wkv0, conv0, k_past, v_past, p, late_f32=None):
    B, T, D = x.shape
    M = B * T
    act5, lw, shift = _rw_proj(
        x, shift0[0], p['a_norm_g'][0], p['rw_mu'][0], p['rw_bias'], p['rw_w3'],
        p['rw_w1'], p['rw_a1'], p['rw_g1'], p['rw_lora2'])
    cast = () if late_f32 is None else tuple(
        late_f32[n].reshape(-1, late_f32[n].shape[-1]) for n in LATE_WEIGHTS)
    o, s_bd, casted = _wkv(act5, lw, _state_to_pairs(wkv0[0]),
                           p['rw_kk'][0], p['rw_ka'][0], p['rw_rk'][0].reshape(-1),
                           p['rw_lnx_g'][0], p['rw_lnx_b'][0], cast=cast)
    if late_f32 is not None:
        for n, w in zip(LATE_WEIGHTS, casted):
            p[n] = w.reshape(late_f32[n].shape)
    x, conv_a = _ffn(x, o, p['rw_wo'], conv0[0], p['f_norm_g'][0], p['f_wup'],
                     p['f_conv_w'][0], p['f_conv_b'][0], p['f_wdown'], p['out_norm_g'],
                     layer=0, final_norm=False)
    k_sh, v_sh, kb, vb, q = _kvq(x.reshape(M, D), p['kv_norm_g'], p['b_norm_g'][0],
                                 p['w_kv'], p['sb_wq'])
    d_att = q.shape[1]
    att = _sb_attention(q.reshape(B, T, d_att), kb.reshape(B, T, d_att), vb.reshape(B, T, d_att),
                        k_past, v_past)
    y, conv_b = _ffn(x, att, p['sb_wo'], conv0[1], p['f_norm_g'][1], p['f_wup'],
                     p['f_conv_w'][1], p['f_conv_b'][1], p['f_wdown'], p['out_norm_g'],
                     layer=1, final_norm=True)
    H = d_att // SB_HEAD_DIM
    return (y, _pairs_to_state(s_bd)[None], shift.reshape(1, B, D), jnp.stack([conv_a, conv_b]),
            k_sh.reshape(B, T, H, SB_HEAD_DIM), v_sh.reshape(B, T, H, SB_HEAD_DIM))


def kernel(x_prompt, x_sample, cache_k, cache_v, state_wkv, state_shift, state_conv, a_norm_g, rw_mu, rw_w0, rw_w1, rw_w2, rw_a0, rw_a1, rw_a2, rw_g1, rw_g2, rw_kk, rw_ka, rw_rk, rw_wr, rw_wk, rw_wv, rw_wo, rw_lnx_g, rw_lnx_b, kv_norm_g, w_kv, b_norm_g, sb_wq, sb_wo, f_norm_g, f_wup, f_conv_w, f_conv_b, f_wdown, out_norm_g):
    bf = lambda w: w.astype(BF16)
    p = dict(a_norm_g=a_norm_g, rw_mu=rw_mu,
             rw_bias=jnp.stack([rw_w0[0], rw_a0[0]]),
             rw_w3=jnp.stack([bf(rw_wr[0]), bf(rw_wk[0]), bf(rw_wv[0])]),
             rw_lora2=jnp.concatenate([bf(rw_w2[0]), bf(rw_a2[0]), bf(rw_g2[0])], axis=0),
             rw_w1=bf(rw_w1[0]), rw_a1=bf(rw_a1[0]), rw_g1=bf(rw_g1[0]),
             rw_kk=rw_kk, rw_ka=rw_ka, rw_rk=rw_rk,
             rw_lnx_g=rw_lnx_g, rw_lnx_b=rw_lnx_b, kv_norm_g=kv_norm_g,
             b_norm_g=b_norm_g, f_norm_g=f_norm_g,
             f_conv_w=f_conv_w, f_conv_b=f_conv_b, out_norm_g=out_norm_g)
    late_f32 = dict(rw_wo=rw_wo[0], f_wup=f_wup, f_wdown=f_wdown, w_kv=w_kv,
                    sb_wq=sb_wq[0], sb_wo=sb_wo[0])
    B, _, D = x_prompt.shape
    n_a = state_shift.shape[0]
    depth = state_conv.shape[0]
    F = state_conv.shape[-1]
    H = state_wkv.shape[2]
    shift0 = jnp.zeros((n_a, B, D), x_prompt.dtype)
    wkv0 = jnp.zeros((n_a, B, H, RW_HEAD, RW_HEAD), F32)
    conv0 = jnp.zeros((depth, B, CONV_W - 1, F), x_prompt.dtype)
    y_p, wkv_p, shift_p, conv_p, k_p, v_p = _trunk(x_prompt, shift0, wkv0, conv0, None, None, p,
                                                   late_f32=late_f32)
    y_s, wkv_s, shift_s, conv_s, k_s, v_s = _trunk(
        x_sample, state_shift, state_wkv.astype(F32), state_conv, cache_k, cache_v, p)
    return (y_p, y_s, wkv_p.astype(state_wkv.dtype), shift_p, conv_p, k_p, v_p,
            wkv_s.astype(state_wkv.dtype), shift_s, conv_s, k_s, v_s)
```

```python
import functools

import jax
import jax.numpy as jnp
from jax import lax
from jax.experimental import pallas as pl
from jax.experimental.pallas import tpu as pltpu

F32 = jnp.float32
BF16 = jnp.bfloat16

RW_HEAD = 64
PAIR = 2 * RW_HEAD
SB_HEAD_DIM = 128
GN_EPS = 64e-5
NORM_EPS = 1e-6
CONV_W = 3
RW_PROJ_DTYPES = (BF16, BF16, BF16, F32, BF16, BF16)
BF16_ROWS = 16
WKV_CHUNK = 64
NEUMANN_BLOCK = 16
WKV_CHUNKS_PER_STEP = 2
LOG2_E = 1.4426950408889634
ROW_TILE = 512
COL_TILE = 512
VMEM_LIMIT = 56 * 1024 * 1024
VMEM_LIMIT_SMALL = 32 * 1024 * 1024


def _tile(n, pref):
    if n <= pref:
        return n
    t = pref
    while n % t:
        t //= 2
    return t


def _row_tiling(B, T):
    if T >= ROW_TILE:
        return B, 1, _tile(T, ROW_TILE)
    nseq = _tile(B, max(ROW_TILE // T, 1))
    return B // nseq, nseq, nseq * T


def _params(sem, rows=None):
    small = rows is not None and rows <= ROW_TILE
    return pltpu.CompilerParams(dimension_semantics=sem,
                                vmem_limit_bytes=VMEM_LIMIT_SMALL if small else VMEM_LIMIT)


def _dot(a, b):
    return jnp.dot(a.astype(BF16), b.astype(BF16), preferred_element_type=F32)


def _dot_nt(a, b):
    return lax.dot_general(a.astype(BF16), b.astype(BF16), (((1,), (1,)), ((), ())),
                           preferred_element_type=F32)


def _dot_tn(a, b):
    return lax.dot_general(a.astype(BF16), b.astype(BF16), (((0,), (0,)), ((), ())),
                           preferred_element_type=F32)


def _rms_scale(x):
    return x * lax.rsqrt(jnp.mean(x * x, axis=-1, keepdims=True) + NORM_EPS)


def _softplus(u):
    return jnp.maximum(u, 0.0) + jnp.log1p(jnp.exp(-jnp.abs(u)))


def _row_ids(shape):
    return lax.broadcasted_iota(jnp.int32, shape, 0)


def _col_ids(shape):
    return lax.broadcasted_iota(jnp.int32, shape, 1)


def _rw_proj_kernel(x_ref, shift0_ref, ng_ref, mu_ref, w0_ref, a0_ref,
                    wr_ref, wk_ref, wv_ref, w1_ref, w2_ref, a1_ref, a2_ref, g1_ref, g2_ref,
                    r_out, k_out, v_out, lw_out, as_out, g_out, shift_out,
                    xr_s, xk_s, xv_s, hw_s, ha_s, hg_s, carry_s, *, nseq, seq_rows):
    t = pl.program_id(1)
    j = pl.program_id(2)

    @pl.when(j == 0)
    def _():
        @pl.when(t == 0)
        def _():
            carry_s[...] = shift0_ref[0]

        xn = _rms_scale(x_ref[...]) * ng_ref[...]
        rows = _row_ids(xn.shape)
        x_prev = jnp.where(rows == 0, carry_s[...], pltpu.roll(xn, 1, axis=0))
        for s in range(1, nseq):
            x_prev = jnp.where(rows == s * seq_rows, shift0_ref[s], x_prev)
        for s in range(nseq):
            shift_out[s] = xn[(s + 1) * seq_rows - 1:(s + 1) * seq_rows, :]
        carry_s[...] = xn[nseq * seq_rows - 1:nseq * seq_rows, :]
        xx = x_prev - xn
        mix = lambda i: (xn + xx * mu_ref[i:i + 1, :]).astype(BF16)
        xr_s[...] = mix(0)
        xk_s[...] = mix(2)
        xv_s[...] = mix(3)
        hw_s[...] = jnp.tanh(_dot(mix(1), w1_ref[...])).astype(BF16)
        ha_s[...] = _dot(mix(4), a1_ref[...]).astype(BF16)
        hg_s[...] = jax.nn.sigmoid(_dot(mix(5), g1_ref[...])).astype(BF16)

    r_out[...] = _dot(xr_s[...], wr_ref[...]).astype(r_out.dtype)
    k_out[...] = _dot(xk_s[...], wk_ref[...]).astype(k_out.dtype)
    v_out[...] = _dot(xv_s[...], wv_ref[...]).astype(v_out.dtype)
    w_log = -_softplus(-(w0_ref[...] + _dot(hw_s[...], w2_ref[...]))) - 0.5
    lw_out[...] = -jnp.exp(w_log)
    as_out[...] = jax.nn.sigmoid(a0_ref[...] + _dot(ha_s[...], a2_ref[...])).astype(as_out.dtype)
    g_out[...] = _dot(hg_s[...], g2_ref[...]).astype(g_out.dtype)


def _rw_proj(x, shift0, ng, mu, w0, a0, wr, wk, wv, w1, w2, a1, a2, g1, g2):
    B, T, D = x.shape
    G, nseq, tm = _row_tiling(B, T)
    R = B * T // G
    tn = _tile(D, COL_TILE)
    lw_dim, la_dim, lg_dim = w1.shape[1], a1.shape[1], g1.shape[1]
    row = lambda b, t, j: (0, 0)
    colv = pl.BlockSpec((1, tn), lambda b, t, j: (0, j))
    act = pl.BlockSpec((None, tm, tn), lambda b, t, j: (b, t, j))
    outs = pl.pallas_call(
        functools.partial(_rw_proj_kernel, nseq=nseq, seq_rows=tm // nseq),
        grid=(G, R // tm, D // tn),
        in_specs=[
            pl.BlockSpec((None, tm, D), lambda b, t, j: (b, t, 0)),
            pl.BlockSpec((nseq, 1, D), lambda b, t, j: (b, 0, 0)),
            pl.BlockSpec((1, D), row),
            pl.BlockSpec((6, D), row),
            colv, colv,
            pl.BlockSpec((D, tn), lambda b, t, j: (0, j)),
            pl.BlockSpec((D, tn), lambda b, t, j: (0, j)),
            pl.BlockSpec((D, tn), lambda b, t, j: (0, j)),
            pl.BlockSpec((D, lw_dim), row),
            pl.BlockSpec((lw_dim, tn), lambda b, t, j: (0, j)),
            pl.BlockSpec((D, la_dim), row),
            pl.BlockSpec((la_dim, tn), lambda b, t, j: (0, j)),
            pl.BlockSpec((D, lg_dim), row),
            pl.BlockSpec((lg_dim, tn), lambda b, t, j: (0, j)),
        ],
        out_specs=[act] * 6 + [pl.BlockSpec((nseq, 1, D), lambda b, t, j: (b, 0, 0))],
        out_shape=[jax.ShapeDtypeStruct((G, R, D), dt) for dt in RW_PROJ_DTYPES]
        + [jax.ShapeDtypeStruct((B, 1, D), F32)],
        scratch_shapes=[
            pltpu.VMEM((tm, D), BF16), pltpu.VMEM((tm, D), BF16), pltpu.VMEM((tm, D), BF16),
            pltpu.VMEM((tm, lw_dim), BF16), pltpu.VMEM((tm, la_dim), BF16),
            pltpu.VMEM((tm, lg_dim), BF16), pltpu.VMEM((1, D), F32),
        ],
        compiler_params=_params(("arbitrary", "arbitrary", "arbitrary"), B * T),
        name="rw_proj",
    )(x.reshape(G, R, D), shift0.reshape(B, 1, D), ng.reshape(1, D), mu, w0.reshape(1, D),
      a0.reshape(1, D), wr, wk, wv, w1, w2, a1, a2, g1, g2)
    return [o.reshape(B, T, D) for o in outs[:6]] + [outs[6]]


def _wkv_kernel(r_ref, lw_ref, k_ref, v_ref, as_ref, g_ref, s0_ref,
                kkw_ref, kaw_ref, rk_ref, lng_ref, lnb_ref, *rest,
                chunk, n_sub, n_pairs, n_cast):
    cast_in = rest[:n_cast]
    o_ref, s_ref = rest[n_cast:n_cast + 2]
    cast_out = rest[n_cast + 2:2 * n_cast + 2]
    c_s = rest[-1]
    C = chunk
    C2 = 2 * C
    trow = lambda sc: slice(sc * C, (sc + 1) * C)

    for w_in, w_out in zip(cast_in, cast_out):
        w_out[...] = w_in[...].astype(w_out.dtype)

    @pl.when(pl.program_id(1) == 0)
    def _():
        s_ref[...] = s0_ref[...]

    tri = (_col_ids((C, C)) <= _row_ids((C, C))).astype(BF16)
    for sc in range(n_sub):
        lw_all = lw_ref[trow(sc), :]
        lw_hi = lw_all.astype(BF16)
        lw_r = lw_all - lw_hi.astype(F32)
        lw_mid = lw_r.astype(BF16)
        lw_lo = (lw_r - lw_mid.astype(F32)).astype(BF16)
        c_s[trow(sc), :] = (jnp.dot(tri, lw_hi, preferred_element_type=F32)
                            + jnp.dot(tri, lw_mid, preferred_element_type=F32)
                            + jnp.dot(tri, lw_lo, preferred_element_type=F32))

    head0 = _col_ids((C, PAIR)) < RW_HEAD
    ones_bd = ((_row_ids((PAIR, PAIR)) // RW_HEAD) == (_col_ids((PAIR, PAIR)) // RW_HEAD))
    ones_bd_bf = ones_bd.astype(BF16)

    rr = _row_ids((C2, C2))
    cc = _col_ids((C2, C2))
    same_head = (rr // C) == (cc // C)
    nb = NEUMANN_BLOCK
    same16 = (rr // nb) == (cc // nb)
    same32 = (rr // (2 * nb)) == (cc // (2 * nb))
    m_e1 = same32 & (~same16) if C >= 2 * nb else None
    m_e2 = same_head & (~same32) if C >= 4 * nb else None
    eye22 = (rr == cc).astype(F32)
    r12 = _row_ids((C, C2))
    c12 = _col_ids((C, C2))
    strict12 = (c12 % C) < r12
    incl12 = (c12 % C) <= r12
    left12 = c12 < C
    same16_c = ((c12 % C) // nb) == (r12 // nb)
    c16 = _col_ids((nb, C2))
    blk_of_lane = (c16 % C) // nb
    eye16 = ((c16 % nb) == _row_ids((nb, C2))).astype(F32)

    def expand(x):
        return jnp.where(same_head, jnp.concatenate([x, x], axis=0), 0.0)

    def expand16(x):
        return jnp.where(same16, jnp.concatenate([x] * (C2 // nb), axis=0), 0.0)

    def group_sum(x):
        return jnp.dot(x.astype(BF16), ones_bd_bf, preferred_element_type=F32)

    cat0 = lambda xs: jnp.concatenate(xs, axis=0)
    zero_h1 = lambda x: jnp.where(head0, x, 0.0)
    zero_h0 = lambda x: jnp.where(head0, 0.0, x)
    inv_n = 1.0 / RW_HEAD

    def state_free(items):
        n = range(len(items))
        tr = [trow(sc) for sc, _ in items]
        sl = [slice(p * PAIR, (p + 1) * PAIR) for _, p in items]
        r = [r_ref[tr[i], sl[i]].astype(F32) for i in n]
        lw = [lw_ref[tr[i], sl[i]] for i in n]
        k = [k_ref[tr[i], sl[i]].astype(F32) for i in n]
        v = [v_ref[tr[i], sl[i]].astype(F32) for i in n]
        asig = [as_ref[tr[i], sl[i]].astype(F32) for i in n]
        c = [c_s[tr[i], sl[i]] for i in n]
        c_end = [x[C - 1:C, :] for x in c]

        kkv = [k[i] * kkw_ref[:, sl[i]] for i in n]
        ss = [group_sum(x * x) for x in kkv]
        kk = [kkv[i] / jnp.maximum(jnp.sqrt(ss[i]), 1e-12) for i in n]
        b_in = [kk[i] * asig[i] for i in n]
        k_in = [k[i] * (1.0 + (asig[i] - 1.0) * kaw_ref[:, sl[i]]) for i in n]
        bonus_s = [group_sum(r[i] * k_in[i] * rk_ref[:, sl[i]]) for i in n]

        e_neg = [jnp.exp(-x) for x in c]
        a_t = [(-kk[i]) * jnp.exp(c[i] - lw[i]) for i in n]
        r_t = [r[i] * jnp.exp(c[i]) for i in n]
        b_h = [b_in[i] * e_neg[i] for i in n]
        k_h = [k_in[i] * e_neg[i] for i in n]
        e_end = [jnp.exp(c_end[i] - c[i]) for i in n]
        RE = [cat0([b_in[i] * e_end[i], k_in[i] * e_end[i]]) for i in n]

        L = [cat0([a_t[i], r_t[i]]) for i in n]
        G = [_dot_nt(L[i], cat0([zero_h1(b_h[i]), zero_h1(k_h[i]),
                                 zero_h0(k_h[i]), zero_h0(b_h[i])])) for i in n]
        G0 = [x[:, :C2] for x in G]
        G1 = [x[:, C2:] for x in G]
        g0t = [x[:C] for x in G0]
        g1t = [x[:C] for x in G1]

        n_c = [jnp.where(strict12, jnp.where(left12, g0t[i], g1t[i]), 0.0) for i in n]
        n_bd = [expand(x) for x in n_c]
        d1 = [sum(jnp.where(blk_of_lane == b, x[b * nb:(b + 1) * nb], 0.0)
                  for b in range(C // nb)) for x in n_c]
        d1_bd = [expand16(x) for x in d1]
        d2 = [_dot(d1[i], d1_bd[i]) for i in n]
        d2_bd = [expand16(x) for x in d2]
        d4 = [_dot(d2[i], d2_bd[i]) for i in n]
        d4_bd = [expand16(x) for x in d4]
        d8 = [_dot(d4[i], d4_bd[i]) for i in n]
        pa = [_dot(eye16 + d1[i], eye22 + d2_bd[i]) for i in n]
        pb = [_dot(eye16 + d4[i], eye22 + expand16(d8[i])) for i in n]
        t16 = [_dot(pa[i], expand16(pb[i])) for i in n]
        tcat = [jnp.where(same16_c, cat0([x] * (C // nb)), 0.0) for x in t16]
        for m_e in (m_e1, m_e2):
            if m_e is not None:
                t_bd = [expand(x) for x in tcat]
                x = [_dot(tcat[i], jnp.where(m_e, n_bd[i], 0.0)) for i in n]
                tcat = [tcat[i] + _dot(x[i], t_bd[i]) for i in n]

        v0 = [jnp.where(head0, x, 0.0) for x in v]
        v1 = [jnp.where(head0, 0.0, x) for x in v]
        av = [_dot(jnp.where(strict12, jnp.where(left12, g1t[i], g0t[i]), 0.0),
                   cat0([v1[i], v0[i]])) for i in n]
        bot = [jnp.concatenate([jnp.where(incl12, G0[i][C:], 0.0),
                                jnp.where(incl12, G1[i][C:], 0.0)], axis=1) for i in n]
        return dict(L=L, tcat=tcat, av=av, bot=bot, RE=RE, v=v, v0=v0, v1=v1,
                    decay=[jnp.exp(x) for x in c_end], bonus=bonus_s)

    def state_chain(sc, pre, sel):
        n = range(len(sel))
        at = lambda name: [pre[name][j] for j in sel]
        L, tcat, av, bot, RE, v, v0, v1, decay, bonus_s = (
            at(x) for x in ("L", "tcat", "av", "bot", "RE", "v", "v0", "v1", "decay", "bonus"))
        sl = [slice(p * PAIR, (p + 1) * PAIR) for p in range(n_pairs)]
        S = [s_ref[p] for p in range(n_pairs)]
        P = [_dot_nt(L[i], S[i]) for i in n]
        W = [P[i][:C] + av[i] for i in n]
        U = [_dot(tcat[i], cat0([zero_h1(W[i]), zero_h0(W[i])])) for i in n]
        Y = [P[i][C:] + _dot(bot[i], cat0([zero_h1(U[i]), v0[i], v1[i], zero_h0(U[i])]))
             for i in n]
        for i in n:
            upd = _dot_tn(cat0([U[i], v[i]]), RE[i])
            s_ref[i] = S[i] * decay[i] + jnp.where(ones_bd, upd, 0.0)

        mean = [group_sum(x) * inv_n for x in Y]
        dlt = [Y[i] - mean[i] for i in n]
        var = [group_sum(x * x) * inv_n for x in dlt]
        for i in n:
            yn = dlt[i] * lax.rsqrt(var[i] + GN_EPS) * lng_ref[:, sl[i]] + lnb_ref[:, sl[i]]
            out = (yn + bonus_s[i] * v[i]) * g_ref[trow(sc), sl[i]].astype(F32)
            o_ref[trow(sc), sl[i]] = out.astype(o_ref.dtype)

    items = [(sc, p) for sc in range(n_sub) for p in range(n_pairs)]
    pre = state_free(items)
    for sc in range(n_sub):
        state_chain(sc, pre, [sc * n_pairs + p for p in range(n_pairs)])


def _cast_slab(rows, n_steps):
    share = 1
    while rows * share % (n_steps * BF16_ROWS):
        share *= 2
    return rows * share // n_steps, share


def _wkv(r, lw, k, v, asig, g, s0_bd, kkw, kaw, rk, lng, lnb, cast=()):
    B, T, D = r.shape
    C = _tile(T, WKV_CHUNK)
    n_sub = _tile(T // C, WKV_CHUNKS_PER_STEP)
    n_pairs = D // PAIR
    steps_t = T // (n_sub * C)
    act = pl.BlockSpec((None, n_sub * C, D), lambda b, t: (b, t, 0))
    vec = pl.BlockSpec((1, D), lambda b, t: (0, 0))
    st = pl.BlockSpec((None, n_pairs, PAIR, PAIR), lambda b, t: (b, 0, 0, 0))
    cast_specs = []
    for w in cast:
        slab, share = _cast_slab(w.shape[0], B * steps_t)
        cast_specs.append(pl.BlockSpec(
            (slab, w.shape[1]), lambda b, t, share=share: ((b * steps_t + t) // share, 0)))
    outs = pl.pallas_call(
        functools.partial(_wkv_kernel, chunk=C, n_sub=n_sub, n_pairs=n_pairs,
                          n_cast=len(cast)),
        grid=(B, steps_t),
        in_specs=[act] * 6 + [st] + [vec] * 5 + cast_specs,
        out_specs=[act, st] + cast_specs,
        out_shape=[jax.ShapeDtypeStruct((B, T, D), BF16),
                   jax.ShapeDtypeStruct((B, n_pairs, PAIR, PAIR), F32)]
        + [jax.ShapeDtypeStruct(w.shape, BF16) for w in cast],
        scratch_shapes=[pltpu.VMEM((n_sub * C, D), F32)],
        compiler_params=_params(("arbitrary", "arbitrary"), B * T),
        name="wkv",
    )(r, lw, k, v, asig, g, s0_bd, kkw.reshape(1, D), kaw.reshape(1, D), rk.reshape(1, D),
      lng.reshape(1, D), lnb.reshape(1, D), *cast)
    return outs[0], outs[1], outs[2:]


def _state_to_pairs(s):
    B, H, n, _ = s.shape
    s = s.reshape(B, H // 2, 2, n, n)
    z = jnp.zeros((B, H // 2, n, n), s.dtype)
    top = jnp.concatenate([s[:, :, 0], z], axis=-1)
    bot = jnp.concatenate([z, s[:, :, 1]], axis=-1)
    return jnp.concatenate([top, bot], axis=-2)


def _pairs_to_state(sp):
    B, P, _, _ = sp.shape
    n = RW_HEAD
    h0 = sp[:, :, :n, :n]
    h1 = sp[:, :, n:, n:]
    return jnp.stack([h0, h1], axis=2).reshape(B, 2 * P, n, n)


def _ffn_kernel(x_ref, a_ref, wo_ref, conv0_ref, ng_ref, wg_ref, wv_ref, cw_ref, cb_ref,
                wd_ref, og_ref, o_ref, conv_out, xn_s, acc_s, carry_s,
                *, final_norm, nseq, seq_rows):
    t = pl.program_id(1)
    f = pl.program_id(2)
    nf = pl.num_programs(2)
    keep = CONV_W - 1

    @pl.when(f == 0)
    def _():
        x1 = x_ref[...] + jnp.dot(a_ref[...], wo_ref[...], preferred_element_type=F32)
        acc_s[...] = x1
        xn_s[...] = (_rms_scale(x1) * ng_ref[...]).astype(BF16)

    @pl.when(t == 0)
    def _():
        carry_s[f] = conv0_ref[0]

    xn = xn_s[...]
    gate = jnp.dot(xn, wg_ref[...], preferred_element_type=F32)
    val = jnp.dot(xn, wv_ref[...], preferred_element_type=F32)
    prev = carry_s[f]
    rows = _row_ids(gate.shape)
    g1 = jnp.where(rows == 0, prev[1:2, :], pltpu.roll(gate, 1, axis=0))
    g2 = jnp.where(rows == 0, prev[0:1, :],
                   jnp.where(rows == 1, prev[1:2, :], pltpu.roll(gate, 2, axis=0)))
    for s in range(1, nseq):
        first = s * seq_rows
        prev_s = conv0_ref[s]
        g1 = jnp.where(rows == first, prev_s[1:2, :], g1)
        g2 = jnp.where(rows == first, prev_s[0:1, :],
                       jnp.where(rows == first + 1, prev_s[1:2, :], g2))
    c = cb_ref[...] + cw_ref[0:1, :] * g2
    c = c + cw_ref[1:2, :] * g1
    c = c + cw_ref[2:3, :] * gate
    y = (c * jax.nn.sigmoid(c)) * val
    acc_s[...] += jnp.dot(y.astype(BF16), wd_ref[...], preferred_element_type=F32)
    tf = gate.shape[1]
    cols = pl.ds(pl.multiple_of(f * tf, tf), tf)
    for s in range(nseq):
        conv_out[s, :, cols] = gate[(s + 1) * seq_rows - keep:(s + 1) * seq_rows, :]
    carry_s[f] = gate[nseq * seq_rows - keep:nseq * seq_rows, :]

    @pl.when(f == nf - 1)
    def _():
        out = acc_s[...]
        if final_norm:
            out = _rms_scale(out) * og_ref[...]
        o_ref[...] = out


def _ffn(x, a, wo, conv0, ng, wup, cw, cb, wdown, og, *, layer, final_norm):
    B, T, D = x.shape
    F = wdown.shape[1]
    Ka = a.shape[-1]
    G, nseq, tm = _row_tiling(B, T)
    R = B * T // G
    tf = _tile(F, COL_TILE)
    nf = F // tf
    keep = CONV_W - 1
    row = lambda b, t, f: (0, 0)
    out, conv = pl.pallas_call(
        functools.partial(_ffn_kernel, final_norm=final_norm, nseq=nseq, seq_rows=tm // nseq),
        grid=(G, R // tm, nf),
        in_specs=[
            pl.BlockSpec((None, tm, D), lambda b, t, f: (b, t, 0)),
            pl.BlockSpec((None, tm, Ka), lambda b, t, f: (b, t, 0)),
            pl.BlockSpec((Ka, D), row, pipeline_mode=pl.Buffered(1)),
            pl.BlockSpec((nseq, keep, tf), lambda b, t, f: (b, 0, f)),
            pl.BlockSpec((1, D), row),
            pl.BlockSpec((None, D, tf), lambda b, t, f: (layer, 0, f)),
            pl.BlockSpec((None, D, tf), lambda b, t, f: (layer, 0, f + nf)),
            pl.BlockSpec((CONV_W, tf), lambda b, t, f: (0, f)),
            pl.BlockSpec((1, tf), lambda b, t, f: (0, f)),
            pl.BlockSpec((None, tf, D), lambda b, t, f: (layer, f, 0)),
            pl.BlockSpec((1, D), row),
        ],
        out_specs=[pl.BlockSpec((None, tm, D), lambda b, t, f: (b, t, 0)),
                   pl.BlockSpec((nseq, keep, F), lambda b, t, f: (b, 0, 0))],
        out_shape=[jax.ShapeDtypeStruct((G, R, D), F32),
                   jax.ShapeDtypeStruct((B, keep, F), F32)],
        scratch_shapes=[pltpu.VMEM((tm, D), BF16), pltpu.VMEM((tm, D), F32),
                        pltpu.VMEM((nf, keep, tf), F32)],
        compiler_params=_params(("arbitrary", "arbitrary", "arbitrary"), B * T),
        name="conv_ffn",
    )(x.reshape(G, R, D), a.reshape(G, R, Ka), wo, conv0, ng.reshape(1, D), wup, wup, cw,
      cb.reshape(1, F), wdown, og.reshape(1, D))
    return out.reshape(B, T, D), conv


def _kvq_kernel(x_ref, gkv_ref, gq_ref, wk_ref, wv_ref, wq_ref,
                k_out, v_out, kb_out, vb_out, q_out, xkv_s, xq_s, *, q_scale):
    @pl.when(pl.program_id(1) == 0)
    def _():
        xh = _rms_scale(x_ref[...])
        xkv_s[...] = (xh * gkv_ref[...]).astype(BF16)
        xq_s[...] = (xh * gq_ref[...]).astype(BF16)

    xkv = xkv_s[...]
    k = jnp.dot(xkv, wk_ref[...], preferred_element_type=F32)
    v = jnp.dot(xkv, wv_ref[...], preferred_element_type=F32)
    k_out[...] = k
    v_out[...] = v
    kb_out[...] = k.astype(BF16)
    vb_out[...] = v.astype(BF16)
    q = jnp.dot(xq_s[...], wq_ref[...], preferred_element_type=F32)
    q_out[...] = (q * q_scale).astype(BF16)


def _kvq(x, gkv, gq, wkv, wq):
    M, D = x.shape
    N = wq.shape[1]
    tm = _tile(M, ROW_TILE)
    tn = _tile(N, COL_TILE)
    nn = N // tn
    row = lambda i, j: (0, 0)
    blk = pl.BlockSpec((tm, tn), lambda i, j: (i, j))
    return pl.pallas_call(
        functools.partial(_kvq_kernel, q_scale=float(SB_HEAD_DIM) ** -0.5 * LOG2_E),
        grid=(M // tm, nn),
        in_specs=[pl.BlockSpec((tm, D), lambda i, j: (i, 0)),
                  pl.BlockSpec((1, D), row), pl.BlockSpec((1, D), row),
                  pl.BlockSpec((D, tn), lambda i, j: (0, j)),
                  pl.BlockSpec((D, tn), lambda i, j: (0, j + nn)),
                  pl.BlockSpec((D, tn), lambda i, j: (0, j))],
        out_specs=[blk] * 5,
        out_shape=[jax.ShapeDtypeStruct((M, N), F32), jax.ShapeDtypeStruct((M, N), F32),
                   jax.ShapeDtypeStruct((M, N), BF16), jax.ShapeDtypeStruct((M, N), BF16),
                   jax.ShapeDtypeStruct((M, N), BF16)],
        scratch_shapes=[pltpu.VMEM((tm, D), BF16), pltpu.VMEM((tm, D), BF16)],
        compiler_params=_params(("parallel", "arbitrary"), M),
        name="kvq_proj",
    )(x, gkv.reshape(1, D), gq.reshape(1, D), wkv, wkv, wq)


SB_DEAD = 152.0


def _sb_block(q, kb, vb, tri, carry, valid, on=None):
    z = _dot_nt(q, kb)
    sp = jnp.maximum(z, 0.0) + jnp.log2(1.0 + jnp.exp2(jnp.minimum(z, -z)))
    if valid is not None:
        sp = jnp.where(valid, sp, 0.0)
    if on is not None:
        sp = jnp.where(on, sp, 0.0)
    later = jnp.dot(sp.astype(BF16), tri, preferred_element_type=F32)
    logw = z - sp - later
    if carry is not None:
        logw = logw - carry
    w = jnp.exp2(logw)
    if valid is not None:
        w = jnp.where(valid, w, 0.0)
    if on is not None:
        w = jnp.where(on, w, 0.0)
    out = jnp.dot(w.astype(BF16), vb.astype(BF16), preferred_element_type=F32)
    return out, later[:, 0:1] + sp[:, 0:1]


def _sb_prompt_kernel(q_ref, k_ref, v_ref, tri_ref, o_ref, acc_s, carry_s, *, nsub, tb):
    qi = pl.program_id(2)
    blk0 = qi * nsub
    tri = tri_ref[...]
    valid = _col_ids((tb, tb)) < _row_ids((tb, tb))
    rows = lambda i: slice(i * tb, (i + 1) * tb)

    def kv_block(j):
        k0 = pl.multiple_of(j * tb, tb)
        return k_ref[pl.ds(k0, tb), :], v_ref[pl.ds(k0, tb), :]

    kvs = [kv_block(blk0 + i) for i in range(nsub)]
    acc, car = [], []
    for i in range(nsub):
        out, tot = _sb_block(q_ref[rows(i), :], kvs[i][0], kvs[i][1], tri, None, valid)
        acc.append(out)
        car.append(tot)
    for i in range(nsub):
        if i == 0:
            kb, vb = kv_block(jnp.maximum(blk0 - 1, 0))
            out, tot = _sb_block(q_ref[rows(0), :], kb, vb, tri, car[0], None, on=blk0 > 0)
        else:
            out, tot = _sb_block(q_ref[rows(i), :], kvs[i - 1][0], kvs[i - 1][1], tri,
                                 car[i], None)
        acc_s[i] = acc[i] + out
        carry_s[i] = car[i] + tot

    def live_min(t):
        m = jnp.float32(jnp.inf)
        for i in range(nsub):
            m = jnp.where(blk0 + i - 2 - t >= 0, jnp.minimum(m, jnp.min(carry_s[i])), m)
        return m

    def cond(st):
        t, m = st
        return jnp.logical_and(blk0 + nsub - 3 - t >= 0, m < SB_DEAD)

    def body(st):
        t, _ = st
        for i in range(nsub):
            j = blk0 + i - 2 - t
            kb, vb = kv_block(jnp.maximum(j, 0))
            out, tot = _sb_block(q_ref[rows(i), :], kb, vb, tri, carry_s[i], None, on=j >= 0)
            acc_s[i] += out
            carry_s[i] += tot
        return t + 1, live_min(t + 1)

    lax.while_loop(cond, body, (jnp.int32(0), live_min(0)))
    for i in range(nsub):
        o_ref[rows(i), :] = acc_s[i].astype(o_ref.dtype)


def _sb_decode_kernel(q_ref, k_ref, v_ref, kc_hbm, vc_hbm, trid_ref, trip_ref, o_ref,
                      acc_s, carry_s, near_k, near_v, near_sem, old_k, old_v, old_sem,
                      *, tq, tk, n_heads, n_old):
    b = pl.program_id(0)
    slot = b % 2
    hd = SB_HEAD_DIM
    newest = n_old * tk
    heads = range(n_heads)
    lanes = lambda h: slice(h * hd, (h + 1) * hd)

    def block_copies(row, k0, k_dst, v_dst, sem, first):
        cps = []
        for h in heads:
            cps.append(pltpu.make_async_copy(kc_hbm.at[row, pl.ds(k0, tk), h, :], k_dst.at[h],
                                             sem.at[first]))
            cps.append(pltpu.make_async_copy(vc_hbm.at[row, pl.ds(k0, tk), h, :], v_dst.at[h],
                                             sem.at[first + 1]))
        return cps

    @pl.when(b == 0)
    def _():
        for c in block_copies(0, newest, near_k.at[0], near_v.at[0], near_sem, 0):
            c.start()

    @pl.when(b + 1 < pl.num_programs(0))
    def _():
        for c in block_copies(b + 1, newest, near_k.at[1 - slot], near_v.at[1 - slot],
                              near_sem, 2 * (1 - slot)):
            c.start()

    valid = _col_ids((tq, tq)) < _row_ids((tq, tq))
    trid = trid_ref[...]
    trip = trip_ref[...]
    q = [q_ref[:, lanes(h)] for h in heads]
    own = [_sb_block(q[h], k_ref[:, lanes(h)], v_ref[:, lanes(h)], trid, None, valid)
           for h in heads]
    for c in block_copies(b, newest, near_k.at[slot], near_v.at[slot], near_sem, 2 * slot):
        c.wait()
    for h in heads:
        acc, carry = own[h]
        out, tot = _sb_block(q[h], near_k[slot, h], near_v[slot, h], trip, carry, None)
        acc_s[h] = acc + out
        carry_s[h] = carry + tot

    def cond(st):
        t, m = st
        return jnp.logical_and(t < n_old, m < SB_DEAD)

    def body(st):
        t, _ = st
        k0 = pl.multiple_of((n_old - 1 - t) * tk, tk)
        copies = block_copies(b, k0, old_k, old_v, old_sem, 0)
        for c in copies:
            c.start()
        for c in copies:
            c.wait()
        for h in heads:
            out, tot = _sb_block(q_ref[:, lanes(h)], old_k[h], old_v[h], trip, carry_s[h], None)
            acc_s[h] += out
            carry_s[h] += tot
        return t + 1, jnp.min(carry_s[...])

    lax.while_loop(cond, body, (jnp.int32(0), jnp.min(carry_s[...])))
    for h in heads:
        o_ref[:, lanes(h)] = acc_s[h].astype(o_ref.dtype)


def _suffix_ones(n):
    return (_row_ids((n, n)) > _col_ids((n, n))).astype(BF16)


SB_BLOCK = 256
SB_QSUB = 4


def _sb_attention(q, k, v, k_past=None, v_past=None):
    B, T, D = q.shape
    H = D // SB_HEAD_DIM
    hd = SB_HEAD_DIM
    full = lambda n: pl.BlockSpec((n, n), lambda b, h, i: (0, 0))
    seq = pl.BlockSpec((None, T, hd), lambda b, h, i: (b, 0, h))
    if k_past is None:
        tb = _tile(T, SB_BLOCK)
        nsub = _tile(T // tb, SB_QSUB)
        tq = nsub * tb
        qspec = pl.BlockSpec((None, tq, hd), lambda b, h, i: (b, i, h))
        return pl.pallas_call(
            functools.partial(_sb_prompt_kernel, nsub=nsub, tb=tb),
            grid=(B, H, T // tq),
            in_specs=[qspec, seq, seq, full(tb)],
            out_specs=qspec,
            out_shape=jax.ShapeDtypeStruct((B, T, D), BF16),
            scratch_shapes=[pltpu.VMEM((nsub, tb, hd), F32), pltpu.VMEM((nsub, tb, 1), F32)],
            compiler_params=_params(("parallel", "parallel", "arbitrary"), 0),
            name="sb_attention",
        )(q, k, v, _suffix_ones(tb))
    P = k_past.shape[1]
    tk = _tile(P, SB_BLOCK)
    row = pl.BlockSpec((None, T, D), lambda b: (b, 0, 0))
    tri = lambda n: pl.BlockSpec((n, n), lambda b: (0, 0))
    hbm = pl.BlockSpec(memory_space=pl.ANY)
    return pl.pallas_call(
        functools.partial(_sb_decode_kernel, tq=T, tk=tk, n_heads=H, n_old=P // tk - 1),
        grid=(B,),
        in_specs=[row, row, row, hbm, hbm, tri(T), tri(tk)],
        out_specs=row,
        out_shape=jax.ShapeDtypeStruct((B, T, D), BF16),
        scratch_shapes=[pltpu.VMEM((H, T, hd), F32), pltpu.VMEM((H, T, 1), F32),
                        pltpu.VMEM((2, H, tk, hd), F32), pltpu.VMEM((2, H, tk, hd), F32),
                        pltpu.SemaphoreType.DMA((4,)),
                        pltpu.VMEM((H, tk, hd), F32), pltpu.VMEM((H, tk, hd), F32),
                        pltpu.SemaphoreType.DMA((2,))],
        compiler_params=_params(("arbitrary",), 0),
        name="sb_attention_decode",
    )(q, k, v, k_past, v_past, _suffix_ones(T), _suffix_ones(tk))


LATE_WEIGHTS = ('rw_wo', 'f_wup', 'f_wdown', 'w_kv', 'sb_wq', 'sb_wo')


def _trunk(x, shift0, wkv0, conv0, k_past, v_past, p, late_f32=None):
    B, T, D = x.shape
    M = B * T
    r, k, v, lw, asig, g, shift = _rw_proj(
        x, shift0[0], p['a_norm_g'][0], p['rw_mu'][0], p['rw_w0'][0], p['rw_a0'][0],
        p['rw_wr'], p['rw_wk'], p['rw_wv'], p['rw_w1'], p['rw_w2'], p['rw_a1'], p['rw_a2'],
        p['rw_g1'], p['rw_g2'])
    cast = () if late_f32 is None else tuple(
        late_f32[n].reshape(-1, late_f32[n].shape[-1]) for n in LATE_WEIGHTS)
    o, s_bd, casted = _wkv(r, lw, k, v, asig, g, _state_to_pairs(wkv0[0]),
                           p['rw_kk'][0], p['rw_ka'][0], p['rw_rk'][0].reshape(-1),
                           p['rw_lnx_g'][0], p['rw_lnx_b'][0], cast=cast)
    if late_f32 is not None:
        for n, w in zip(LATE_WEIGHTS, casted):
            p[n] = w.reshape(late_f32[n].shape)
    x, conv_a = _ffn(x, o, p['rw_wo'], conv0[0], p['f_norm_g'][0], p['f_wup'],
                     p['f_conv_w'][0], p['f_conv_b'][0], p['f_wdown'], p['out_norm_g'],
                     layer=0, final_norm=False)
    k_sh, v_sh, kb, vb, q = _kvq(x.reshape(M, D), p['kv_norm_g'], p['b_norm_g'][0],
                                 p['w_kv'], p['sb_wq'])
    d_att = q.shape[1]
    att = _sb_attention(q.reshape(B, T, d_att), kb.reshape(B, T, d_att), vb.reshape(B, T, d_att),
                        k_past, v_past)
    y, conv_b = _ffn(x, att, p['sb_wo'], conv0[1], p['f_norm_g'][1], p['f_wup'],
                     p['f_conv_w'][1], p['f_conv_b'][1], p['f_wdown'], p['out_norm_g'],
                     layer=1, final_norm=True)
    H = d_att // SB_HEAD_DIM
    return (y, _pairs_to_state(s_bd)[None], shift.reshape(1, B, D), jnp.stack([conv_a, conv_b]),
            k_sh.reshape(B, T, H, SB_HEAD_DIM), v_sh.reshape(B, T, H, SB_HEAD_DIM))


def kernel(x_prompt, x_sample, cache_k, cache_v, state_wkv, state_shift, state_conv, a_norm_g, rw_mu, rw_w0, rw_w1, rw_w2, rw_a0, rw_a1, rw_a2, rw_g1, rw_g2, rw_kk, rw_ka, rw_rk, rw_wr, rw_wk, rw_wv, rw_wo, rw_lnx_g, rw_lnx_b, kv_norm_g, w_kv, b_norm_g, sb_wq, sb_wo, f_norm_g, f_wup, f_conv_w, f_conv_b, f_wdown, out_norm_g):
    bf = lambda w: w.astype(BF16)
    p = dict(a_norm_g=a_norm_g, rw_mu=rw_mu, rw_w0=rw_w0, rw_a0=rw_a0,
             rw_w1=bf(rw_w1[0]), rw_w2=bf(rw_w2[0]), rw_a1=bf(rw_a1[0]), rw_a2=bf(rw_a2[0]),
             rw_g1=bf(rw_g1[0]), rw_g2=bf(rw_g2[0]),
             rw_kk=rw_kk, rw_ka=rw_ka, rw_rk=rw_rk,
             rw_wr=bf(rw_wr[0]), rw_wk=bf(rw_wk[0]), rw_wv=bf(rw_wv[0]),
             rw_lnx_g=rw_lnx_g, rw_lnx_b=rw_lnx_b, kv_norm_g=kv_norm_g,
             b_norm_g=b_norm_g, f_norm_g=f_norm_g,
             f_conv_w=f_conv_w, f_conv_b=f_conv_b, out_norm_g=out_norm_g)
    late_f32 = dict(rw_wo=rw_wo[0], f_wup=f_wup, f_wdown=f_wdown, w_kv=w_kv,
                    sb_wq=sb_wq[0], sb_wo=sb_wo[0])
    B, _, D = x_prompt.shape
    n_a = state_shift.shape[0]
    depth = state_conv.shape[0]
    F = state_conv.shape[-1]
    H = state_wkv.shape[2]
    shift0 = jnp.zeros((n_a, B, D), x_prompt.dtype)
    wkv0 = jnp.zeros((n_a, B, H, RW_HEAD, RW_HEAD), F32)
    conv0 = jnp.zeros((depth, B, CONV_W - 1, F), x_prompt.dtype)
    y_p, wkv_p, shift_p, conv_p, k_p, v_p = _trunk(x_prompt, shift0, wkv0, conv0, None, None, p,
                                                   late_f32=late_f32)
    y_s, wkv_s, shift_s, conv_s, k_s, v_s = _trunk(
        x_sample, state_shift, state_wkv.astype(F32), state_conv, cache_k, cache_v, p)
    return (y_p, y_s, wkv_p.astype(state_wkv.dtype), shift_p, conv_p, k_p, v_p,
            wkv_s.astype(state_wkv.dtype), shift_s, conv_s, k_s, v_s)
```

```python
import functools

import jax
import jax.numpy as jnp
from jax import lax
from jax.experimental import pallas as pl
from jax.experimental.pallas import tpu as pltpu

F32 = jnp.float32
BF16 = jnp.bfloat16

RW_HEAD = 64
PAIR = 2 * RW_HEAD
SB_HEAD_DIM = 128
GN_EPS = 64e-5
NORM_EPS = 1e-6
CONV_W = 3
RW_PROJ_DTYPES = (BF16, BF16, BF16, F32, BF16, BF16)
BF16_ROWS = 16
WKV_CHUNK = 64
NEUMANN_BLOCK = 16
WKV_CHUNKS_PER_STEP = 2
LOG2_E = 1.4426950408889634
ROW_TILE = 512
COL_TILE = 512
VMEM_LIMIT = 56 * 1024 * 1024
VMEM_LIMIT_SMALL = 24 * 1024 * 1024


def _tile(n, pref):
    if n <= pref:
        return n
    t = pref
    while n % t:
        t //= 2
    return t


def _row_tiling(B, T):
    if T >= ROW_TILE:
        return B, 1, _tile(T, ROW_TILE)
    nseq = _tile(B, max(ROW_TILE // T, 1))
    return B // nseq, nseq, nseq * T


def _params(sem, rows=None):
    small = rows is not None and rows <= ROW_TILE
    return pltpu.CompilerParams(dimension_semantics=sem,
                                vmem_limit_bytes=VMEM_LIMIT_SMALL if small else VMEM_LIMIT)


def _dot(a, b):
    return jnp.dot(a.astype(BF16), b.astype(BF16), preferred_element_type=F32)


def _dot_nt(a, b):
    return lax.dot_general(a.astype(BF16), b.astype(BF16), (((1,), (1,)), ((), ())),
                           preferred_element_type=F32)


def _dot_tn(a, b):
    return lax.dot_general(a.astype(BF16), b.astype(BF16), (((0,), (0,)), ((), ())),
                           preferred_element_type=F32)


def _rms_scale(x):
    return x * lax.rsqrt(jnp.mean(x * x, axis=-1, keepdims=True) + NORM_EPS)


def _softplus(u):
    return jnp.maximum(u, 0.0) + jnp.log1p(jnp.exp(-jnp.abs(u)))


def _row_ids(shape):
    return lax.broadcasted_iota(jnp.int32, shape, 0)


def _col_ids(shape):
    return lax.broadcasted_iota(jnp.int32, shape, 1)


def _rw_proj_kernel(x_ref, shift0_ref, ng_ref, mu_ref, w0_ref, a0_ref,
                    wr_ref, wk_ref, wv_ref, w1_ref, w2_ref, a1_ref, a2_ref, g1_ref, g2_ref,
                    r_out, k_out, v_out, lw_out, as_out, g_out, shift_out,
                    xr_s, xk_s, xv_s, hw_s, ha_s, hg_s, carry_s, *, nseq, seq_rows):
    t = pl.program_id(1)
    j = pl.program_id(2)

    @pl.when(j == 0)
    def _():
        @pl.when(t == 0)
        def _():
            carry_s[...] = shift0_ref[0]

        xn = _rms_scale(x_ref[...]) * ng_ref[...]
        rows = _row_ids(xn.shape)
        x_prev = jnp.where(rows == 0, carry_s[...], pltpu.roll(xn, 1, axis=0))
        for s in range(1, nseq):
            x_prev = jnp.where(rows == s * seq_rows, shift0_ref[s], x_prev)
        for s in range(nseq):
            shift_out[s] = xn[(s + 1) * seq_rows - 1:(s + 1) * seq_rows, :]
        carry_s[...] = xn[nseq * seq_rows - 1:nseq * seq_rows, :]
        xx = x_prev - xn
        mix = lambda i: (xn + xx * mu_ref[i:i + 1, :]).astype(BF16)
        xr_s[...] = mix(0)
        xk_s[...] = mix(2)
        xv_s[...] = mix(3)
        hw_s[...] = jnp.tanh(_dot(mix(1), w1_ref[...])).astype(BF16)
        ha_s[...] = _dot(mix(4), a1_ref[...]).astype(BF16)
        hg_s[...] = jax.nn.sigmoid(_dot(mix(5), g1_ref[...])).astype(BF16)

    r_out[...] = _dot(xr_s[...], wr_ref[...]).astype(r_out.dtype)
    k_out[...] = _dot(xk_s[...], wk_ref[...]).astype(k_out.dtype)
    v_out[...] = _dot(xv_s[...], wv_ref[...]).astype(v_out.dtype)
    w_log = -_softplus(-(w0_ref[...] + _dot(hw_s[...], w2_ref[...]))) - 0.5
    lw_out[...] = -jnp.exp(w_log)
    as_out[...] = jax.nn.sigmoid(a0_ref[...] + _dot(ha_s[...], a2_ref[...])).astype(as_out.dtype)
    g_out[...] = _dot(hg_s[...], g2_ref[...]).astype(g_out.dtype)


def _rw_proj(x, shift0, ng, mu, w0, a0, wr, wk, wv, w1, w2, a1, a2, g1, g2):
    B, T, D = x.shape
    G, nseq, tm = _row_tiling(B, T)
    R = B * T // G
    tn = _tile(D, COL_TILE)
    lw_dim, la_dim, lg_dim = w1.shape[1], a1.shape[1], g1.shape[1]
    row = lambda b, t, j: (0, 0)
    colv = pl.BlockSpec((1, tn), lambda b, t, j: (0, j))
    act = pl.BlockSpec((None, tm, tn), lambda b, t, j: (b, t, j))
    outs = pl.pallas_call(
        functools.partial(_rw_proj_kernel, nseq=nseq, seq_rows=tm // nseq),
        grid=(G, R // tm, D // tn),
        in_specs=[
            pl.BlockSpec((None, tm, D), lambda b, t, j: (b, t, 0)),
            pl.BlockSpec((nseq, 1, D), lambda b, t, j: (b, 0, 0)),
            pl.BlockSpec((1, D), row),
            pl.BlockSpec((6, D), row),
            colv, colv,
            pl.BlockSpec((D, tn), lambda b, t, j: (0, j)),
            pl.BlockSpec((D, tn), lambda b, t, j: (0, j)),
            pl.BlockSpec((D, tn), lambda b, t, j: (0, j)),
            pl.BlockSpec((D, lw_dim), row),
            pl.BlockSpec((lw_dim, tn), lambda b, t, j: (0, j)),
            pl.BlockSpec((D, la_dim), row),
            pl.BlockSpec((la_dim, tn), lambda b, t, j: (0, j)),
            pl.BlockSpec((D, lg_dim), row),
            pl.BlockSpec((lg_dim, tn), lambda b, t, j: (0, j)),
        ],
        out_specs=[act] * 6 + [pl.BlockSpec((nseq, 1, D), lambda b, t, j: (b, 0, 0))],
        out_shape=[jax.ShapeDtypeStruct((G, R, D), dt) for dt in RW_PROJ_DTYPES]
        + [jax.ShapeDtypeStruct((B, 1, D), F32)],
        scratch_shapes=[
            pltpu.VMEM((tm, D), BF16), pltpu.VMEM((tm, D), BF16), pltpu.VMEM((tm, D), BF16),
            pltpu.VMEM((tm, lw_dim), BF16), pltpu.VMEM((tm, la_dim), BF16),
            pltpu.VMEM((tm, lg_dim), BF16), pltpu.VMEM((1, D), F32),
        ],
        compiler_params=_params(("arbitrary", "arbitrary", "arbitrary"), B * T),
        name="rw_proj",
    )(x.reshape(G, R, D), shift0.reshape(B, 1, D), ng.reshape(1, D), mu, w0.reshape(1, D),
      a0.reshape(1, D), wr, wk, wv, w1, w2, a1, a2, g1, g2)
    return [o.reshape(B, T, D) for o in outs[:6]] + [outs[6]]


def _wkv_kernel(r_ref, lw_ref, k_ref, v_ref, as_ref, g_ref, s0_ref,
                kkw_ref, kaw_ref, rk_ref, lng_ref, lnb_ref, *rest,
                chunk, n_sub, n_pairs, n_cast):
    cast_in = rest[:n_cast]
    o_ref, s_ref = rest[n_cast:n_cast + 2]
    cast_out = rest[n_cast + 2:2 * n_cast + 2]
    c_s = rest[-1]
    C = chunk
    C2 = 2 * C
    trow = lambda sc: slice(sc * C, (sc + 1) * C)

    for w_in, w_out in zip(cast_in, cast_out):
        w_out[...] = w_in[...].astype(w_out.dtype)

    @pl.when(pl.program_id(1) == 0)
    def _():
        s_ref[...] = s0_ref[...]

    tri = (_col_ids((C, C)) <= _row_ids((C, C))).astype(BF16)
    for sc in range(n_sub):
        lw_all = lw_ref[trow(sc), :]
        lw_hi = lw_all.astype(BF16)
        lw_r = lw_all - lw_hi.astype(F32)
        lw_mid = lw_r.astype(BF16)
        lw_lo = (lw_r - lw_mid.astype(F32)).astype(BF16)
        c_s[trow(sc), :] = (jnp.dot(tri, lw_hi, preferred_element_type=F32)
                            + jnp.dot(tri, lw_mid, preferred_element_type=F32)
                            + jnp.dot(tri, lw_lo, preferred_element_type=F32))

    head0 = _col_ids((C, PAIR)) < RW_HEAD
    ones_bd = ((_row_ids((PAIR, PAIR)) // RW_HEAD) == (_col_ids((PAIR, PAIR)) // RW_HEAD))
    ones_bd_bf = ones_bd.astype(BF16)

    rr = _row_ids((C2, C2))
    cc = _col_ids((C2, C2))
    same_head = (rr // C) == (cc // C)
    nb = NEUMANN_BLOCK
    same16 = (rr // nb) == (cc // nb)
    same32 = (rr // (2 * nb)) == (cc // (2 * nb))
    m_e1 = same32 & (~same16) if C >= 2 * nb else None
    m_e2 = same_head & (~same32) if C >= 4 * nb else None
    eye22 = (rr == cc).astype(F32)
    r12 = _row_ids((C, C2))
    c12 = _col_ids((C, C2))
    strict12 = (c12 % C) < r12
    incl12 = (c12 % C) <= r12
    left12 = c12 < C
    same16_c = ((c12 % C) // nb) == (r12 // nb)
    c16 = _col_ids((nb, C2))
    blk_of_lane = (c16 % C) // nb
    eye16 = ((c16 % nb) == _row_ids((nb, C2))).astype(F32)

    def expand(x):
        return jnp.where(same_head, jnp.concatenate([x, x], axis=0), 0.0)

    def expand16(x):
        return jnp.where(same16, jnp.concatenate([x] * (C2 // nb), axis=0), 0.0)

    def group_sum(x):
        return jnp.dot(x.astype(BF16), ones_bd_bf, preferred_element_type=F32)

    cat0 = lambda xs: jnp.concatenate(xs, axis=0)
    zero_h1 = lambda x: jnp.where(head0, x, 0.0)
    zero_h0 = lambda x: jnp.where(head0, 0.0, x)
    inv_n = 1.0 / RW_HEAD

    def state_free(items):
        n = range(len(items))
        tr = [trow(sc) for sc, _ in items]
        sl = [slice(p * PAIR, (p + 1) * PAIR) for _, p in items]
        r = [r_ref[tr[i], sl[i]].astype(F32) for i in n]
        lw = [lw_ref[tr[i], sl[i]] for i in n]
        k = [k_ref[tr[i], sl[i]].astype(F32) for i in n]
        v = [v_ref[tr[i], sl[i]].astype(F32) for i in n]
        asig = [as_ref[tr[i], sl[i]].astype(F32) for i in n]
        c = [c_s[tr[i], sl[i]] for i in n]
        c_end = [x[C - 1:C, :] for x in c]

        kkv = [k[i] * kkw_ref[:, sl[i]] for i in n]
        ss = [group_sum(x * x) for x in kkv]
        kk = [kkv[i] / jnp.maximum(jnp.sqrt(ss[i]), 1e-12) for i in n]
        b_in = [kk[i] * asig[i] for i in n]
        k_in = [k[i] * (1.0 + (asig[i] - 1.0) * kaw_ref[:, sl[i]]) for i in n]
        bonus_s = [group_sum(r[i] * k_in[i] * rk_ref[:, sl[i]]) for i in n]

        e_neg = [jnp.exp(-x) for x in c]
        a_t = [(-kk[i]) * jnp.exp(c[i] - lw[i]) for i in n]
        r_t = [r[i] * jnp.exp(c[i]) for i in n]
        b_h = [b_in[i] * e_neg[i] for i in n]
        k_h = [k_in[i] * e_neg[i] for i in n]
        e_end = [jnp.exp(c_end[i] - c[i]) for i in n]
        RE = [cat0([b_in[i] * e_end[i], k_in[i] * e_end[i]]) for i in n]

        L = [cat0([a_t[i], r_t[i]]) for i in n]
        G = [_dot_nt(L[i], cat0([zero_h1(b_h[i]), zero_h1(k_h[i]),
                                 zero_h0(k_h[i]), zero_h0(b_h[i])])) for i in n]
        G0 = [x[:, :C2] for x in G]
        G1 = [x[:, C2:] for x in G]
        g0t = [x[:C] for x in G0]
        g1t = [x[:C] for x in G1]

        n_c = [jnp.where(strict12, jnp.where(left12, g0t[i], g1t[i]), 0.0) for i in n]
        n_bd = [expand(x) for x in n_c]
        d1 = [sum(jnp.where(blk_of_lane == b, x[b * nb:(b + 1) * nb], 0.0)
                  for b in range(C // nb)) for x in n_c]
        d1_bd = [expand16(x) for x in d1]
        d2 = [_dot(d1[i], d1_bd[i]) for i in n]
        d2_bd = [expand16(x) for x in d2]
        d4 = [_dot(d2[i], d2_bd[i]) for i in n]
        d4_bd = [expand16(x) for x in d4]
        d8 = [_dot(d4[i], d4_bd[i]) for i in n]
        pa = [_dot(eye16 + d1[i], eye22 + d2_bd[i]) for i in n]
        pb = [_dot(eye16 + d4[i], eye22 + expand16(d8[i])) for i in n]
        t16 = [_dot(pa[i], expand16(pb[i])) for i in n]
        tcat = [jnp.where(same16_c, cat0([x] * (C // nb)), 0.0) for x in t16]
        for m_e in (m_e1, m_e2):
            if m_e is not None:
                t_bd = [expand(x) for x in tcat]
                x = [_dot(tcat[i], jnp.where(m_e, n_bd[i], 0.0)) for i in n]
                tcat = [tcat[i] + _dot(x[i], t_bd[i]) for i in n]

        v0 = [jnp.where(head0, x, 0.0) for x in v]
        v1 = [jnp.where(head0, 0.0, x) for x in v]
        av = [_dot(jnp.where(strict12, jnp.where(left12, g1t[i], g0t[i]), 0.0),
                   cat0([v1[i], v0[i]])) for i in n]
        bot = [jnp.concatenate([jnp.where(incl12, G0[i][C:], 0.0),
                                jnp.where(incl12, G1[i][C:], 0.0)], axis=1) for i in n]
        return dict(L=L, tcat=tcat, av=av, bot=bot, RE=RE, v=v, v0=v0, v1=v1,
                    decay=[jnp.exp(x) for x in c_end], bonus=bonus_s)

    def state_chain(sc, pre, sel):
        n = range(len(sel))
        at = lambda name: [pre[name][j] for j in sel]
        L, tcat, av, bot, RE, v, v0, v1, decay, bonus_s = (
            at(x) for x in ("L", "tcat", "av", "bot", "RE", "v", "v0", "v1", "decay", "bonus"))
        sl = [slice(p * PAIR, (p + 1) * PAIR) for p in range(n_pairs)]
        S = [s_ref[p] for p in range(n_pairs)]
        P = [_dot_nt(L[i], S[i]) for i in n]
        W = [P[i][:C] + av[i] for i in n]
        U = [_dot(tcat[i], cat0([zero_h1(W[i]), zero_h0(W[i])])) for i in n]
        Y = [P[i][C:] + _dot(bot[i], cat0([zero_h1(U[i]), v0[i], v1[i], zero_h0(U[i])]))
             for i in n]
        for i in n:
            upd = _dot_tn(cat0([U[i], v[i]]), RE[i])
            s_ref[i] = S[i] * decay[i] + jnp.where(ones_bd, upd, 0.0)

        mean = [group_sum(x) * inv_n for x in Y]
        dlt = [Y[i] - mean[i] for i in n]
        var = [group_sum(x * x) * inv_n for x in dlt]
        for i in n:
            yn = dlt[i] * lax.rsqrt(var[i] + GN_EPS) * lng_ref[:, sl[i]] + lnb_ref[:, sl[i]]
            out = (yn + bonus_s[i] * v[i]) * g_ref[trow(sc), sl[i]].astype(F32)
            o_ref[trow(sc), sl[i]] = out.astype(o_ref.dtype)

    items = [(sc, p) for sc in range(n_sub) for p in range(n_pairs)]
    pre = state_free(items)
    for sc in range(n_sub):
        state_chain(sc, pre, [sc * n_pairs + p for p in range(n_pairs)])


def _cast_slab(rows, n_steps):
    share = 1
    while rows * share % (n_steps * BF16_ROWS):
        share *= 2
    return rows * share // n_steps, share


def _wkv(r, lw, k, v, asig, g, s0_bd, kkw, kaw, rk, lng, lnb, cast=()):
    B, T, D = r.shape
    C = _tile(T, WKV_CHUNK)
    n_sub = _tile(T // C, WKV_CHUNKS_PER_STEP)
    n_pairs = D // PAIR
    steps_t = T // (n_sub * C)
    act = pl.BlockSpec((None, n_sub * C, D), lambda b, t: (b, t, 0))
    vec = pl.BlockSpec((1, D), lambda b, t: (0, 0))
    st = pl.BlockSpec((None, n_pairs, PAIR, PAIR), lambda b, t: (b, 0, 0, 0))
    cast_specs = []
    for w in cast:
        slab, share = _cast_slab(w.shape[0], B * steps_t)
        cast_specs.append(pl.BlockSpec(
            (slab, w.shape[1]), lambda b, t, share=share: ((b * steps_t + t) // share, 0)))
    outs = pl.pallas_call(
        functools.partial(_wkv_kernel, chunk=C, n_sub=n_sub, n_pairs=n_pairs,
                          n_cast=len(cast)),
        grid=(B, steps_t),
        in_specs=[act] * 6 + [st] + [vec] * 5 + cast_specs,
        out_specs=[act, st] + cast_specs,
        out_shape=[jax.ShapeDtypeStruct((B, T, D), BF16),
                   jax.ShapeDtypeStruct((B, n_pairs, PAIR, PAIR), F32)]
        + [jax.ShapeDtypeStruct(w.shape, BF16) for w in cast],
        scratch_shapes=[pltpu.VMEM((n_sub * C, D), F32)],
        compiler_params=_params(("arbitrary", "arbitrary"), B * T),
        name="wkv",
    )(r, lw, k, v, asig, g, s0_bd, kkw.reshape(1, D), kaw.reshape(1, D), rk.reshape(1, D),
      lng.reshape(1, D), lnb.reshape(1, D), *cast)
    return outs[0], outs[1], outs[2:]


def _state_to_pairs(s):
    B, H, n, _ = s.shape
    s = s.reshape(B, H // 2, 2, n, n)
    z = jnp.zeros((B, H // 2, n, n), s.dtype)
    top = jnp.concatenate([s[:, :, 0], z], axis=-1)
    bot = jnp.concatenate([z, s[:, :, 1]], axis=-1)
    return jnp.concatenate([top, bot], axis=-2)


def _pairs_to_state(sp):
    B, P, _, _ = sp.shape
    n = RW_HEAD
    h0 = sp[:, :, :n, :n]
    h1 = sp[:, :, n:, n:]
    return jnp.stack([h0, h1], axis=2).reshape(B, 2 * P, n, n)


def _ffn_kernel(x_ref, a_ref, wo_ref, conv0_ref, ng_ref, wg_ref, wv_ref, cw_ref, cb_ref,
                wd_ref, og_ref, o_ref, conv_out, xn_s, acc_s, carry_s,
                *, final_norm, nseq, seq_rows):
    t = pl.program_id(1)
    f = pl.program_id(2)
    nf = pl.num_programs(2)
    keep = CONV_W - 1

    @pl.when(f == 0)
    def _():
        x1 = x_ref[...] + jnp.dot(a_ref[...], wo_ref[...], preferred_element_type=F32)
        acc_s[...] = x1
        xn_s[...] = (_rms_scale(x1) * ng_ref[...]).astype(BF16)

    @pl.when(t == 0)
    def _():
        carry_s[f] = conv0_ref[0]

    xn = xn_s[...]
    gate = jnp.dot(xn, wg_ref[...], preferred_element_type=F32)
    val = jnp.dot(xn, wv_ref[...], preferred_element_type=F32)
    prev = carry_s[f]
    rows = _row_ids(gate.shape)
    g1 = jnp.where(rows == 0, prev[1:2, :], pltpu.roll(gate, 1, axis=0))
    g2 = jnp.where(rows == 0, prev[0:1, :],
                   jnp.where(rows == 1, prev[1:2, :], pltpu.roll(gate, 2, axis=0)))
    for s in range(1, nseq):
        first = s * seq_rows
        prev_s = conv0_ref[s]
        g1 = jnp.where(rows == first, prev_s[1:2, :], g1)
        g2 = jnp.where(rows == first, prev_s[0:1, :],
                       jnp.where(rows == first + 1, prev_s[1:2, :], g2))
    c = cb_ref[...] + cw_ref[0:1, :] * g2
    c = c + cw_ref[1:2, :] * g1
    c = c + cw_ref[2:3, :] * gate
    y = (c * jax.nn.sigmoid(c)) * val
    acc_s[...] += jnp.dot(y.astype(BF16), wd_ref[...], preferred_element_type=F32)
    tf = gate.shape[1]
    cols = pl.ds(pl.multiple_of(f * tf, tf), tf)
    for s in range(nseq):
        conv_out[s, :, cols] = gate[(s + 1) * seq_rows - keep:(s + 1) * seq_rows, :]
    carry_s[f] = gate[nseq * seq_rows - keep:nseq * seq_rows, :]

    @pl.when(f == nf - 1)
    def _():
        out = acc_s[...]
        if final_norm:
            out = _rms_scale(out) * og_ref[...]
        o_ref[...] = out


def _ffn(x, a, wo, conv0, ng, wup, cw, cb, wdown, og, *, layer, final_norm):
    B, T, D = x.shape
    F = wdown.shape[1]
    Ka = a.shape[-1]
    G, nseq, tm = _row_tiling(B, T)
    R = B * T // G
    tf = _tile(F, COL_TILE)
    nf = F // tf
    keep = CONV_W - 1
    row = lambda b, t, f: (0, 0)
    out, conv = pl.pallas_call(
        functools.partial(_ffn_kernel, final_norm=final_norm, nseq=nseq, seq_rows=tm // nseq),
        grid=(G, R // tm, nf),
        in_specs=[
            pl.BlockSpec((None, tm, D), lambda b, t, f: (b, t, 0)),
            pl.BlockSpec((None, tm, Ka), lambda b, t, f: (b, t, 0)),
            pl.BlockSpec((Ka, D), row, pipeline_mode=pl.Buffered(1)),
            pl.BlockSpec((nseq, keep, tf), lambda b, t, f: (b, 0, f)),
            pl.BlockSpec((1, D), row),
            pl.BlockSpec((None, D, tf), lambda b, t, f: (layer, 0, f)),
            pl.BlockSpec((None, D, tf), lambda b, t, f: (layer, 0, f + nf)),
            pl.BlockSpec((CONV_W, tf), lambda b, t, f: (0, f)),
            pl.BlockSpec((1, tf), lambda b, t, f: (0, f)),
            pl.BlockSpec((None, tf, D), lambda b, t, f: (layer, f, 0)),
            pl.BlockSpec((1, D), row),
        ],
        out_specs=[pl.BlockSpec((None, tm, D), lambda b, t, f: (b, t, 0)),
                   pl.BlockSpec((nseq, keep, F), lambda b, t, f: (b, 0, 0))],
        out_shape=[jax.ShapeDtypeStruct((G, R, D), F32),
                   jax.ShapeDtypeStruct((B, keep, F), F32)],
        scratch_shapes=[pltpu.VMEM((tm, D), BF16), pltpu.VMEM((tm, D), F32),
                        pltpu.VMEM((nf, keep, tf), F32)],
        compiler_params=_params(("arbitrary", "arbitrary", "arbitrary"), B * T),
        name="conv_ffn",
    )(x.reshape(G, R, D), a.reshape(G, R, Ka), wo, conv0, ng.reshape(1, D), wup, wup, cw,
      cb.reshape(1, F), wdown, og.reshape(1, D))
    return out.reshape(B, T, D), conv


def _kvq_kernel(x_ref, gkv_ref, gq_ref, wk_ref, wv_ref, wq_ref,
                k_out, v_out, kb_out, vb_out, q_out, xkv_s, xq_s, *, q_scale):
    @pl.when(pl.program_id(1) == 0)
    def _():
        xh = _rms_scale(x_ref[...])
        xkv_s[...] = (xh * gkv_ref[...]).astype(BF16)
        xq_s[...] = (xh * gq_ref[...]).astype(BF16)

    xkv = xkv_s[...]
    k = jnp.dot(xkv, wk_ref[...], preferred_element_type=F32)
    v = jnp.dot(xkv, wv_ref[...], preferred_element_type=F32)
    k_out[...] = k
    v_out[...] = v
    kb_out[...] = k.astype(BF16)
    vb_out[...] = v.astype(BF16)
    q = jnp.dot(xq_s[...], wq_ref[...], preferred_element_type=F32)
    q_out[...] = (q * q_scale).astype(BF16)


def _kvq(x, gkv, gq, wkv, wq):
    M, D = x.shape
    N = wq.shape[1]
    tm = _tile(M, ROW_TILE)
    tn = _tile(N, COL_TILE)
    nn = N // tn
    row = lambda i, j: (0, 0)
    blk = pl.BlockSpec((tm, tn), lambda i, j: (i, j))
    return pl.pallas_call(
        functools.partial(_kvq_kernel, q_scale=float(SB_HEAD_DIM) ** -0.5 * LOG2_E),
        grid=(M // tm, nn),
        in_specs=[pl.BlockSpec((tm, D), lambda i, j: (i, 0)),
                  pl.BlockSpec((1, D), row), pl.BlockSpec((1, D), row),
                  pl.BlockSpec((D, tn), lambda i, j: (0, j)),
                  pl.BlockSpec((D, tn), lambda i, j: (0, j + nn)),
                  pl.BlockSpec((D, tn), lambda i, j: (0, j))],
        out_specs=[blk] * 5,
        out_shape=[jax.ShapeDtypeStruct((M, N), F32), jax.ShapeDtypeStruct((M, N), F32),
                   jax.ShapeDtypeStruct((M, N), BF16), jax.ShapeDtypeStruct((M, N), BF16),
                   jax.ShapeDtypeStruct((M, N), BF16)],
        scratch_shapes=[pltpu.VMEM((tm, D), BF16), pltpu.VMEM((tm, D), BF16)],
        compiler_params=_params(("parallel", "arbitrary"), M),
        name="kvq_proj",
    )(x, gkv.reshape(1, D), gq.reshape(1, D), wkv, wkv, wq)


SB_DEAD = 152.0


def _sb_block(q, kb, vb, tri, carry, valid, on=None):
    z = _dot_nt(q, kb)
    sp = jnp.maximum(z, 0.0) + jnp.log2(1.0 + jnp.exp2(jnp.minimum(z, -z)))
    if valid is not None:
        sp = jnp.where(valid, sp, 0.0)
    if on is not None:
        sp = jnp.where(on, sp, 0.0)
    later = jnp.dot(sp.astype(BF16), tri, preferred_element_type=F32)
    logw = z - sp - later
    if carry is not None:
        logw = logw - carry
    w = jnp.exp2(logw)
    if valid is not None:
        w = jnp.where(valid, w, 0.0)
    if on is not None:
        w = jnp.where(on, w, 0.0)
    out = jnp.dot(w.astype(BF16), vb.astype(BF16), preferred_element_type=F32)
    return out, later[:, 0:1] + sp[:, 0:1]


def _sb_prompt_kernel(q_ref, k_ref, v_ref, tri_ref, o_ref, acc_s, carry_s, *, nsub, tb):
    qi = pl.program_id(2)
    blk0 = qi * nsub
    tri = tri_ref[...]
    valid = _col_ids((tb, tb)) < _row_ids((tb, tb))
    rows = lambda i: slice(i * tb, (i + 1) * tb)

    def kv_block(j):
        k0 = pl.multiple_of(j * tb, tb)
        return k_ref[pl.ds(k0, tb), :], v_ref[pl.ds(k0, tb), :]

    kvs = [kv_block(blk0 + i) for i in range(nsub)]
    acc, car = [], []
    for i in range(nsub):
        out, tot = _sb_block(q_ref[rows(i), :], kvs[i][0], kvs[i][1], tri, None, valid)
        acc.append(out)
        car.append(tot)
    for i in range(nsub):
        if i == 0:
            kb, vb = kv_block(jnp.maximum(blk0 - 1, 0))
            out, tot = _sb_block(q_ref[rows(0), :], kb, vb, tri, car[0], None, on=blk0 > 0)
        else:
            out, tot = _sb_block(q_ref[rows(i), :], kvs[i - 1][0], kvs[i - 1][1], tri,
                                 car[i], None)
        acc_s[i] = acc[i] + out
        carry_s[i] = car[i] + tot

    def live_min(t):
        m = jnp.float32(jnp.inf)
        for i in range(nsub):
            m = jnp.where(blk0 + i - 2 - t >= 0, jnp.minimum(m, jnp.min(carry_s[i])), m)
        return m

    def cond(st):
        t, m = st
        return jnp.logical_and(blk0 + nsub - 3 - t >= 0, m < SB_DEAD)

    def body(st):
        t, _ = st
        for i in range(nsub):
            j = blk0 + i - 2 - t
            kb, vb = kv_block(jnp.maximum(j, 0))
            out, tot = _sb_block(q_ref[rows(i), :], kb, vb, tri, carry_s[i], None, on=j >= 0)
            acc_s[i] += out
            carry_s[i] += tot
        return t + 1, live_min(t + 1)

    lax.while_loop(cond, body, (jnp.int32(0), live_min(0)))
    for i in range(nsub):
        o_ref[rows(i), :] = acc_s[i].astype(o_ref.dtype)


def _sb_decode_kernel(q_ref, k_ref, v_ref, kc_hbm, vc_hbm, trid_ref, trip_ref, o_ref,
                      acc_s, carry_s, near_k, near_v, near_sem, old_k, old_v, old_sem,
                      *, tq, tk, n_heads, n_old):
    b = pl.program_id(0)
    slot = b % 2
    hd = SB_HEAD_DIM
    newest = n_old * tk
    heads = range(n_heads)
    lanes = lambda h: slice(h * hd, (h + 1) * hd)

    def block_copies(row, k0, k_dst, v_dst, sem, first):
        cps = []
        for h in heads:
            cps.append(pltpu.make_async_copy(kc_hbm.at[row, pl.ds(k0, tk), h, :], k_dst.at[h],
                                             sem.at[first]))
            cps.append(pltpu.make_async_copy(vc_hbm.at[row, pl.ds(k0, tk), h, :], v_dst.at[h],
                                             sem.at[first + 1]))
        return cps

    @pl.when(b == 0)
    def _():
        for c in block_copies(0, newest, near_k.at[0], near_v.at[0], near_sem, 0):
            c.start()

    @pl.when(b + 1 < pl.num_programs(0))
    def _():
        for c in block_copies(b + 1, newest, near_k.at[1 - slot], near_v.at[1 - slot],
                              near_sem, 2 * (1 - slot)):
            c.start()

    valid = _col_ids((tq, tq)) < _row_ids((tq, tq))
    trid = trid_ref[...]
    trip = trip_ref[...]
    q = [q_ref[:, lanes(h)] for h in heads]
    own = [_sb_block(q[h], k_ref[:, lanes(h)], v_ref[:, lanes(h)], trid, None, valid)
           for h in heads]
    for c in block_copies(b, newest, near_k.at[slot], near_v.at[slot], near_sem, 2 * slot):
        c.wait()
    for h in heads:
        acc, carry = own[h]
        out, tot = _sb_block(q[h], near_k[slot, h], near_v[slot, h], trip, carry, None)
        acc_s[h] = acc + out
        carry_s[h] = carry + tot

    def cond(st):
        t, m = st
        return jnp.logical_and(t < n_old, m < SB_DEAD)

    def body(st):
        t, _ = st
        k0 = pl.multiple_of((n_old - 1 - t) * tk, tk)
        copies = block_copies(b, k0, old_k, old_v, old_sem, 0)
        for c in copies:
            c.start()
        for c in copies:
            c.wait()
        for h in heads:
            out, tot = _sb_block(q_ref[:, lanes(h)], old_k[h], old_v[h], trip, carry_s[h], None)
            acc_s[h] += out
            carry_s[h] += tot
        return t + 1, jnp.min(carry_s[...])

    lax.while_loop(cond, body, (jnp.int32(0), jnp.min(carry_s[...])))
    for h in heads:
        o_ref[:, lanes(h)] = acc_s[h].astype(o_ref.dtype)


def _suffix_ones(n):
    return (_row_ids((n, n)) > _col_ids((n, n))).astype(BF16)


SB_BLOCK = 256
SB_QSUB = 4


def _sb_attention(q, k, v, k_past=None, v_past=None):
    B, T, D = q.shape
    H = D // SB_HEAD_DIM
    hd = SB_HEAD_DIM
    full = lambda n: pl.BlockSpec((n, n), lambda b, h, i: (0, 0))
    seq = pl.BlockSpec((None, T, hd), lambda b, h, i: (b, 0, h))
    if k_past is None:
        tb = _tile(T, SB_BLOCK)
        nsub = _tile(T // tb, SB_QSUB)
        tq = nsub * tb
        qspec = pl.BlockSpec((None, tq, hd), lambda b, h, i: (b, i, h))
        return pl.pallas_call(
            functools.partial(_sb_prompt_kernel, nsub=nsub, tb=tb),
            grid=(B, H, T // tq),
            in_specs=[qspec, seq, seq, full(tb)],
            out_specs=qspec,
            out_shape=jax.ShapeDtypeStruct((B, T, D), BF16),
            scratch_shapes=[pltpu.VMEM((nsub, tb, hd), F32), pltpu.VMEM((nsub, tb, 1), F32)],
            compiler_params=_params(("parallel", "parallel", "arbitrary"), 0),
            name="sb_attention",
        )(q, k, v, _suffix_ones(tb))
    P = k_past.shape[1]
    tk = _tile(P, SB_BLOCK)
    row = pl.BlockSpec((None, T, D), lambda b: (b, 0, 0))
    tri = lambda n: pl.BlockSpec((n, n), lambda b: (0, 0))
    hbm = pl.BlockSpec(memory_space=pl.ANY)
    return pl.pallas_call(
        functools.partial(_sb_decode_kernel, tq=T, tk=tk, n_heads=H, n_old=P // tk - 1),
        grid=(B,),
        in_specs=[row, row, row, hbm, hbm, tri(T), tri(tk)],
        out_specs=row,
        out_shape=jax.ShapeDtypeStruct((B, T, D), BF16),
        scratch_shapes=[pltpu.VMEM((H, T, hd), F32), pltpu.VMEM((H, T, 1), F32),
                        pltpu.VMEM((2, H, tk, hd), F32), pltpu.VMEM((2, H, tk, hd), F32),
                        pltpu.SemaphoreType.DMA((4,)),
                        pltpu.VMEM((H, tk, hd), F32), pltpu.VMEM((H, tk, hd), F32),
                        pltpu.SemaphoreType.DMA((2,))],
        compiler_params=_params(("arbitrary",), 0),
        name="sb_attention_decode",
    )(q, k, v, k_past, v_past, _suffix_ones(T), _suffix_ones(tk))


LATE_WEIGHTS = ('rw_wo', 'f_wup', 'f_wdown', 'w_kv', 'sb_wq', 'sb_wo')


def _trunk(x, shift0, wkv0, conv0, k_past, v_past, p, late_f32=None):
    B, T, D = x.shape
    M = B * T
    r, k, v, lw, asig, g, shift = _rw_proj(
        x, shift0[0], p['a_norm_g'][0], p['rw_mu'][0], p['rw_w0'][0], p['rw_a0'][0],
        p['rw_wr'], p['rw_wk'], p['rw_wv'], p['rw_w1'], p['rw_w2'], p['rw_a1'], p['rw_a2'],
        p['rw_g1'], p['rw_g2'])
    cast = () if late_f32 is None else tuple(
        late_f32[n].reshape(-1, late_f32[n].shape[-1]) for n in LATE_WEIGHTS)
    o, s_bd, casted = _wkv(r, lw, k, v, asig, g, _state_to_pairs(wkv0[0]),
                           p['rw_kk'][0], p['rw_ka'][0], p['rw_rk'][0].reshape(-1),
                           p['rw_lnx_g'][0], p['rw_lnx_b'][0], cast=cast)
    if late_f32 is not None:
        for n, w in zip(LATE_WEIGHTS, casted):
            p[n] = w.reshape(late_f32[n].shape)
    x, conv_a = _ffn(x, o, p['rw_wo'], conv0[0], p['f_norm_g'][0], p['f_wup'],
                     p['f_conv_w'][0], p['f_conv_b'][0], p['f_wdown'], p['out_norm_g'],
                     layer=0, final_norm=False)
    k_sh, v_sh, kb, vb, q = _kvq(x.reshape(M, D), p['kv_norm_g'], p['b_norm_g'][0],
                                 p['w_kv'], p['sb_wq'])
    d_att = q.shape[1]
    att = _sb_attention(q.reshape(B, T, d_att), kb.reshape(B, T, d_att), vb.reshape(B, T, d_att),
                        k_past, v_past)
    y, conv_b = _ffn(x, att, p['sb_wo'], conv0[1], p['f_norm_g'][1], p['f_wup'],
                     p['f_conv_w'][1], p['f_conv_b'][1], p['f_wdown'], p['out_norm_g'],
                     layer=1, final_norm=True)
    H = d_att // SB_HEAD_DIM
    return (y, _pairs_to_state(s_bd)[None], shift.reshape(1, B, D), jnp.stack([conv_a, conv_b]),
            k_sh.reshape(B, T, H, SB_HEAD_DIM), v_sh.reshape(B, T, H, SB_HEAD_DIM))


def kernel(x_prompt, x_sample, cache_k, cache_v, state_wkv, state_shift, state_conv, a_norm_g, rw_mu, rw_w0, rw_w1, rw_w2, rw_a0, rw_a1, rw_a2, rw_g1, rw_g2, rw_kk, rw_ka, rw_rk, rw_wr, rw_wk, rw_wv, rw_wo, rw_lnx_g, rw_lnx_b, kv_norm_g, w_kv, b_norm_g, sb_wq, sb_wo, f_norm_g, f_wup, f_conv_w, f_conv_b, f_wdown, out_norm_g):
    bf = lambda w: w.astype(BF16)
    p = dict(a_norm_g=a_norm_g, rw_mu=rw_mu, rw_w0=rw_w0, rw_a0=rw_a0,
             rw_w1=bf(rw_w1[0]), rw_w2=bf(rw_w2[0]), rw_a1=bf(rw_a1[0]), rw_a2=bf(rw_a2[0]),
             rw_g1=bf(rw_g1[0]), rw_g2=bf(rw_g2[0]),
             rw_kk=rw_kk, rw_ka=rw_ka, rw_rk=rw_rk,
             rw_wr=bf(rw_wr[0]), rw_wk=bf(rw_wk[0]), rw_wv=bf(rw_wv[0]),
             rw_lnx_g=rw_lnx_g, rw_lnx_b=rw_lnx_b, kv_norm_g=kv_norm_g,
             b_norm_g=b_norm_g, f_norm_g=f_norm_g,
             f_conv_w=f_conv_w, f_conv_b=f_conv_b, out_norm_g=out_norm_g)
    late_f32 = dict(rw_wo=rw_wo[0], f_wup=f_wup, f_wdown=f_wdown, w_kv=w_kv,
                    sb_wq=sb_wq[0], sb_wo=sb_wo[0])
    B, _, D = x_prompt.shape
    n_a = state_shift.shape[0]
    depth = state_conv.shape[0]
    F = state_conv.shape[-1]
    H = state_wkv.shape[2]
    shift0 = jnp.zeros((n_a, B, D), x_prompt.dtype)
    wkv0 = jnp.zeros((n_a, B, H, RW_HEAD, RW_HEAD), F32)
    conv0 = jnp.zeros((depth, B, CONV_W - 1, F), x_prompt.dtype)
    y_p, wkv_p, shift_p, conv_p, k_p, v_p = _trunk(x_prompt, shift0, wkv0, conv0, None, None, p,
                                                   late_f32=late_f32)
    y_s, wkv_s, shift_s, conv_s, k_s, v_s = _trunk(
        x_sample, state_shift, state_wkv.astype(F32), state_conv, cache_k, cache_v, p)
    return (y_p, y_s, wkv_p.astype(state_wkv.dtype), shift_p, conv_p, k_p, v_p,
            wkv_s.astype(state_wkv.dtype), shift_s, conv_s, k_s, v_s)
```
